```python
import math, functools
import jax, jax.numpy as jnp
from jax import lax
import numpy as np

D_MODEL = 1024
BATCH = 4
SEQ = 4096
DEPTH = 2

GRID_W = 64
CTX_LEN = 256

GLA_HEADS = 4
GLA_DK = D_MODEL // (2 * GLA_HEADS)
GLA_DV = D_MODEL // GLA_HEADS
GLA_QK = GLA_HEADS * GLA_DK
GLA_V = GLA_HEADS * GLA_DV
GLA_GATE_RANK = 16
GLA_GATE_NORM = 16.0
GLA_CHUNK = 64

RWKV_HEAD = 64
RWKV_HEADS = D_MODEL // RWKV_HEAD
RWKV_DIM = RWKV_HEADS * RWKV_HEAD
RWKV_W_RANK = 64
RWKV_A_RANK = 64
RWKV_G_RANK = 128
RWKV_W_SCALE = math.exp(-0.5)
RWKV_LN_EPS = 64e-5

S5_GROUP_CH = 16
S5_GROUPS = D_MODEL // S5_GROUP_CH
S5_STATE = 64
S5_DIM = S5_GROUPS * S5_GROUP_CH

N_EXPERTS = 32
TOP_K = 4
D_EXPERT = D_MODEL
SWIGLU_LIMIT = 7.0
SWIGLU_ALPHA = 1.702

NORM_EPS = 1e-6

GLA_CONV = 2 * GLA_QK + GLA_V
CONV_CH = GLA_CONV + 3 * RWKV_DIM
IN_SIZES = (CONV_CH, GLA_V, 2 * GLA_GATE_RANK, RWKV_G_RANK, 2 * RWKV_W_RANK, 2 * RWKV_A_RANK, S5_DIM, 3 * D_MODEL)
D_IN = CONV_CH + GLA_V + 2 * GLA_GATE_RANK + RWKV_G_RANK + 2 * RWKV_W_RANK + 2 * RWKV_A_RANK + S5_DIM + 3 * D_MODEL

kernel_name = "hybrid_gla_rwkv7_s5_moe_prefix_dit"

F32 = jnp.float32


def _split(a, sizes):
    out, o = [], 0
    for s in sizes:
        out.append(a[..., o:o + s])
        o += s
    return out


def _rmsnorm(x, g, eps=NORM_EPS):
    xf = x.astype(F32)
    y = xf * lax.rsqrt(jnp.mean(xf * xf, axis=-1, keepdims=True) + eps) * g.astype(F32)
    return y.astype(x.dtype)


def _modulate(h, shift, scale):
    return h * (1.0 + scale) + shift


def _bidir_prefix(run_f, run_b, xs_f, xs_b, s0, n_ctx):
    def one_direction(run, xs, reverse):
        ctx_xs = [a[:, :n_ctx] for a in xs]
        lat_xs = [a[:, n_ctx:] for a in xs]
        if reverse:
            ctx_xs = [jnp.flip(a, axis=1) for a in ctx_xs]
            lat_xs = [jnp.flip(a, axis=1) for a in lat_xs]
        y_ctx, s_ctx = run(*ctx_xs, s0)
        y_lat, _ = run(*lat_xs, s_ctx)
        if reverse:
            y_ctx, y_lat = jnp.flip(y_ctx, axis=1), jnp.flip(y_lat, axis=1)
        return jnp.concatenate([y_ctx, y_lat], axis=1)
    return one_direction(run_f, xs_f, False) + one_direction(run_b, xs_b, True)


def _short_conv(pc, n_ctx, rows, conv_w):
    bn, t, ch = pc.shape
    w = conv_w.astype(pc.dtype)
    lat = pc[:, n_ctx:].reshape(bn, rows, GRID_W, ch)
    lat = lax.conv_general_dilated(lat, w[:, :, None, :], (1, 1), 'SAME',
                                   dimension_numbers=('NHWC', 'HWIO', 'NHWC'), feature_group_count=ch)
    cx = lax.conv_general_dilated(pc[:, :n_ctx], w[1][:, None, :], (1,), 'SAME',
                                  dimension_numbers=('NWC', 'WIO', 'NWC'), feature_group_count=ch)
    return jnp.concatenate([cx, lat.reshape(bn, t - n_ctx, ch)], axis=1)


def _gla_chunked(q, k, v, log_a, s0):
    bn, L, H, dk = q.shape
    dv = v.shape[-1]
    n = L // GLA_CHUNK

    def chunks(a):
        return a.astype(F32).reshape(bn, n, GLA_CHUNK, H, a.shape[-1]).transpose(1, 0, 3, 2, 4)

    qc, kc, vc, gc = chunks(q), chunks(k), chunks(v), chunks(log_a)
    b = jnp.cumsum(gc, axis=3)
    b_last = b[:, :, :, -1:]
    q_dec = qc * jnp.exp(b)
    k_inv = kc * jnp.exp(-b)
    k_end = kc * jnp.exp(b_last - b)
    mask = jnp.tril(jnp.ones((GLA_CHUNK, GLA_CHUNK), dtype=bool))
    scores = jnp.where(mask, jnp.einsum('nbhik,nbhjk->nbhij', q_dec, k_inv), 0.0)
    o_intra = jnp.einsum('nbhij,nbhjv->nbhiv', scores, vc)

    def step(S, inp):
        q_n, k_n, v_n, dl = inp
        o_n = jnp.einsum('bhik,bhkv->bhiv', q_n, S)
        S = jnp.exp(dl)[:, :, 0, :, None] * S + jnp.einsum('bhjk,bhjv->bhkv', k_n, v_n)
        return S, o_n

    S, o_inter = lax.scan(step, s0, (q_dec, k_end, vc, b_last))
    o = (o_intra + o_inter).transpose(1, 0, 3, 2, 4).reshape(bn, L, H, dv)
    return o, S


def _gla_branch(qkv, p_og, p_gate, n_ctx, wg2, bg, norm_g):
    bn, t, _ = qkv.shape
    q, k, v = _split(jax.nn.silu(qkv.astype(F32)), (GLA_QK, GLA_QK, GLA_V))
    q = q.reshape(bn, t, GLA_HEADS, GLA_DK) * (GLA_DK ** -0.5)
    k = k.reshape(bn, t, GLA_HEADS, GLA_DK)
    v = v.reshape(bn, t, GLA_HEADS, GLA_DV)
    xs = []
    for d in range(2):
        z = p_gate[..., d * GLA_GATE_RANK:(d + 1) * GLA_GATE_RANK] @ wg2[d] + bg[d]
        log_a = jax.nn.log_sigmoid(z.astype(F32)).reshape(bn, t, GLA_HEADS, GLA_DK) / GLA_GATE_NORM
        xs.append((q, k, v, log_a))
    s0 = jnp.zeros((bn, GLA_HEADS, GLA_DK, GLA_DV), F32)
    o = _bidir_prefix(_gla_chunked, _gla_chunked, xs[0], xs[1], s0, n_ctx)
    o = _rmsnorm(o, norm_g, 1e-5).reshape(bn, t, GLA_V)
    return (o * jax.nn.silu(p_og.astype(F32))).astype(qkv.dtype)


def _rwkv7_scan(r, logw, k, v, kk, kka, s0):
    def step(S, inp):
        r_t, w_t, k_t, v_t, kk_t, kka_t = inp
        S = (S * w_t[:, :, None, :]
             - jnp.einsum('bhvk,bhk->bhv', S, kk_t)[..., None] * kka_t[:, :, None, :]
             + v_t[..., None] * k_t[:, :, None, :])
        return S, jnp.einsum('bhvk,bhk->bhv', S, r_t)
    xs = tuple(jnp.moveaxis(a.astype(F32), 1, 0) for a in (r, jnp.exp(logw), k, v, kk, kka))
    S, ys = lax.scan(step, s0, xs)
    return jnp.moveaxis(ys, 0, 1), S


def _rwkv7_branch(rkv, p_g, p_w, p_a, n_ctx, w0, w2, a0, a2, g2, k_k, k_a, r_k, ln_g, ln_b):
    bn, t, _ = rkv.shape
    H, N = RWKV_HEADS, RWKV_HEAD
    r, k, v = (a.astype(F32).reshape(bn, t, H, N) for a in _split(rkv, (RWKV_DIM,) * 3))
    kk = k * k_k.astype(F32).reshape(H, N)
    kk = kk / jnp.maximum(jnp.sqrt(jnp.sum(kk * kk, axis=-1, keepdims=True)), 1e-12)
    g = (jax.nn.sigmoid(p_g) @ g2).astype(F32)
    xs, bonus = [], []
    for d in range(2):
        lw = w0[d] + jnp.tanh(p_w[..., d * RWKV_W_RANK:(d + 1) * RWKV_W_RANK]) @ w2[d]
        logw = (-RWKV_W_SCALE * jax.nn.sigmoid(lw.astype(F32))).reshape(bn, t, H, N)
        la = a0[d] + p_a[..., d * RWKV_A_RANK:(d + 1) * RWKV_A_RANK] @ a2[d]
        a = jax.nn.sigmoid(la.astype(F32)).reshape(bn, t, H, N)
        kd = k * (1.0 + (a - 1.0) * k_a.astype(F32).reshape(H, N))
        xs.append((r, logw, kd, v, kk, kk * a))
        bonus.append(jnp.sum(r * kd * r_k.astype(F32), axis=-1, keepdims=True) * v)
    s0 = jnp.zeros((bn, H, N, N), F32)
    y = _bidir_prefix(_rwkv7_scan, _rwkv7_scan, xs[0], xs[1], s0, n_ctx)
    mu = jnp.mean(y, axis=-1, keepdims=True)
    var = jnp.mean(jnp.square(y - mu), axis=-1, keepdims=True)
    y = (y - mu) * lax.rsqrt(var + RWKV_LN_EPS) * ln_g.astype(F32).reshape(H, N) + ln_b.astype(F32).reshape(H, N)
    y = y + bonus[0] + bonus[1]
    return (y.reshape(bn, t, RWKV_DIM) * g).astype(rkv.dtype)


def _complex_affine_combine(e1, e2):
    ar1, ai1, br1, bi1 = e1
    ar2, ai2, br2, bi2 = e2
    return (ar2 * ar1 - ai2 * ai1, ar2 * ai1 + ai2 * ar1,
            ar2 * br1 - ai2 * bi1 + br2, ar2 * bi1 + ai2 * br1 + bi2)


def _s5_scan(u, s0, a_r, a_i, bb_r, bb_i, c_r, c_i):
    L = u.shape[1]
    bu_r = jnp.einsum('gpc,blgc->lbgp', bb_r, u)
    bu_i = jnp.einsum('gpc,blgc->lbgp', bb_i, u)
    s_r, s_i = s0
    bu_r = bu_r.at[0].add(a_r * s_r - a_i * s_i)
    bu_i = bu_i.at[0].add(a_r * s_i + a_i * s_r)
    ar = jnp.broadcast_to(a_r, (L, 1) + a_r.shape)
    ai = jnp.broadcast_to(a_i, (L, 1) + a_i.shape)
    _, _, xr, xi = lax.associative_scan(_complex_affine_combine, (ar, ai, bu_r, bu_i), axis=0)
    y = jnp.einsum('gcp,lbgp->blgc', c_r, xr) - jnp.einsum('gcp,lbgp->blgc', c_i, xi)
    return y, (xr[-1], xi[-1])


def _s5_branch(u, n_ctx, lam_re, lam_im, log_dt, b_re, b_im, c_re, c_im, d_skip, glu_w, glu_b):
    bn, t, _ = u.shape
    uf = u.astype(F32)
    ug = uf.reshape(bn, t, S5_GROUPS, S5_GROUP_CH)
    lr, li = lam_re.astype(F32), lam_im.astype(F32)
    br, bi = b_re.astype(F32), b_im.astype(F32)
    den = lr * lr + li * li
    runs = []
    for d in range(2):
        dt = jnp.exp(log_dt[d].astype(F32))[:, None]
        mag = jnp.exp(lr * dt)
        ar, ai = mag * jnp.cos(li * dt), mag * jnp.sin(li * dt)
        fr = ((ar - 1.0) * lr + ai * li) / den
        fi = (ai * lr - (ar - 1.0) * li) / den
        bb_r = fr[..., None] * br - fi[..., None] * bi
        bb_i = fr[..., None] * bi + fi[..., None] * br
        runs.append(functools.partial(_s5_scan, a_r=ar, a_i=ai, bb_r=bb_r, bb_i=bb_i,
                                      c_r=c_re[d].astype(F32), c_i=c_im[d].astype(F32)))
    z0 = jnp.zeros((bn, S5_GROUPS, S5_STATE), F32)
    y = _bidir_prefix(runs[0], runs[1], (ug,), (ug,), (z0, z0), n_ctx)
    y = y.reshape(bn, t, S5_DIM) + d_skip.astype(F32) * uf
    z = jax.nn.gelu(y)
    return (z * jax.nn.sigmoid(z @ glu_w.astype(F32) + glu_b.astype(F32))).astype(u.dtype)


def _mixer(h, n_ctx, w_in, conv_w, gla_wg2, gla_bg, gla_norm,
           rw_w0, rw_w2, rw_a0, rw_a2, rw_g2, rw_kk, rw_ka, rw_rk, rw_ln_g, rw_ln_b,
           s5_lam_re, s5_lam_im, s5_log_dt, s5_b_re, s5_b_im, s5_c_re, s5_c_im, s5_d, s5_glu_w, s5_glu_b,
           w_branch, w_out):
    t = h.shape[1]
    rows = (t - n_ctx) // GRID_W
    p = h @ w_in
    pc, p_og, p_gg, p_rg, p_rw, p_ra, p_u, p_gates = _split(p, IN_SIZES)
    pc = _short_conv(pc, n_ctx, rows, conv_w)
    y_gla = _gla_branch(pc[..., :GLA_CONV], p_og, p_gg, n_ctx, gla_wg2, gla_bg, gla_norm)
    y_rw = _rwkv7_branch(pc[..., GLA_CONV:], p_rg, p_rw, p_ra, n_ctx, rw_w0, rw_w2, rw_a0, rw_a2,
                         rw_g2, rw_kk, rw_ka, rw_rk, rw_ln_g, rw_ln_b)
    y_s5 = _s5_branch(p_u, n_ctx, s5_lam_re, s5_lam_im, s5_log_dt, s5_b_re, s5_b_im,
                      s5_c_re, s5_c_im, s5_d, s5_glu_w, s5_glu_b)
    ga, gb, gc = _split(jax.nn.sigmoid(p_gates), (D_MODEL,) * 3)
    merged = ga * (y_gla @ w_branch[0]) + gb * (y_rw @ w_branch[1]) + gc * (y_s5 @ w_branch[2])
    return (merged @ w_out).astype(h.dtype)


def _moe(h, router_w, router_b, w_gate, b_gate, w_up, b_up, w_down, b_down):
    logits = (h @ router_w + router_b).astype(F32)
    top_v, top_i = lax.top_k(logits, TOP_K)
    probs = jax.nn.softmax(top_v, axis=-1)
    comb = jnp.sum(jax.nn.one_hot(top_i, N_EXPERTS, dtype=F32) * probs[..., None], axis=-2)
    out = jnp.zeros(h.shape, F32)
    for e in range(N_EXPERTS):
        gl = jnp.minimum((h @ w_gate[e] + b_gate[e]).astype(F32), SWIGLU_LIMIT)
        li = jnp.clip((h @ w_up[e] + b_up[e]).astype(F32), -SWIGLU_LIMIT, SWIGLU_LIMIT)
        act = gl * jax.nn.sigmoid(SWIGLU_ALPHA * gl) * (li + 1.0)
        out = out + comb[..., e:e + 1] * (act.astype(h.dtype) @ w_down[e] + b_down[e])
    return out.astype(h.dtype)


def setup_inputs(seed: int = 0) -> dict:
    key = jax.random.key(seed)
    ks = iter(jax.random.split(key, 64))

    def nrm(shape, scale):
        return scale * jax.random.normal(next(ks), shape, F32)

    L, D, G, P, GC = DEPTH, D_MODEL, S5_GROUPS, S5_STATE, S5_GROUP_CH
    return {
        "x": nrm((BATCH, SEQ, D), 1.0),
        "c": nrm((BATCH, D), 1.0),
        "ctx": nrm((BATCH, CTX_LEN, D), 1.0),
        "c_ctx": nrm((D,), 1.0),
        "w_ada": nrm((L, D, 6 * D), 0.5 * D ** -0.5),
        "b_ada": nrm((L, 6 * D), 0.01),
        "norm_mix": 1.0 + nrm((L, D), 0.02),
        "w_in": nrm((L, D, D_IN), D ** -0.5),
        "conv_w": nrm((L, 3, 3, CONV_CH), 1.0 / 3.0),
        "gla_wg2": nrm((L, 2, GLA_GATE_RANK, GLA_QK), GLA_GATE_RANK ** -0.5),
        "gla_bg": 1.0 + nrm((L, 2, GLA_QK), 0.5),
        "gla_norm": 1.0 + nrm((L, GLA_DV), 0.02),
        "rw_w0": nrm((L, 2, RWKV_DIM), 0.5),
        "rw_w2": nrm((L, 2, RWKV_W_RANK, RWKV_DIM), 0.5 * RWKV_W_RANK ** -0.5),
        "rw_a0": nrm((L, 2, RWKV_DIM), 0.1),
        "rw_a2": nrm((L, 2, RWKV_A_RANK, RWKV_DIM), 0.5 * RWKV_A_RANK ** -0.5),
        "rw_g2": nrm((L, RWKV_G_RANK, RWKV_DIM), RWKV_G_RANK ** -0.5),
        "rw_kk": 0.85 + nrm((L, RWKV_DIM), 0.02),
        "rw_ka": 1.0 + nrm((L, RWKV_DIM), 0.02),
        "rw_rk": nrm((L, RWKV_HEADS, RWKV_HEAD), 0.1),
        "rw_ln_g": 1.0 + nrm((L, RWKV_DIM), 0.02),
        "rw_ln_b": nrm((L, RWKV_DIM), 0.01),
        "s5_lam_re": -0.5 + nrm((L, G, P), 0.01),
        "s5_lam_im": jnp.pi * jnp.arange(P, dtype=F32) + nrm((L, G, P), 0.01),
        "s5_log_dt": jax.random.uniform(next(ks), (L, 2, G), F32, minval=math.log(1e-3), maxval=math.log(1e-1)),
        "s5_b_re": nrm((L, G, P, GC), (2 * GC) ** -0.5),
        "s5_b_im": nrm((L, G, P, GC), (2 * GC) ** -0.5),
        "s5_c_re": nrm((L, 2, G, GC, P), (2 * P) ** -0.5),
        "s5_c_im": nrm((L, 2, G, GC, P), (2 * P) ** -0.5),
        "s5_d": nrm((L, S5_DIM), 0.5),
        "s5_glu_w": nrm((L, S5_DIM, S5_DIM), S5_DIM ** -0.5),
        "s5_glu_b": nrm((L, S5_DIM), 0.01),
        "w_branch": nrm((L, 3, D, D), D ** -0.5),
        "w_out": nrm((L, D, D), D ** -0.5),
        "norm_ffn": 1.0 + nrm((L, D), 0.02),
        "router_w": nrm((L, D, N_EXPERTS), D ** -0.5),
        "router_b": nrm((L, N_EXPERTS), 0.01),
        "exp_w_gate": nrm((L, N_EXPERTS, D, D_EXPERT), D ** -0.5),
        "exp_b_gate": nrm((L, N_EXPERTS, D_EXPERT), 0.01),
        "exp_w_up": nrm((L, N_EXPERTS, D, D_EXPERT), D ** -0.5),
        "exp_b_up": nrm((L, N_EXPERTS, D_EXPERT), 0.01),
        "exp_w_down": nrm((L, N_EXPERTS, D_EXPERT, D), D_EXPERT ** -0.5),
        "exp_b_down": nrm((L, N_EXPERTS, D), 0.01),
        "final_norm": 1.0 + nrm((D,), 0.02),
    }


def reference(x, c, ctx, c_ctx, w_ada, b_ada, norm_mix, w_in, conv_w, gla_wg2, gla_bg, gla_norm,
              rw_w0, rw_w2, rw_a0, rw_a2, rw_g2, rw_kk, rw_ka, rw_rk, rw_ln_g, rw_ln_b,
              s5_lam_re, s5_lam_im, s5_log_dt, s5_b_re, s5_b_im, s5_c_re, s5_c_im, s5_d, s5_glu_w, s5_glu_b,
              w_branch, w_out, norm_ffn, router_w, router_b,
              exp_w_gate, exp_b_gate, exp_w_up, exp_b_up, exp_w_down, exp_b_down, final_norm):
    n_ctx = ctx.shape[1]
    for l in range(DEPTH):
        last = l == DEPTH - 1
        mod = _split(jax.nn.silu(c) @ w_ada[l] + b_ada[l], (D_MODEL,) * 6)
        mod_c = _split(jax.nn.silu(c_ctx) @ w_ada[l] + b_ada[l], (D_MODEL,) * 6)
        sh1, sc1, g1, sh2, sc2, g2 = (m[:, None, :] for m in mod)
        csh1, csc1, cg1, csh2, csc2, cg2 = mod_c
        h = jnp.concatenate([_modulate(_rmsnorm(ctx, norm_mix[l]), csh1, csc1),
                             _modulate(_rmsnorm(x, norm_mix[l]), sh1, sc1)], axis=1)
        mix = _mixer(h, n_ctx, w_in[l], conv_w[l], gla_wg2[l], gla_bg[l], gla_norm[l],
                     rw_w0[l], rw_w2[l], rw_a0[l], rw_a2[l], rw_g2[l], rw_kk[l], rw_ka[l], rw_rk[l],
                     rw_ln_g[l], rw_ln_b[l], s5_lam_re[l], s5_lam_im[l], s5_log_dt[l], s5_b_re[l], s5_b_im[l],
                     s5_c_re[l], s5_c_im[l], s5_d[l], s5_glu_w[l], s5_glu_b[l], w_branch[l], w_out[l])
        x = x + g1 * mix[:, n_ctx:]
        moe = functools.partial(_moe, router_w=router_w[l], router_b=router_b[l],
                                w_gate=exp_w_gate[l], b_gate=exp_b_gate[l], w_up=exp_w_up[l], b_up=exp_b_up[l],
                                w_down=exp_w_down[l], b_down=exp_b_down[l])
        hx = _modulate(_rmsnorm(x, norm_ffn[l]), sh2, sc2)
        if last:
            x = x + g2 * moe(hx)
        else:
            ctx = ctx + cg1 * mix[:, :n_ctx]
            hc = _modulate(_rmsnorm(ctx, norm_ffn[l]), csh2, csc2)
            y = moe(jnp.concatenate([hc, hx], axis=1))
            ctx = ctx + cg2 * y[:, :n_ctx]
            x = x + g2 * y[:, n_ctx:]
    return _rmsnorm(x, final_norm)
```

```python
import functools
import math

import jax
import jax.numpy as jnp
from jax import lax
from jax.experimental import pallas as pl
from jax.experimental.pallas import tpu as pltpu

F32 = jnp.float32
BF16 = jnp.bfloat16
HI = lax.Precision.HIGHEST

D_MODEL = 1024
GRID_W = 64

GLA_HEADS = 4
GLA_DK = 128
GLA_DV = 256
GLA_QK = GLA_HEADS * GLA_DK
GLA_V = GLA_HEADS * GLA_DV
GLA_GATE_RANK = 16
GLA_GATE_NORM = 16.0
CHUNK = 64

RWKV_HEAD = 64
RWKV_DIM = 1024
RWKV_W_RANK = 64
RWKV_A_RANK = 64
RWKV_G_RANK = 128
RWKV_W_SCALE = math.exp(-0.5)
RWKV_LN_EPS = 64e-5
RWKV_LANES = 256
RWKV_HPB = RWKV_LANES // RWKV_HEAD
RWKV_SEG = 256

S5_GROUP_CH = 16
S5_GROUPS = 64
S5_STATE = 64
S5_CHUNK = 16
S5_PAIRS = S5_GROUPS // 2
S5_BPAD = 8

N_EXPERTS = 32
TOP_K = 4
SWIGLU_LIMIT = 7.0
SWIGLU_ALPHA = 1.702
NORM_EPS = 1e-6

GLA_CONV = 2 * GLA_QK + GLA_V
CONV_CH = GLA_CONV + 3 * RWKV_DIM

OFF_PC = 0
OFF_OG = CONV_CH
OFF_U = OFF_OG + GLA_V
OFF_GATES = OFF_U + D_MODEL
OFF_GG = OFF_GATES + 3 * D_MODEL
OFF_RG = OFF_GG + 128
OFF_RW = OFF_RG + 128
OFF_RA = OFF_RW + 128
NP = OFF_RA + 128

VMEM_LIMIT = 56 * 1024 * 1024


def _cparams(sem):
    return pltpu.CompilerParams(dimension_semantics=sem, vmem_limit_bytes=VMEM_LIMIT)


def _row_tile(t, cap):
    best = 8
    for d in range(8, min(t, cap) + 1, 8):
        if t % d == 0:
            best = d
    return best


def _silu(x):
    return x * jax.nn.sigmoid(x)


def _log_sigmoid(z):
    return jnp.minimum(z, 0.0) - jnp.log(1.0 + jnp.exp(-jnp.abs(z)))


def _mod_rows(mod_ref, b, nb, off, is_ctx):
    vb = mod_ref[pl.ds(b, 1), off:off + D_MODEL]
    vc = mod_ref[nb:nb + 1, off:off + D_MODEL]
    return jnp.where(is_ctx, vc, vb)


def _ada_kernel(c_ref, w_ref, b_ref, o_ref):
    a = _silu(c_ref[...])
    o_ref[0] = jnp.dot(a, w_ref[0], precision=HI, preferred_element_type=F32) + b_ref[0]


def _ada_table(cc, w_ada, b_ada):
    depth, d, n = w_ada.shape
    tn = 1536
    return pl.pallas_call(
        _ada_kernel,
        out_shape=jax.ShapeDtypeStruct((depth, cc.shape[0], n), F32),
        grid=(depth, n // tn),
        in_specs=[
            pl.BlockSpec((cc.shape[0], d), lambda l, j: (0, 0)),
            pl.BlockSpec((1, d, tn), lambda l, j: (l, 0, j)),
            pl.BlockSpec((1, 1, tn), lambda l, j: (l, 0, j)),
        ],
        out_specs=pl.BlockSpec((1, cc.shape[0], tn), lambda l, j: (l, 0, j)),
        compiler_params=_cparams(("arbitrary", "arbitrary")),
        name="ada_table",
    )(cc, w_ada, b_ada.reshape(depth, 1, n))


def _norm_mm_kernel(x_ref, nw_ref, mod_ref, w_ref, o_ref, h_ref, *, tm, n_ctx, nb):
    b = pl.program_id(0)
    i = pl.program_id(1)

    @pl.when(pl.program_id(2) == 0)
    def _():
        x = x_ref[0]
        xn = x * lax.rsqrt(jnp.mean(x * x, axis=-1, keepdims=True) + NORM_EPS) * nw_ref[...]
        row = i * tm + lax.broadcasted_iota(jnp.int32, (tm, 1), 0)
        is_ctx = row < n_ctx
        sh = _mod_rows(mod_ref, b, nb, 0, is_ctx)
        sc = _mod_rows(mod_ref, b, nb, D_MODEL, is_ctx)
        h_ref[...] = (xn * (1.0 + sc) + sh).astype(BF16)

    o_ref[0] = jnp.dot(h_ref[...], w_ref[...], preferred_element_type=F32)


def _norm_in_proj(resid, norm_w, mod, w_in_p, n_ctx):
    nb, t, d = resid.shape
    tm = _row_tile(t, 1088)
    tn = 768
    kern = functools.partial(_norm_mm_kernel, tm=tm, n_ctx=n_ctx, nb=nb)
    return pl.pallas_call(
        kern,
        out_shape=jax.ShapeDtypeStruct((nb, t, NP), F32),
        grid=(nb, t // tm, NP // tn),
        in_specs=[
            pl.BlockSpec((1, tm, d), lambda b, i, j: (b, i, 0)),
            pl.BlockSpec((1, d), lambda b, i, j: (0, 0)),
            pl.BlockSpec(mod.shape, lambda b, i, j: (0, 0)),
            pl.BlockSpec((d, tn), lambda b, i, j: (0, j)),
        ],
        out_specs=pl.BlockSpec((1, tm, tn), lambda b, i, j: (b, i, j)),
        scratch_shapes=[pltpu.VMEM((tm, d), BF16)],
        compiler_params=_cparams(("arbitrary", "arbitrary", "arbitrary")),
        name="norm_in_proj",
    )(resid, norm_w.reshape(1, d), mod, w_in_p)


def _conv_kernel(x_ref, w_ref, o_ref, xp_ref, xl_ref, xr_ref, *, n_ctx, rows):
    tc = x_ref.shape[-1]
    gw = GRID_W

    def wrow(k):
        return w_ref[k:k + 1, :]

    xc = x_ref[0, 0:n_ctx, :]
    ci = lax.broadcasted_iota(jnp.int32, (n_ctx, tc), 0)
    left = jnp.where(ci > 0, pltpu.roll(xc, 1, 0), 0.0)
    right = jnp.where(ci < n_ctx - 1, pltpu.roll(xc, n_ctx - 1, 0), 0.0)
    o_ref[0, 0:n_ctx, :] = left * wrow(3) + xc * wrow(4) + right * wrow(5)

    zeros = jnp.zeros((gw, tc), F32)
    for ref in (xp_ref, xl_ref, xr_ref):
        ref[0:gw, :] = zeros
        ref[(rows + 1) * gw:(rows + 2) * gw, :] = zeros
    wi = lax.broadcasted_iota(jnp.int32, (gw, tc), 0)

    def fill(r, carry):
        base = pl.multiple_of(r * gw, gw)
        xrow = x_ref[0, pl.ds(n_ctx + base, gw), :]
        xp_ref[pl.ds(gw + base, gw), :] = xrow
        xl_ref[pl.ds(gw + base, gw), :] = jnp.where(wi > 0, pltpu.roll(xrow, 1, 0), 0.0)
        xr_ref[pl.ds(gw + base, gw), :] = jnp.where(wi < gw - 1, pltpu.roll(xrow, gw - 1, 0), 0.0)
        return carry

    lax.fori_loop(0, rows, fill, 0)

    def comp(r, carry):
        base = pl.multiple_of(r * gw, gw)
        acc = jnp.zeros((gw, tc), F32)
        for dr in range(3):
            sl = pl.ds(base + gw * dr, gw)
            acc = acc + xl_ref[sl, :] * wrow(3 * dr) + xp_ref[sl, :] * wrow(3 * dr + 1) \
                + xr_ref[sl, :] * wrow(3 * dr + 2)
        o_ref[0, pl.ds(n_ctx + base, gw), :] = acc
        return carry

    lax.fori_loop(0, rows, comp, 0)


def _short_conv(p, conv_w, n_ctx):
    nb, t, _ = p.shape
    rows = (t - n_ctx) // GRID_W
    tc = 256
    kern = functools.partial(_conv_kernel, n_ctx=n_ctx, rows=rows)
    pad_rows = (rows + 2) * GRID_W
    return pl.pallas_call(
        kern,
        out_shape=jax.ShapeDtypeStruct((nb, t, CONV_CH), F32),
        grid=(nb, CONV_CH // tc),
        in_specs=[
            pl.BlockSpec((1, t, tc), lambda b, j: (b, 0, j)),
            pl.BlockSpec((9, tc), lambda b, j: (0, j)),
        ],
        out_specs=pl.BlockSpec((1, t, tc), lambda b, j: (b, 0, j)),
        scratch_shapes=[pltpu.VMEM((pad_rows, tc), F32)] * 3,
        compiler_params=_cparams(("arbitrary", "arbitrary")),
        name="short_conv",
    )(p, conv_w.reshape(9, CONV_CH))


def _gla_kernel(q_ref, k_ref, v_ref, gg_ref, og_ref, wg_ref, bg_ref, ng_ref, o_ref, *, n_ctx, t):
    c = CHUNK
    nch = t // c
    nctx = n_ctx // c
    ii = lax.broadcasted_iota(jnp.int32, (c, c), 0)
    jj = lax.broadcasted_iota(jnp.int32, (c, c), 1)
    ones_cv = jnp.ones((c, GLA_DV), F32)
    scale = GLA_DK ** -0.5

    def chunk(ci, s, d):
        rows = pl.ds(pl.multiple_of(ci * c, c), c)
        q = _silu(q_ref[0, rows, :]) * scale
        k = _silu(k_ref[0, rows, :])
        v = _silu(v_ref[0, rows, :])
        gg = gg_ref[0, rows, :][:, GLA_GATE_RANK * d:GLA_GATE_RANK * (d + 1)]
        z = jnp.dot(gg, wg_ref[d], preferred_element_type=F32) + bg_ref[d]
        la = _log_sigmoid(z) * (1.0 / GLA_GATE_NORM)
        keep = (ii >= jj) if d == 0 else (ii <= jj)
        bcum = jnp.dot(keep.astype(F32), la, precision=HI, preferred_element_type=F32)
        blast = bcum[c - 1:c, :] if d == 0 else bcum[0:1, :]
        q_dec = q * jnp.exp(bcum)
        k_inv = k * jnp.exp(-bcum)
        k_end = k * jnp.exp(blast - bcum)
        sc = lax.dot_general(q_dec, k_inv, (((1,), (1,)), ((), ())), preferred_element_type=F32)
        sc = jnp.where(keep, sc, 0.0)
        o = jnp.dot(sc, v, preferred_element_type=F32) + jnp.dot(q_dec, s, preferred_element_type=F32)
        dec = jnp.exp(lax.dot_general(la, ones_cv, (((0,), (0,)), ((), ())), precision=HI,
                                      preferred_element_type=F32))
        s = dec * s + lax.dot_general(k_end, v, (((0,), (0,)), ((), ())), preferred_element_type=F32)
        return rows, o, s

    s0 = jnp.zeros((GLA_DK, GLA_DV), F32)

    def fwd(ci, s):
        rows, o, s = chunk(ci, s, 0)
        o_ref[0, rows, :] = o
        return s

    lax.fori_loop(0, nch, fwd, s0)

    def finish(rows, o):
        o = o_ref[0, rows, :] + o
        o = o * lax.rsqrt(jnp.mean(o * o, axis=-1, keepdims=True) + 1e-5) * ng_ref[...]
        o_ref[0, rows, :] = o * _silu(og_ref[0, rows, :])

    def bwd_ctx(i, s):
        rows, o, s = chunk(nctx - 1 - i, s, 1)
        finish(rows, o)
        return s

    def bwd_lat(i, s):
        rows, o, s = chunk(nch - 1 - i, s, 1)
        finish(rows, o)
        return s

    s = lax.fori_loop(0, nctx, bwd_ctx, s0)
    lax.fori_loop(0, nch - nctx, bwd_lat, s)


def _gla_mixer(pcv, p, wg2, bg, norm_g, n_ctx):
    nb, t, _ = pcv.shape
    kern = functools.partial(_gla_kernel, n_ctx=n_ctx, t=t)
    return pl.pallas_call(
        kern,
        out_shape=jax.ShapeDtypeStruct((nb, t, GLA_V), F32),
        grid=(nb, GLA_HEADS),
        in_specs=[
            pl.BlockSpec((1, t, GLA_DK), lambda b, h: (b, 0, h)),
            pl.BlockSpec((1, t, GLA_DK), lambda b, h: (b, 0, GLA_HEADS + h)),
            pl.BlockSpec((1, t, GLA_DV), lambda b, h: (b, 0, GLA_HEADS + h)),
            pl.BlockSpec((1, t, 128), lambda b, h: (b, 0, OFF_GG // 128)),
            pl.BlockSpec((1, t, GLA_DV), lambda b, h: (b, 0, OFF_OG // GLA_DV + h)),
            pl.BlockSpec((2, GLA_GATE_RANK, GLA_DK), lambda b, h: (0, 0, h)),
            pl.BlockSpec((2, 1, GLA_DK), lambda b, h: (0, 0, h)),
            pl.BlockSpec((1, GLA_DV), lambda b, h: (0, 0)),
        ],
        out_specs=pl.BlockSpec((1, t, GLA_DV), lambda b, h: (b, 0, h)),
        compiler_params=_cparams(("arbitrary", "arbitrary")),
        name="gla_mixer",
    )(pcv, pcv, pcv, p, p, wg2, bg.reshape(2, 1, GLA_QK), norm_g.reshape(1, GLA_DV))


def _bd_mask():
    shift = RWKV_HEAD.bit_length() - 1
    r = lax.broadcasted_iota(jnp.int32, (RWKV_LANES, RWKV_LANES), 0) >> shift
    c = lax.broadcasted_iota(jnp.int32, (RWKV_LANES, RWKV_LANES), 1) >> shift
    return r == c


def _bd(x, mask):
    xb = x.astype(BF16)
    return jnp.where(mask, jnp.concatenate([xb] * RWKV_HPB, axis=0), jnp.zeros((), BF16))


def _head_sum(x, ones_bd):
    hi = x.astype(BF16)
    lo = (x - hi.astype(F32)).astype(BF16)
    return jnp.dot(hi, ones_bd, preferred_element_type=F32) + jnp.dot(lo, ones_bd, preferred_element_type=F32)


def _rwkv_inputs(r_ref, k_ref, rw_ref, ra_ref, w0_ref, w2_ref, a0_ref, a2_ref, kkp_ref, kap_ref, rows, d,
                 ones_bd):
    r = r_ref[0, rows, :]
    k = k_ref[0, rows, :]
    pw = rw_ref[0, rows, :][:, RWKV_W_RANK * d:RWKV_W_RANK * (d + 1)]
    pa = ra_ref[0, rows, :][:, RWKV_A_RANK * d:RWKV_A_RANK * (d + 1)]
    lw = w0_ref[d] + jnp.dot(jnp.tanh(pw), w2_ref[d], preferred_element_type=F32)
    logw = -RWKV_W_SCALE * jax.nn.sigmoid(lw)
    la = a0_ref[d] + jnp.dot(pa, a2_ref[d], preferred_element_type=F32)
    a = jax.nn.sigmoid(la)
    kkv = k * kkp_ref[...]
    nrm = jnp.maximum(jnp.sqrt(_head_sum(kkv * kkv, ones_bd)), 1e-12)
    kk = kkv / nrm
    kd = k * (1.0 + (a - 1.0) * kap_ref[...])
    return r, logw, kd, kk, kk * a


def _rwkv_chunk(st, r, logw, kd, v, kk, kka, d, mask_bd, ones_cl):
    c = CHUNK
    ii = lax.broadcasted_iota(jnp.int32, (c, c), 0)
    jj = lax.broadcasted_iota(jnp.int32, (c, c), 1)
    keep_sq = (ii >= jj) if d == 0 else (ii <= jj)
    ti = lax.broadcasted_iota(jnp.int32, (c, RWKV_LANES), 0)
    si = lax.broadcasted_iota(jnp.int32, (c, RWKV_LANES), 1) & (c - 1)
    incl = (si <= ti) if d == 0 else (si >= ti)
    strict = (si < ti) if d == 0 else (si > ti)
    eye = (si == ti).astype(F32)

    cb = jnp.dot(keep_sq.astype(F32), logw, precision=HI, preferred_element_type=F32)
    cblast = cb[c - 1:c, :] if d == 0 else cb[0:1, :]
    ad = -kk * jnp.exp(cb - logw)
    rd = r * jnp.exp(cb)
    einv = jnp.exp(-cb)
    bi = kka * einv
    ki = kd * einv
    eend = jnp.exp(cblast - cb)
    bend = kka * eend
    kend = kd * eend

    nt = (((1,), (1,)), ((), ()))
    tn = (((0,), (0,)), ((), ()))
    lhs = jnp.concatenate([ad, rd], axis=0).astype(BF16)
    pb = lax.dot_general(lhs, _bd(bi, mask_bd), nt, preferred_element_type=F32)
    pk = lax.dot_general(lhs, _bd(ki, mask_bd), nt, preferred_element_type=F32)
    a_ab = jnp.where(strict, pb[0:c], 0.0)
    r_ab = jnp.where(incl, pb[c:2 * c], 0.0)
    a_ak = jnp.where(strict, pk[0:c], 0.0)
    r_ak = jnp.where(incl, pk[c:2 * c], 0.0)

    def mm(x, y):
        return jnp.dot(x.astype(BF16), _bd(y, mask_bd), preferred_element_type=F32)

    tinv = eye + a_ab
    pw = a_ab
    for _ in range(5):
        pw = mm(pw, pw)
        tinv = tinv + mm(pw, tinv)

    stb = st.astype(BF16)
    vbd = _bd(v, mask_bd)
    x = jnp.dot(ad.astype(BF16), stb, preferred_element_type=F32) \
        + jnp.dot(a_ak.astype(BF16), vbd, preferred_element_type=F32)
    u = mm(tinv, x)
    y = jnp.dot(rd.astype(BF16), stb, preferred_element_type=F32) + mm(r_ab, u) \
        + jnp.dot(r_ak.astype(BF16), vbd, preferred_element_type=F32)
    dec = jnp.exp(lax.dot_general(logw, ones_cl, tn, precision=HI, preferred_element_type=F32))
    upd = lax.dot_general(jnp.concatenate([bend, kend], axis=0).astype(BF16),
                          jnp.concatenate([u, v], axis=0).astype(BF16), tn, preferred_element_type=F32)
    st = dec * st + jnp.where(mask_bd, upd, 0.0)
    return y, st


def _rwkv_kernel(*refs, d, nseg_ctx, final):
    if final:
        (r_ref, k_ref, v_ref, rw_ref, ra_ref, rg_ref, yf_ref, w0_ref, w2_ref, a0_ref, a2_ref, g2_ref,
         kkp_ref, kap_ref, rkp_ref, lng_ref, lnb_ref, o_ref, st_ref) = refs
    else:
        (r_ref, k_ref, v_ref, rw_ref, ra_ref, w0_ref, w2_ref, a0_ref, a2_ref,
         kkp_ref, kap_ref, o_ref, st_ref) = refs
    c = CHUNK
    nchunk = RWKV_SEG // c
    mask_bd = _bd_mask()
    ones_bd = mask_bd.astype(BF16)
    ones_cl = jnp.ones((c, RWKV_LANES), F32)
    seg = pl.program_id(2)

    @pl.when(seg == 0)
    def _():
        st_ref[...] = jnp.zeros_like(st_ref)

    order = range(nchunk) if d == 0 else range(nchunk - 1, -1, -1)
    st = st_ref[...]
    for ci in order:
        rows = slice(ci * c, (ci + 1) * c)
        r, logw, kd, kk, kka = _rwkv_inputs(r_ref, k_ref, rw_ref, ra_ref, w0_ref, w2_ref, a0_ref, a2_ref,
                                            kkp_ref, kap_ref, rows, d, ones_bd)
        v = v_ref[0, rows, :]
        y, st = _rwkv_chunk(st, r, logw, kd, v, kk, kka, d, mask_bd, ones_cl)
        if not final:
            o_ref[0, rows, :] = y
        else:
            y = y + yf_ref[0, rows, :]
            inv_n = 1.0 / RWKV_HEAD
            mu = _head_sum(y, ones_bd) * inv_n
            yc = y - mu
            var = _head_sum(yc * yc, ones_bd) * inv_n
            yn = yc * lax.rsqrt(var + RWKV_LN_EPS) * lng_ref[...] + lnb_ref[...]
            _, _, kd0, _, _ = _rwkv_inputs(r_ref, k_ref, rw_ref, ra_ref, w0_ref, w2_ref, a0_ref, a2_ref,
                                           kkp_ref, kap_ref, rows, 0, ones_bd)
            bonus = _head_sum(r * (kd0 + kd) * rkp_ref[...], ones_bd) * v
            g = jnp.dot(jax.nn.sigmoid(rg_ref[0, rows, :]), g2_ref[...], preferred_element_type=F32)
            o_ref[0, rows, :] = (yn + bonus) * g
    st_ref[...] = st


def _rwkv_pass(pcv, p, yf, prm, d, n_ctx):
    nb, t, _ = pcv.shape
    seg = RWKV_SEG
    nseg = t // seg
    nseg_ctx = n_ctx // seg
    final = yf is not None
    ln = RWKV_LANES
    cb0 = GLA_CONV // ln

    if d == 0:
        def smap(s):
            return s
    else:
        def smap(s):
            return jnp.where(s < nseg_ctx, nseg_ctx - 1 - s, nseg - 1 - (s - nseg_ctx))

    def tok(off_blocks):
        return pl.BlockSpec((1, seg, ln), lambda b, h, s: (b, smap(s), off_blocks + h))

    def small(off):
        return pl.BlockSpec((1, seg, 128), lambda b, h, s: (b, smap(s), off // 128))

    def vec():
        return pl.BlockSpec((1, ln), lambda b, h, s: (0, h))

    in_specs = [tok(cb0), tok(cb0 + RWKV_DIM // ln), tok(cb0 + 2 * RWKV_DIM // ln),
                small(OFF_RW), small(OFF_RA)]
    args = [pcv, pcv, pcv, p, p]
    if final:
        in_specs += [small(OFF_RG), pl.BlockSpec((1, seg, ln), lambda b, h, s: (b, smap(s), h))]
        args += [p, yf]
    in_specs += [pl.BlockSpec((2, 1, ln), lambda b, h, s: (0, 0, h)),
                 pl.BlockSpec((2, RWKV_W_RANK, ln), lambda b, h, s: (0, 0, h)),
                 pl.BlockSpec((2, 1, ln), lambda b, h, s: (0, 0, h)),
                 pl.BlockSpec((2, RWKV_A_RANK, ln), lambda b, h, s: (0, 0, h))]
    args += [prm["w0"].reshape(2, 1, RWKV_DIM), prm["w2"], prm["a0"].reshape(2, 1, RWKV_DIM), prm["a2"]]
    if final:
        in_specs += [pl.BlockSpec((RWKV_G_RANK, ln), lambda b, h, s: (0, h))]
        args += [prm["g2"]]
    in_specs += [vec(), vec()]
    args += [prm["kk"].reshape(1, RWKV_DIM), prm["ka"].reshape(1, RWKV_DIM)]
    if final:
        in_specs += [vec(), vec(), vec()]
        args += [prm["rk"].reshape(1, RWKV_DIM), prm["ln_g"].reshape(1, RWKV_DIM),
                 prm["ln_b"].reshape(1, RWKV_DIM)]
    kern = functools.partial(_rwkv_kernel, d=d, nseg_ctx=nseg_ctx, final=final)
    return pl.pallas_call(
        kern,
        out_shape=jax.ShapeDtypeStruct((nb, t, RWKV_DIM), F32),
        grid=(nb, RWKV_DIM // ln, nseg),
        in_specs=in_specs,
        out_specs=pl.BlockSpec((1, seg, ln), lambda b, h, s: (b, smap(s), h)),
        scratch_shapes=[pltpu.VMEM((ln, ln), F32)],
        compiler_params=_cparams(("arbitrary", "arbitrary", "arbitrary")),
        name="rwkv7_fwd" if d == 0 else "rwkv7_bwd",
    )(*args)


def _s5_operators(lam_re, lam_im, log_dt, b_re, b_im, c_re, c_im):
    g, p_, lc = S5_GROUPS, S5_STATE, S5_CHUNK
    lr, li = lam_re.astype(F32), lam_im.astype(F32)
    br, bi = b_re.astype(F32), b_im.astype(F32)
    den = lr * lr + li * li
    n = jnp.arange(lc + 1, dtype=F32)[:, None, None]
    ein = functools.partial(jnp.einsum, precision=HI)
    out = []
    for d in range(2):
        dt = jnp.exp(log_dt[d].astype(F32))[:, None]
        mag = jnp.exp(n * (lr * dt)[None])
        ang = n * (li * dt)[None]
        pr, pi = mag * jnp.cos(ang), mag * jnp.sin(ang)
        m1 = jnp.exp(lr * dt)
        ar, ai = m1 * jnp.cos(li * dt), m1 * jnp.sin(li * dt)
        fr = ((ar - 1.0) * lr + ai * li) / den
        fi = (ai * lr - (ar - 1.0) * li) / den
        bb_r = fr[..., None] * br - fi[..., None] * bi
        bb_i = fr[..., None] * bi + fi[..., None] * br
        cr, ci = c_re[d].astype(F32), c_im[d].astype(F32)
        ca_r = cr[None] * pr[:, :, None, :] - ci[None] * pi[:, :, None, :]
        ca_i = cr[None] * pi[:, :, None, :] + ci[None] * pr[:, :, None, :]
        kt = ein('tgop,gpc->tgoc', ca_r[:lc], bb_r) - ein('tgop,gpc->tgoc', ca_i[:lc], bb_i)
        s_idx = jnp.arange(lc)[:, None]
        t_idx = jnp.arange(lc)[None, :]
        lag = (t_idx - s_idx) if d == 0 else (s_idx - t_idx)
        valid = lag >= 0
        toep = jnp.where(valid[:, :, None, None, None], kt[jnp.clip(lag, 0, lc - 1)], 0.0)
        toep = toep.transpose(2, 0, 4, 1, 3).reshape(g, lc * S5_GROUP_CH, lc * S5_GROUP_CH)
        e_s = (lc - 1 - jnp.arange(lc)) if d == 0 else jnp.arange(lc)
        ps_r, ps_i = pr[e_s], pi[e_s]
        bst_r = ps_r[..., None] * bb_r[None] - ps_i[..., None] * bb_i[None]
        bst_i = ps_r[..., None] * bb_i[None] + ps_i[..., None] * bb_r[None]
        bst_r = bst_r.transpose(1, 0, 3, 2).reshape(g, lc * S5_GROUP_CH, p_)
        bst_i = bst_i.transpose(1, 0, 3, 2).reshape(g, lc * S5_GROUP_CH, p_)
        f_t = (jnp.arange(lc) + 1) if d == 0 else (lc - jnp.arange(lc))
        co_r = ca_r[f_t].transpose(1, 3, 0, 2).reshape(g, p_, lc * S5_GROUP_CH)
        co_i = (-ca_i[f_t]).transpose(1, 3, 0, 2).reshape(g, p_, lc * S5_GROUP_CH)

        def pair_rows(x):
            x = x.reshape(S5_PAIRS, 2, x.shape[1], x.shape[2])
            z = jnp.zeros_like(x[:, 0])
            top = jnp.concatenate([x[:, 0], z], axis=2)
            bot = jnp.concatenate([z, x[:, 1]], axis=2)
            return jnp.concatenate([top, bot], axis=1)

        def pair_cols(x):
            x = x.reshape(S5_PAIRS, 2, x.shape[1], x.shape[2])
            z = jnp.zeros_like(x[:, 0])
            top = jnp.concatenate([x[:, 0], z], axis=2)
            bot = jnp.concatenate([z, x[:, 1]], axis=2)
            return jnp.concatenate([top, bot], axis=1)

        out.append(dict(
            toep=toep.reshape(S5_PAIRS, 2, 256, 256).astype(BF16),
            bst_r=pair_rows(bst_r).astype(BF16), bst_i=pair_rows(bst_i).astype(BF16),
            co_r=pair_cols(co_r).astype(BF16), co_i=pair_cols(co_i).astype(BF16),
            a_r=pr[lc].reshape(S5_PAIRS, 1, 2 * p_), a_i=pi[lc].reshape(S5_PAIRS, 1, 2 * p_)))
    return out


def _s5_kernel(u_ref, tf_ref, brf_ref, bif_ref, crf_ref, cif_ref, arf_ref, aif_ref,
               tb_ref, brb_ref, bib_ref, crb_ref, cib_ref, arb_ref, aib_ref,
               o_ref, ir_ref, ii_ref, xr_ref, xi_ref, *, nchunk, nchunk_ctx):
    bp = S5_BPAD
    u = u_ref[0].astype(BF16)
    ua, ub = u[:, 0:256], u[:, 256:512]
    for d, (t_ref, br_ref, bi_ref, cr_ref, ci_ref, ar_ref, ai_ref) in enumerate(
            ((tf_ref, brf_ref, bif_ref, crf_ref, cif_ref, arf_ref, aif_ref),
             (tb_ref, brb_ref, bib_ref, crb_ref, cib_ref, arb_ref, aib_ref))):
        ir_ref[...] = jnp.dot(u, br_ref[0], preferred_element_type=F32)
        ii_ref[...] = jnp.dot(u, bi_ref[0], preferred_element_type=F32)
        ar = ar_ref[0]
        ai = ai_ref[0]

        def step(ci, carry):
            sr, si = carry
            rows = pl.ds(pl.multiple_of(ci * bp, bp), bp)
            xr_ref[rows, :] = sr
            xi_ref[rows, :] = si
            nr = ar * sr - ai * si + ir_ref[rows, :]
            ni = ar * si + ai * sr + ii_ref[rows, :]
            return nr, ni

        z = jnp.zeros((bp, 2 * S5_STATE), F32)
        if d == 0:
            lax.fori_loop(0, nchunk, step, (z, z))
        else:
            carry = lax.fori_loop(0, nchunk_ctx, lambda i, cr: step(nchunk_ctx - 1 - i, cr), (z, z))
            lax.fori_loop(0, nchunk - nchunk_ctx, lambda i, cr: step(nchunk - 1 - i, cr), carry)
        y = jnp.concatenate([jnp.dot(ua, t_ref[0, 0], preferred_element_type=F32),
                             jnp.dot(ub, t_ref[0, 1], preferred_element_type=F32)], axis=1)
        y = y + jnp.dot(xr_ref[...].astype(BF16), cr_ref[0], preferred_element_type=F32) \
            + jnp.dot(xi_ref[...].astype(BF16), ci_ref[0], preferred_element_type=F32)
        if d == 0:
            o_ref[0] = y
        else:
            o_ref[0] = o_ref[0] + y


def _s5_mixer(p, ops, n_ctx):
    nb, t, _ = p.shape
    lc = S5_CHUNK
    nchunk = t // lc
    u = p[:, :, OFF_U:OFF_U + D_MODEL].reshape(nb, nchunk, lc, S5_PAIRS, 2, S5_GROUP_CH)
    u = u.transpose(3, 1, 0, 4, 2, 5)
    u = jnp.pad(u, ((0, 0), (0, 0), (0, S5_BPAD - nb), (0, 0), (0, 0), (0, 0)))
    nr = nchunk * S5_BPAD
    u = u.reshape(S5_PAIRS, nr, 512)

    def spec(shape):
        nd = len(shape)
        return pl.BlockSpec((1,) + tuple(shape[1:]), lambda g: (g,) + (0,) * (nd - 1))

    args = [u]
    in_specs = [spec(u.shape)]
    for o in ops:
        for key in ("toep", "bst_r", "bst_i", "co_r", "co_i", "a_r", "a_i"):
            args.append(o[key])
            in_specs.append(spec(o[key].shape))
    kern = functools.partial(_s5_kernel, nchunk=nchunk, nchunk_ctx=n_ctx // lc)
    y = pl.pallas_call(
        kern,
        out_shape=jax.ShapeDtypeStruct((S5_PAIRS, nr, 512), F32),
        grid=(S5_PAIRS,),
        in_specs=in_specs,
        out_specs=spec((S5_PAIRS, nr, 512)),
        scratch_shapes=[pltpu.VMEM((nr, 2 * S5_STATE), F32)] * 4,
        compiler_params=_cparams(("arbitrary",)),
        name="s5_mixer",
    )(*args)
    y = y.reshape(S5_PAIRS, nchunk, S5_BPAD, 2, lc, S5_GROUP_CH)[:, :, :nb]
    return y.transpose(2, 1, 4, 0, 3, 5).reshape(nb, t, D_MODEL)


def _merge_kernel(ygla_ref, yrw_ref, ys5_ref, u_ref, ga_ref, gb_ref, gc_ref, x_ref, mod_ref, wb_ref, wo_ref,
                  d_ref, gw_ref, gbias_ref, o_ref, *, tm, n_ctx, nb):
    b = pl.program_id(0)
    i = pl.program_id(1)
    dot = functools.partial(jnp.dot, preferred_element_type=F32)
    z = jax.nn.gelu(ys5_ref[0] + d_ref[...] * u_ref[0])
    zs = z * jax.nn.sigmoid(dot(z.astype(BF16), gw_ref[...]) + gbias_ref[...])
    m = jax.nn.sigmoid(ga_ref[0]) * dot(ygla_ref[0].astype(BF16), wb_ref[0]) \
        + jax.nn.sigmoid(gb_ref[0]) * dot(yrw_ref[0].astype(BF16), wb_ref[1]) \
        + jax.nn.sigmoid(gc_ref[0]) * dot(zs.astype(BF16), wb_ref[2])
    mix = dot(m.astype(BF16), wo_ref[...])
    row = i * tm + lax.broadcasted_iota(jnp.int32, (tm, 1), 0)
    gate = _mod_rows(mod_ref, b, nb, 2 * D_MODEL, row < n_ctx)
    o_ref[0] = x_ref[0] + gate * mix


def _merge(ygla, yrw, ys5, p, resid, mod, w_branch, w_out, s5_d, glu_w, glu_b, n_ctx):
    nb, t, d = resid.shape
    tm = _row_tile(t, 272)
    kern = functools.partial(_merge_kernel, tm=tm, n_ctx=n_ctx, nb=nb)

    def tok():
        return pl.BlockSpec((1, tm, d), lambda b, i: (b, i, 0))

    return pl.pallas_call(
        kern,
        out_shape=jax.ShapeDtypeStruct((nb, t, d), F32),
        grid=(nb, t // tm),
        in_specs=[
            tok(), tok(), tok(),
            pl.BlockSpec((1, tm, d), lambda b, i: (b, i, OFF_U // d)),
            pl.BlockSpec((1, tm, d), lambda b, i: (b, i, OFF_GATES // d)),
            pl.BlockSpec((1, tm, d), lambda b, i: (b, i, OFF_GATES // d + 1)),
            pl.BlockSpec((1, tm, d), lambda b, i: (b, i, OFF_GATES // d + 2)),
            tok(),
            pl.BlockSpec(mod.shape, lambda b, i: (0, 0)),
            pl.BlockSpec((3, d, d), lambda b, i: (0, 0, 0)),
            pl.BlockSpec((d, d), lambda b, i: (0, 0)),
            pl.BlockSpec((1, d), lambda b, i: (0, 0)),
            pl.BlockSpec((d, d), lambda b, i: (0, 0)),
            pl.BlockSpec((1, d), lambda b, i: (0, 0)),
        ],
        out_specs=tok(),
        compiler_params=_cparams(("arbitrary", "arbitrary")),
        name="merge",
    )(ygla, yrw, ys5, p, p, p, p, resid, mod, w_branch.astype(BF16), w_out.astype(BF16), s5_d.reshape(1, d),
      glu_w.astype(BF16), glu_b.reshape(1, d))


def _router_kernel(x_ref, nw_ref, mod_ref, rw_ref, rb_ref, h_ref, comb_ref, *, tm, n_ctx, nb):
    b = pl.program_id(0)
    i = pl.program_id(1)
    x = x_ref[0]
    xn = x * lax.rsqrt(jnp.mean(x * x, axis=-1, keepdims=True) + NORM_EPS) * nw_ref[...]
    row = i * tm + lax.broadcasted_iota(jnp.int32, (tm, 1), 0)
    is_ctx = row < n_ctx
    sh = _mod_rows(mod_ref, b, nb, 3 * D_MODEL, is_ctx)
    sc = _mod_rows(mod_ref, b, nb, 4 * D_MODEL, is_ctx)
    h = xn * (1.0 + sc) + sh
    h_ref[0] = h.astype(BF16)
    logits = jnp.dot(h, rw_ref[...], precision=HI, preferred_element_type=F32) + rb_ref[...]
    lane = lax.broadcasted_iota(jnp.int32, logits.shape, 1).astype(F32)
    neg = jnp.float32(-jnp.inf)
    cur = logits
    comb = jnp.zeros_like(logits)
    denom = jnp.zeros((tm, 1), F32)
    top = None
    for kk in range(TOP_K):
        m = jnp.max(cur, axis=-1, keepdims=True)
        idx = jnp.min(jnp.where(cur == m, lane, 128.0), axis=-1, keepdims=True)
        sel = lane == idx
        if kk == 0:
            top = m
        e = jnp.exp(m - top)
        comb = comb + jnp.where(sel, e, 0.0)
        denom = denom + e
        cur = jnp.where(sel, neg, cur)
    comb_ref[0] = comb / denom


def _router(resid, norm_w, mod, router_w, router_b, n_ctx):
    nb, t, d = resid.shape
    tm = _row_tile(t, 544)
    rw = jnp.pad(router_w, ((0, 0), (0, 128 - N_EXPERTS)))
    rb = jnp.pad(router_b, (0, 128 - N_EXPERTS), constant_values=-1e30).reshape(1, 128)
    kern = functools.partial(_router_kernel, tm=tm, n_ctx=n_ctx, nb=nb)
    return pl.pallas_call(
        kern,
        out_shape=(jax.ShapeDtypeStruct((nb, t, d), BF16), jax.ShapeDtypeStruct((nb, t, 128), F32)),
        grid=(nb, t // tm),
        in_specs=[
            pl.BlockSpec((1, tm, d), lambda b, i: (b, i, 0)),
            pl.BlockSpec((1, d), lambda b, i: (0, 0)),
            pl.BlockSpec(mod.shape, lambda b, i: (0, 0)),
            pl.BlockSpec((d, 128), lambda b, i: (0, 0)),
            pl.BlockSpec((1, 128), lambda b, i: (0, 0)),
        ],
        out_specs=(pl.BlockSpec((1, tm, d), lambda b, i: (b, i, 0)),
                   pl.BlockSpec((1, tm, 128), lambda b, i: (b, i, 0))),
        compiler_params=_cparams(("arbitrary", "arbitrary")),
        name="router",
    )(resid, norm_w.reshape(1, d), mod, rw, rb)


def _moe_kernel(h_ref, comb_ref, x_ref, mod_ref, wg_ref, bg_ref, wu_ref, bu_ref, wd_ref, bd_ref, o_ref,
                acc_ref, *, tm, n_ctx, nb):
    b = pl.program_id(0)
    i = pl.program_id(1)
    e = pl.program_id(2)
    dot = functools.partial(jnp.dot, preferred_element_type=F32)

    @pl.when(e == 0)
    def _():
        acc_ref[...] = jnp.zeros_like(acc_ref)

    h = h_ref[0]
    gl = jnp.minimum(dot(h, wg_ref[0]) + bg_ref[0], SWIGLU_LIMIT)
    li = jnp.clip(dot(h, wu_ref[0]) + bu_ref[0], -SWIGLU_LIMIT, SWIGLU_LIMIT)
    act = gl * jax.nn.sigmoid(SWIGLU_ALPHA * gl) * (li + 1.0)
    y = dot(act.astype(BF16), wd_ref[0]) + bd_ref[0]
    comb = comb_ref[0]
    lane = lax.broadcasted_iota(jnp.int32, comb.shape, 1)
    ce = jnp.sum(jnp.where(lane == e, comb, 0.0), axis=-1, keepdims=True)
    acc_ref[...] += ce * y

    @pl.when(e == N_EXPERTS - 1)
    def _():
        row = i * tm + lax.broadcasted_iota(jnp.int32, (tm, 1), 0)
        gate = _mod_rows(mod_ref, b, nb, 5 * D_MODEL, row < n_ctx)
        o_ref[0] = x_ref[0] + gate * acc_ref[...]


def _moe(h, comb, resid, mod, wg, bg, wu, bu, wd, bd, n_ctx):
    nb, t, d = resid.shape
    tm = _row_tile(t, 1088)
    kern = functools.partial(_moe_kernel, tm=tm, n_ctx=n_ctx, nb=nb)

    def wspec():
        return pl.BlockSpec((1, d, d), lambda b, i, e: (e, 0, 0))

    def bspec():
        return pl.BlockSpec((1, 1, d), lambda b, i, e: (e, 0, 0))

    return pl.pallas_call(
        kern,
        out_shape=jax.ShapeDtypeStruct((nb, t, d), F32),
        grid=(nb, t // tm, N_EXPERTS),
        in_specs=[
            pl.BlockSpec((1, tm, d), lambda b, i, e: (b, i, 0)),
            pl.BlockSpec((1, tm, 128), lambda b, i, e: (b, i, 0)),
            pl.BlockSpec((1, tm, d), lambda b, i, e: (b, i, 0)),
            pl.BlockSpec(mod.shape, lambda b, i, e: (0, 0)),
            wspec(), bspec(), wspec(), bspec(), wspec(), bspec(),
        ],
        out_specs=pl.BlockSpec((1, tm, d), lambda b, i, e: (b, i, 0)),
        scratch_shapes=[pltpu.VMEM((tm, d), F32)],
        compiler_params=_cparams(("arbitrary", "arbitrary", "arbitrary")),
        name="moe_experts",
    )(h, comb, resid, mod, wg.astype(BF16), bg.reshape(N_EXPERTS, 1, d), wu.astype(BF16),
      bu.reshape(N_EXPERTS, 1, d), wd.astype(BF16), bd.reshape(N_EXPERTS, 1, d))


def _final_norm_kernel(x_ref, w_ref, o_ref):
    x = x_ref[0]
    o_ref[0] = x * lax.rsqrt(jnp.mean(x * x, axis=-1, keepdims=True) + NORM_EPS) * w_ref[...]


def _final_norm(resid, w, n_ctx):
    nb, t, d = resid.shape
    tm = n_ctx
    return pl.pallas_call(
        _final_norm_kernel,
        out_shape=jax.ShapeDtypeStruct((nb, t - n_ctx, d), F32),
        grid=(nb, (t - n_ctx) // tm),
        in_specs=[pl.BlockSpec((1, tm, d), lambda b, i: (b, i + 1, 0)),
                  pl.BlockSpec((1, d), lambda b, i: (0, 0))],
        out_specs=pl.BlockSpec((1, tm, d), lambda b, i: (b, i, 0)),
        compiler_params=_cparams(("arbitrary", "arbitrary")),
        name="final_norm",
    )(resid, w.reshape(1, d))


def _pack_w_in(w):
    o_gg = CONV_CH + GLA_V
    o_rg = o_gg + 2 * GLA_GATE_RANK
    o_rw = o_rg + RWKV_G_RANK
    o_ra = o_rw + 2 * RWKV_W_RANK
    o_u = o_ra + 2 * RWKV_A_RANK
    o_gt = o_u + D_MODEL
    pad = jnp.zeros((w.shape[0], 128 - 2 * GLA_GATE_RANK), w.dtype)
    return jnp.concatenate([w[:, :o_gg], w[:, o_u:o_gt], w[:, o_gt:], w[:, o_gg:o_rg], pad,
                            w[:, o_rg:o_rw], w[:, o_rw:o_ra], w[:, o_ra:o_u]], axis=1).astype(BF16)


def kernel(x, c, ctx, c_ctx, w_ada, b_ada, norm_mix, w_in, conv_w, gla_wg2, gla_bg, gla_norm, rw_w0, rw_w2, rw_a0, rw_a2, rw_g2, rw_kk, rw_ka, rw_rk, rw_ln_g, rw_ln_b, s5_lam_re, s5_lam_im, s5_log_dt, s5_b_re, s5_b_im, s5_c_re, s5_c_im, s5_d, s5_glu_w, s5_glu_b, w_branch, w_out, norm_ffn, router_w, router_b, exp_w_gate, exp_b_gate, exp_w_up, exp_b_up, exp_w_down, exp_b_down, final_norm):
    depth = w_ada.shape[0]
    nb, _, d = x.shape
    n_ctx = ctx.shape[1]
    assert d == D_MODEL and n_ctx % RWKV_SEG == 0 and x.shape[1] % RWKV_SEG == 0 and nb < S5_BPAD

    resid = jnp.concatenate([ctx, x], axis=1).astype(F32)
    cc = jnp.concatenate([c, c_ctx[None], jnp.zeros((S5_BPAD - nb - 1, d), F32)], axis=0)
    mods = _ada_table(cc, w_ada, b_ada)

    for l in range(depth):
        mod = mods[l]
        p = _norm_in_proj(resid, norm_mix[l], mod, _pack_w_in(w_in[l]), n_ctx)
        pcv = _short_conv(p, conv_w[l], n_ctx)
        y_gla = _gla_mixer(pcv, p, gla_wg2[l], gla_bg[l], gla_norm[l], n_ctx)
        prm = dict(w0=rw_w0[l], w2=rw_w2[l], a0=rw_a0[l], a2=rw_a2[l], g2=rw_g2[l], kk=rw_kk[l], ka=rw_ka[l],
                   rk=rw_rk[l].reshape(-1), ln_g=rw_ln_g[l], ln_b=rw_ln_b[l])
        y_f = _rwkv_pass(pcv, p, None, prm, 0, n_ctx)
        y_rw = _rwkv_pass(pcv, p, y_f, prm, 1, n_ctx)
        ops = _s5_operators(s5_lam_re[l], s5_lam_im[l], s5_log_dt[l], s5_b_re[l], s5_b_im[l],
                            s5_c_re[l], s5_c_im[l])
        y_s5 = _s5_mixer(p, ops, n_ctx)
        resid = _merge(y_gla, y_rw, y_s5, p, resid, mod, w_branch[l], w_out[l], s5_d[l], s5_glu_w[l],
                       s5_glu_b[l], n_ctx)
        h, comb = _router(resid, norm_ffn[l], mod, router_w[l], router_b[l], n_ctx)
        resid = _moe(h, comb, resid, mod, exp_w_gate[l], exp_b_gate[l], exp_w_up[l], exp_b_up[l],
                     exp_w_down[l], exp_b_down[l], n_ctx)
    return _final_norm(resid, final_norm, n_ctx)
```

```python
import functools
import math

import jax
import jax.numpy as jnp
from jax import lax
from jax.experimental import pallas as pl
from jax.experimental.pallas import tpu as pltpu

F32 = jnp.float32
BF16 = jnp.bfloat16
HI = lax.Precision.HIGHEST

D_MODEL = 1024
GRID_W = 64

GLA_HEADS = 4
GLA_DK = 128
GLA_DV = 256
GLA_QK = GLA_HEADS * GLA_DK
GLA_V = GLA_HEADS * GLA_DV
GLA_GATE_RANK = 16
GLA_GATE_NORM = 16.0
CHUNK = 64
GLA_GROUP = 4

RWKV_HEAD = 64
RWKV_DIM = 1024
RWKV_W_RANK = 64
RWKV_A_RANK = 64
RWKV_G_RANK = 128
RWKV_W_SCALE = math.exp(-0.5)
RWKV_LN_EPS = 64e-5
RWKV_LANES = 256
RWKV_HPB = RWKV_LANES // RWKV_HEAD
RWKV_SEG = 256

S5_GROUP_CH = 16
S5_GROUPS = 64
S5_STATE = 64
S5_CHUNK = 16
S5_LANE_GROUPS = 8
MOD_ROWS = 8

N_EXPERTS = 32
TOP_K = 4
SWIGLU_LIMIT = 7.0
SWIGLU_ALPHA = 1.702
NORM_EPS = 1e-6

GLA_CONV = 2 * GLA_QK + GLA_V
CONV_CH = GLA_CONV + 3 * RWKV_DIM

OFF_PC = 0
OFF_OG = CONV_CH
OFF_U = OFF_OG + GLA_V
OFF_GATES = OFF_U + D_MODEL
OFF_GG = OFF_GATES + 3 * D_MODEL
OFF_RG = OFF_GG + 128
OFF_RW = OFF_RG + 128
OFF_RA = OFF_RW + 128
NP = OFF_RA + 128

VMEM_LIMIT = 56 * 1024 * 1024


def _cparams(sem):
    return pltpu.CompilerParams(dimension_semantics=sem, vmem_limit_bytes=VMEM_LIMIT)


def _row_tile(t, cap):
    best = 8
    for d in range(8, min(t, cap) + 1, 8):
        if t % d == 0:
            best = d
    return best


def _silu(x):
    return x * jax.nn.sigmoid(x)


def _log_sigmoid(z):
    return jnp.minimum(z, 0.0) - jnp.log(1.0 + jnp.exp(-jnp.abs(z)))


def _tri_matmul(keep, x):
    n = x.shape[1]
    h1 = x.astype(BF16)
    r1 = x - h1.astype(F32)
    h2 = r1.astype(BF16)
    h3 = (r1 - h2.astype(F32)).astype(BF16)
    y = jnp.dot(keep.astype(BF16), jnp.concatenate([h1, h2, h3], axis=1), preferred_element_type=F32)
    return y[:, 0:n] + y[:, n:2 * n] + y[:, 2 * n:3 * n]


def _row_to_col(row):
    return jnp.transpose(jnp.broadcast_to(row, (128, row.shape[1])))[:, 0:1]


def _mod_rows(mod_ref, b, nb, off, is_ctx):
    vb = mod_ref[pl.ds(b, 1), off:off + D_MODEL]
    vc = mod_ref[nb:nb + 1, off:off + D_MODEL]
    return jnp.where(is_ctx, vc, vb)


def _ada_kernel(c_ref, w_ref, b_ref, o_ref):
    a = _silu(c_ref[...])
    o_ref[0] = jnp.dot(a, w_ref[0], precision=HI, preferred_element_type=F32) + b_ref[0]


def _ada_table(cc, w_ada, b_ada):
    depth, d, n = w_ada.shape
    tn = 1536
    return pl.pallas_call(
        _ada_kernel,
        out_shape=jax.ShapeDtypeStruct((depth, cc.shape[0], n), F32),
        grid=(depth, n // tn),
        in_specs=[
            pl.BlockSpec((cc.shape[0], d), lambda l, j: (0, 0)),
            pl.BlockSpec((1, d, tn), lambda l, j: (l, 0, j)),
            pl.BlockSpec((1, 1, tn), lambda l, j: (l, 0, j)),
        ],
        out_specs=pl.BlockSpec((1, cc.shape[0], tn), lambda l, j: (l, 0, j)),
        compiler_params=_cparams(("arbitrary", "arbitrary")),
        name="ada_table",
    )(cc, w_ada, b_ada.reshape(depth, 1, n))


def _norm_mm_kernel(x_ref, nw_ref, mod_ref, w_ref, o_ref, h_ref, *, tm, n_ctx, nb):
    b = pl.program_id(0)
    i = pl.program_id(1)

    @pl.when(pl.program_id(2) == 0)
    def _():
        x = x_ref[0]
        xn = x * lax.rsqrt(jnp.mean(x * x, axis=-1, keepdims=True) + NORM_EPS) * nw_ref[...]
        row = i * tm + lax.broadcasted_iota(jnp.int32, (tm, 1), 0)
        is_ctx = row < n_ctx
        sh = _mod_rows(mod_ref, b, nb, 0, is_ctx)
        sc = _mod_rows(mod_ref, b, nb, D_MODEL, is_ctx)
        h_ref[...] = (xn * (1.0 + sc) + sh).astype(BF16)

    o_ref[0] = jnp.dot(h_ref[...], w_ref[...], preferred_element_type=F32)


def _norm_in_proj(resid, norm_w, mod, w_in_p, n_ctx):
    nb, t, d = resid.shape
    tm = _row_tile(t, 1088)
    tn = 768
    kern = functools.partial(_norm_mm_kernel, tm=tm, n_ctx=n_ctx, nb=nb)
    return pl.pallas_call(
        kern,
        out_shape=jax.ShapeDtypeStruct((nb, t, NP), F32),
        grid=(nb, t // tm, NP // tn),
        in_specs=[
            pl.BlockSpec((1, tm, d), lambda b, i, j: (b, i, 0)),
            pl.BlockSpec((1, d), lambda b, i, j: (0, 0)),
            pl.BlockSpec(mod.shape, lambda b, i, j: (0, 0)),
            pl.BlockSpec((d, tn), lambda b, i, j: (0, j)),
        ],
        out_specs=pl.BlockSpec((1, tm, tn), lambda b, i, j: (b, i, j)),
        scratch_shapes=[pltpu.VMEM((tm, d), BF16)],
        compiler_params=_cparams(("arbitrary", "arbitrary", "arbitrary")),
        name="norm_in_proj",
    )(resid, norm_w.reshape(1, d), mod, w_in_p)


def _conv_kernel(x_ref, w_ref, o_ref, xp_ref, xl_ref, xr_ref, *, n_ctx, rows):
    tc = x_ref.shape[-1]
    gw = GRID_W

    def wrow(k):
        return w_ref[k:k + 1, :]

    xc = x_ref[0, 0:n_ctx, :]
    ci = lax.broadcasted_iota(jnp.int32, (n_ctx, tc), 0)
    left = jnp.where(ci > 0, pltpu.roll(xc, 1, 0), 0.0)
    right = jnp.where(ci < n_ctx - 1, pltpu.roll(xc, n_ctx - 1, 0), 0.0)
    o_ref[0, 0:n_ctx, :] = left * wrow(3) + xc * wrow(4) + right * wrow(5)

    zeros = jnp.zeros((gw, tc), F32)
    for ref in (xp_ref, xl_ref, xr_ref):
        ref[0:gw, :] = zeros
        ref[(rows + 1) * gw:(rows + 2) * gw, :] = zeros
    wi = lax.broadcasted_iota(jnp.int32, (gw, tc), 0)

    def fill(r, carry):
        base = pl.multiple_of(r * gw, gw)
        xrow = x_ref[0, pl.ds(n_ctx + base, gw), :]
        xp_ref[pl.ds(gw + base, gw), :] = xrow
        xl_ref[pl.ds(gw + base, gw), :] = jnp.where(wi > 0, pltpu.roll(xrow, 1, 0), 0.0)
        xr_ref[pl.ds(gw + base, gw), :] = jnp.where(wi < gw - 1, pltpu.roll(xrow, gw - 1, 0), 0.0)
        return carry

    lax.fori_loop(0, rows, fill, 0)

    def comp(r, carry):
        base = pl.multiple_of(r * gw, gw)
        acc = jnp.zeros((gw, tc), F32)
        for dr in range(3):
            sl = pl.ds(base + gw * dr, gw)
            acc = acc + xl_ref[sl, :] * wrow(3 * dr) + xp_ref[sl, :] * wrow(3 * dr + 1) \
                + xr_ref[sl, :] * wrow(3 * dr + 2)
        o_ref[0, pl.ds(n_ctx + base, gw), :] = acc
        return carry

    lax.fori_loop(0, rows, comp, 0)


def _short_conv(p, conv_w, n_ctx):
    nb, t, _ = p.shape
    rows = (t - n_ctx) // GRID_W
    tc = 256
    kern = functools.partial(_conv_kernel, n_ctx=n_ctx, rows=rows)
    pad_rows = (rows + 2) * GRID_W
    return pl.pallas_call(
        kern,
        out_shape=jax.ShapeDtypeStruct((nb, t, CONV_CH), F32),
        grid=(nb, CONV_CH // tc),
        in_specs=[
            pl.BlockSpec((1, t, tc), lambda b, j: (b, 0, j)),
            pl.BlockSpec((9, tc), lambda b, j: (0, j)),
        ],
        out_specs=pl.BlockSpec((1, t, tc), lambda b, j: (b, 0, j)),
        scratch_shapes=[pltpu.VMEM((pad_rows, tc), F32)] * 3,
        compiler_params=_cparams(("arbitrary", "arbitrary")),
        name="short_conv",
    )(p, conv_w.reshape(9, CONV_CH))


def _gla_kernel(q_ref, k_ref, v_ref, gg_ref, og_ref, wg_ref, bg_ref, ng_ref, o_ref, ob_ref, *, n_ctx, t):
    c = CHUNK
    nch = t // c
    nctx = n_ctx // c
    ii = lax.broadcasted_iota(jnp.int32, (c, c), 0)
    jj = lax.broadcasted_iota(jnp.int32, (c, c), 1)
    scale = GLA_DK ** -0.5

    nt = (((1,), (1,)), ((), ()))
    tn = (((0,), (0,)), ((), ()))
    dot = functools.partial(jnp.dot, preferred_element_type=F32)
    grp = GLA_GROUP
    assert nctx % grp == 0 and nch % grp == 0

    def group(g, carry):
        chains = []
        for d in range(2):
            for u in range(grp):
                i = g * grp + u
                ci = i if d == 0 else jnp.where(i < nctx, nctx - 1 - i, nch + nctx - 1 - i)
                chains.append((d, pl.ds(pl.multiple_of(ci * c, c), c)))
        keeps = [(ii >= jj) if d == 0 else (ii <= jj) for d, _ in chains]
        q = [_silu(q_ref[0, r, :]) * scale for _, r in chains]
        k = [_silu(k_ref[0, r, :]) for _, r in chains]
        vb = [_silu(v_ref[0, r, :]).astype(BF16) for _, r in chains]
        z = [dot(gg_ref[0, r, :][:, GLA_GATE_RANK * d:GLA_GATE_RANK * (d + 1)], wg_ref[d]) + bg_ref[d]
             for d, r in chains]
        la = [_log_sigmoid(x) * (1.0 / GLA_GATE_NORM) for x in z]
        bcum = [_tri_matmul(kp, x) for kp, x in zip(keeps, la)]
        blast = [b[c - 1:c, :] if d == 0 else b[0:1, :] for (d, _), b in zip(chains, bcum)]
        q_dec = [(x * jnp.exp(b)).astype(BF16) for x, b in zip(q, bcum)]
        k_inv = [(x * jnp.exp(-b)).astype(BF16) for x, b in zip(k, bcum)]
        k_end = [(x * jnp.exp(bl - b)).astype(BF16) for x, b, bl in zip(k, bcum, blast)]
        sc = [lax.dot_general(a, b, nt, preferred_element_type=F32) for a, b in zip(q_dec, k_inv)]
        sc = [jnp.where(kp, x, 0.0).astype(BF16) for kp, x in zip(keeps, sc)]
        intra = [dot(a, b) for a, b in zip(sc, vb)]
        kv = [lax.dot_general(a, b, tn, preferred_element_type=F32) for a, b in zip(k_end, vb)]
        dec = [_row_to_col(jnp.exp(bl)) for bl in blast]
        states = list(carry)
        for n, (d, r) in enumerate(chains):
            s = states[d]
            o = intra[n] + dot(q_dec[n], s.astype(BF16))
            if d == 0:
                o_ref[0, r, :] = o
            else:
                ob_ref[r, :] = o
            states[d] = dec[n] * s + kv[n]
        return tuple(states)

    s0 = jnp.zeros((GLA_DK, GLA_DV), F32)
    lax.fori_loop(0, nch // grp, group, (s0, s0))

    def finish(ci, carry):
        rows = pl.ds(pl.multiple_of(ci * c, c), c)
        o = o_ref[0, rows, :] + ob_ref[rows, :]
        o = o * lax.rsqrt(jnp.mean(o * o, axis=-1, keepdims=True) + 1e-5) * ng_ref[...]
        o_ref[0, rows, :] = o * _silu(og_ref[0, rows, :])
        return carry

    lax.fori_loop(0, nch, finish, 0, unroll=2)


def _gla_mixer(pcv, p, wg2, bg, norm_g, n_ctx):
    nb, t, _ = pcv.shape
    kern = functools.partial(_gla_kernel, n_ctx=n_ctx, t=t)
    return pl.pallas_call(
        kern,
        out_shape=jax.ShapeDtypeStruct((nb, t, GLA_V), F32),
        grid=(nb, GLA_HEADS),
        in_specs=[
            pl.BlockSpec((1, t, GLA_DK), lambda b, h: (b, 0, h)),
            pl.BlockSpec((1, t, GLA_DK), lambda b, h: (b, 0, GLA_HEADS + h)),
            pl.BlockSpec((1, t, GLA_DV), lambda b, h: (b, 0, GLA_HEADS + h)),
            pl.BlockSpec((1, t, 128), lambda b, h: (b, 0, OFF_GG // 128)),
            pl.BlockSpec((1, t, GLA_DV), lambda b, h: (b, 0, OFF_OG // GLA_DV + h)),
            pl.BlockSpec((2, GLA_GATE_RANK, GLA_DK), lambda b, h: (0, 0, h)),
            pl.BlockSpec((2, 1, GLA_DK), lambda b, h: (0, 0, h)),
            pl.BlockSpec((1, GLA_DV), lambda b, h: (0, 0)),
        ],
        out_specs=pl.BlockSpec((1, t, GLA_DV), lambda b, h: (b, 0, h)),
        scratch_shapes=[pltpu.VMEM((t, GLA_DV), F32)],
        compiler_params=_cparams(("arbitrary", "arbitrary")),
        name="gla_mixer",
    )(pcv, pcv, pcv, p, p, wg2, bg.reshape(2, 1, GLA_QK), norm_g.reshape(1, GLA_DV))


def _bd_mask():
    shift = RWKV_HEAD.bit_length() - 1
    r = lax.broadcasted_iota(jnp.int32, (RWKV_LANES, RWKV_LANES), 0) >> shift
    c = lax.broadcasted_iota(jnp.int32, (RWKV_LANES, RWKV_LANES), 1) >> shift
    return r == c


def _bd(x, mask):
    xb = x.astype(BF16)
    return jnp.where(mask, jnp.concatenate([xb] * RWKV_HPB, axis=0), jnp.zeros((), BF16))


def _head_sum(x, ones_bd):
    hi = x.astype(BF16)
    lo = (x - hi.astype(F32)).astype(BF16)
    return jnp.dot(hi, ones_bd, preferred_element_type=F32) + jnp.dot(lo, ones_bd, preferred_element_type=F32)


def _rwkv_inputs(r_ref, k_ref, rw_ref, ra_ref, w0_ref, w2_ref, a0_ref, a2_ref, kkp_ref, kap_ref, rows, d,
                 ones_bd):
    r = r_ref[0, rows, :]
    k = k_ref[0, rows, :]
    pw = rw_ref[0, rows, :][:, RWKV_W_RANK * d:RWKV_W_RANK * (d + 1)]
    pa = ra_ref[0, rows, :][:, RWKV_A_RANK * d:RWKV_A_RANK * (d + 1)]
    lw = w0_ref[d] + jnp.dot(jnp.tanh(pw), w2_ref[d], preferred_element_type=F32)
    logw = -RWKV_W_SCALE * jax.nn.sigmoid(lw)
    la = a0_ref[d] + jnp.dot(pa, a2_ref[d], preferred_element_type=F32)
    a = jax.nn.sigmoid(la)
    kkv = k * kkp_ref[...]
    nrm = jnp.maximum(jnp.sqrt(_head_sum(kkv * kkv, ones_bd)), 1e-12)
    kk = kkv / nrm
    kd = k * (1.0 + (a - 1.0) * kap_ref[...])
    return r, logw, kd, kk, kk * a


def _rwkv_chunk(st, r, logw, kd, v, kk, kka, d, mask_bd):
    c = CHUNK
    ii = lax.broadcasted_iota(jnp.int32, (c, c), 0)
    jj = lax.broadcasted_iota(jnp.int32, (c, c), 1)
    keep_sq = (ii >= jj) if d == 0 else (ii <= jj)
    ti = lax.broadcasted_iota(jnp.int32, (c, RWKV_LANES), 0)
    si = lax.broadcasted_iota(jnp.int32, (c, RWKV_LANES), 1) & (c - 1)
    incl = (si <= ti) if d == 0 else (si >= ti)
    strict = (si < ti) if d == 0 else (si > ti)
    eye = (si == ti).astype(F32)

    cb = _tri_matmul(keep_sq, logw)
    cblast = cb[c - 1:c, :] if d == 0 else cb[0:1, :]
    ad = -kk * jnp.exp(cb - logw)
    rd = r * jnp.exp(cb)
    einv = jnp.exp(-cb)
    bi = kka * einv
    ki = kd * einv
    eend = jnp.exp(cblast - cb)
    bend = kka * eend
    kend = kd * eend

    nt = (((1,), (1,)), ((), ()))
    tn = (((0,), (0,)), ((), ()))
    lhs = jnp.concatenate([ad, rd], axis=0).astype(BF16)
    pb = lax.dot_general(lhs, _bd(bi, mask_bd), nt, preferred_element_type=F32)
    pk = lax.dot_general(lhs, _bd(ki, mask_bd), nt, preferred_element_type=F32)
    a_ab = jnp.where(strict, pb[0:c], 0.0)
    r_ab = jnp.where(incl, pb[c:2 * c], 0.0)
    a_ak = jnp.where(strict, pk[0:c], 0.0)
    r_ak = jnp.where(incl, pk[c:2 * c], 0.0)

    def mm(x, y):
        return jnp.dot(x.astype(BF16), _bd(y, mask_bd), preferred_element_type=F32)

    tinv = eye + a_ab
    pw = a_ab
    for _ in range(5):
        pw = mm(pw, pw)
        tinv = tinv + mm(pw, tinv)

    stb = st.astype(BF16)
    vbd = _bd(v, mask_bd)
    x = jnp.dot(ad.astype(BF16), stb, preferred_element_type=F32) \
        + jnp.dot(a_ak.astype(BF16), vbd, preferred_element_type=F32)
    u = mm(tinv, x)
    y = jnp.dot(rd.astype(BF16), stb, preferred_element_type=F32) + mm(r_ab, u) \
        + jnp.dot(r_ak.astype(BF16), vbd, preferred_element_type=F32)
    dec = _row_to_col(jnp.exp(cblast))
    upd = lax.dot_general(jnp.concatenate([bend, kend], axis=0).astype(BF16),
                          jnp.concatenate([u, v], axis=0).astype(BF16), tn, preferred_element_type=F32)
    st = dec * st + jnp.where(mask_bd, upd, 0.0)
    return y, st


def _rwkv_scan_kernel(rf_ref, kf_ref, vf_ref, rwf_ref, raf_ref, rb_ref, kb_ref, vb_ref, rwb_ref, rab_ref,
                      w0_ref, w2_ref, a0_ref, a2_ref, kkp_ref, kap_ref, yf_ref, yb_ref, stf_ref, stb_ref):
    c = CHUNK
    nchunk = RWKV_SEG // c
    mask_bd = _bd_mask()
    ones_bd = mask_bd.astype(BF16)

    @pl.when(pl.program_id(2) == 0)
    def _():
        stf_ref[...] = jnp.zeros_like(stf_ref)
        stb_ref[...] = jnp.zeros_like(stb_ref)

    sf = stf_ref[...]
    sb = stb_ref[...]
    for ci in range(nchunk):
        rows = slice(ci * c, (ci + 1) * c)
        r, logw, kd, kk, kka = _rwkv_inputs(rf_ref, kf_ref, rwf_ref, raf_ref, w0_ref, w2_ref, a0_ref, a2_ref,
                                            kkp_ref, kap_ref, rows, 0, ones_bd)
        y, sf = _rwkv_chunk(sf, r, logw, kd, vf_ref[0, rows, :], kk, kka, 0, mask_bd)
        yf_ref[0, rows, :] = y
        cj = nchunk - 1 - ci
        rows = slice(cj * c, (cj + 1) * c)
        r, logw, kd, kk, kka = _rwkv_inputs(rb_ref, kb_ref, rwb_ref, rab_ref, w0_ref, w2_ref, a0_ref, a2_ref,
                                            kkp_ref, kap_ref, rows, 1, ones_bd)
        y, sb = _rwkv_chunk(sb, r, logw, kd, vb_ref[0, rows, :], kk, kka, 1, mask_bd)
        yb_ref[0, rows, :] = y
    stf_ref[...] = sf
    stb_ref[...] = sb


def _rwkv_group_kernel(rf_ref, kf_ref, vf_ref, rwf_ref, raf_ref, rb_ref, kb_ref, vb_ref, rwb_ref, rab_ref,
                       w0_ref, w2_ref, a0_ref, a2_ref, kkp_ref, kap_ref, yf_ref, yb_ref, stf_ref, stb_ref):
    c = CHUNK
    nchunk = RWKV_SEG // c
    mask_bd = _bd_mask()
    ones_bd = mask_bd.astype(BF16)
    dot = functools.partial(jnp.dot, preferred_element_type=F32)
    nt = (((1,), (1,)), ((), ()))
    tn = (((0,), (0,)), ((), ()))

    @pl.when(pl.program_id(2) == 0)
    def _():
        stf_ref[...] = jnp.zeros_like(stf_ref)
        stb_ref[...] = jnp.zeros_like(stb_ref)

    chains = [(0, slice(u * c, (u + 1) * c)) for u in range(nchunk)] \
        + [(1, slice(u * c, (u + 1) * c)) for u in range(nchunk - 1, -1, -1)]
    refs = ((rf_ref, kf_ref, vf_ref, rwf_ref, raf_ref), (rb_ref, kb_ref, vb_ref, rwb_ref, rab_ref))

    def each(f, *lists):
        return [f(*xs) for xs in zip(*lists)]

    ds = [d for d, _ in chains]
    r = [refs[d][0][0, rows, :] for d, rows in chains]
    k = [refs[d][1][0, rows, :] for d, rows in chains]
    v = [refs[d][2][0, rows, :] for d, rows in chains]
    pw = [refs[d][3][0, rows, :][:, RWKV_W_RANK * d:RWKV_W_RANK * (d + 1)] for d, rows in chains]
    pa = [refs[d][4][0, rows, :][:, RWKV_A_RANK * d:RWKV_A_RANK * (d + 1)] for d, rows in chains]
    lw = each(lambda d, x: w0_ref[d] + dot(jnp.tanh(x), w2_ref[d]), ds, pw)
    logw = [-RWKV_W_SCALE * jax.nn.sigmoid(x) for x in lw]
    la = each(lambda d, x: a0_ref[d] + dot(x, a2_ref[d]), ds, pa)
    a = [jax.nn.sigmoid(x) for x in la]
    kkv = [x * kkp_ref[...] for x in k]
    ss = [_head_sum(x * x, ones_bd) for x in kkv]
    kk = each(lambda x, s: x / jnp.maximum(jnp.sqrt(s), 1e-12), kkv, ss)
    kd = each(lambda x, y: x * (1.0 + (y - 1.0) * kap_ref[...]), k, a)
    kka = each(lambda x, y: x * y, kk, a)

    ii = lax.broadcasted_iota(jnp.int32, (c, c), 0)
    jj = lax.broadcasted_iota(jnp.int32, (c, c), 1)
    ti = lax.broadcasted_iota(jnp.int32, (c, RWKV_LANES), 0)
    si = lax.broadcasted_iota(jnp.int32, (c, RWKV_LANES), 1) & (c - 1)
    keep_sq = [(ii >= jj) if d == 0 else (ii <= jj) for d in ds]
    incl = [(si <= ti) if d == 0 else (si >= ti) for d in ds]
    strict = [(si < ti) if d == 0 else (si > ti) for d in ds]
    eye = (si == ti).astype(F32)

    cb = each(_tri_matmul, keep_sq, logw)
    cblast = each(lambda d, x: x[c - 1:c, :] if d == 0 else x[0:1, :], ds, cb)
    ad = each(lambda x, b, w: -x * jnp.exp(b - w), kk, cb, logw)
    rd = each(lambda x, b: x * jnp.exp(b), r, cb)
    einv = [jnp.exp(-b) for b in cb]
    bi = each(lambda x, e: x * e, kka, einv)
    ki = each(lambda x, e: x * e, kd, einv)
    eend = each(lambda bl, b: jnp.exp(bl - b), cblast, cb)
    bend = each(lambda x, e: x * e, kka, eend)
    kend = each(lambda x, e: x * e, kd, eend)

    def bd(x):
        return _bd(x, mask_bd)

    def mm(x, y_bd):
        return dot(x.astype(BF16), y_bd)

    lhs = each(lambda x, y: jnp.concatenate([x, y], axis=0).astype(BF16), ad, rd)
    pb = each(lambda x, y: lax.dot_general(x, bd(y), nt, preferred_element_type=F32), lhs, bi)
    pk = each(lambda x, y: lax.dot_general(x, bd(y), nt, preferred_element_type=F32), lhs, ki)
    a_ab = each(lambda m, x: jnp.where(m, x[0:c], 0.0), strict, pb)
    r_ab = each(lambda m, x: jnp.where(m, x[c:2 * c], 0.0), incl, pb)
    a_ak = each(lambda m, x: jnp.where(m, x[0:c], 0.0), strict, pk)
    r_ak = each(lambda m, x: jnp.where(m, x[c:2 * c], 0.0), incl, pk)

    tinv = [eye + x for x in a_ab]
    pwr = a_ab
    for _ in range(5):
        pwr = each(lambda x: mm(x, bd(x)), pwr)
        tinv = each(lambda t_, p_: t_ + mm(p_, bd(t_)), tinv, pwr)

    vbd = [bd(x) for x in v]
    m1 = each(lambda t_, x: mm(t_, bd(x)), tinv, ad)
    u0 = each(lambda t_, x, vb_: mm(t_, bd(dot(x.astype(BF16), vb_))), tinv, a_ak, vbd)
    r1 = each(lambda x, rab, m: (x + mm(rab, bd(m))).astype(BF16), rd, r_ab, m1)
    y0 = each(lambda rab, x, rak, vb_: mm(rab, bd(x)) + dot(rak.astype(BF16), vb_), r_ab, u0, r_ak, vbd)
    g1 = each(lambda x, m: jnp.where(mask_bd, lax.dot_general(x.astype(BF16), m.astype(BF16), tn,
                                                              preferred_element_type=F32), 0.0).astype(BF16),
              bend, m1)
    h = each(lambda x, y, u_, v_: jnp.where(mask_bd, lax.dot_general(
        jnp.concatenate([x, y], axis=0).astype(BF16), jnp.concatenate([u_, v_], axis=0).astype(BF16), tn,
        preferred_element_type=F32), 0.0), bend, kend, u0, v)
    dec = [_row_to_col(jnp.exp(x)) for x in cblast]

    states = [stf_ref[...], stb_ref[...]]
    outs = (yf_ref, yb_ref)
    for n, (d, rows) in enumerate(chains):
        st = states[d]
        stb = st.astype(BF16)
        outs[d][0, rows, :] = dot(r1[n], stb) + y0[n]
        states[d] = dec[n] * st + dot(g1[n], stb) + h[n]
    stf_ref[...] = states[0]
    stb_ref[...] = states[1]


def _rwkv_post_kernel(r_ref, k_ref, v_ref, ra_ref, rg_ref, yf_ref, yb_ref, a0_ref, a2_ref, g2_ref,
                      kap_ref, rkp_ref, lng_ref, lnb_ref, o_ref):
    ones_bd = _bd_mask().astype(BF16)
    r = r_ref[0]
    k = k_ref[0]
    ra = ra_ref[0]
    y = yf_ref[0] + yb_ref[0]
    inv_n = 1.0 / RWKV_HEAD
    mu = _head_sum(y, ones_bd) * inv_n
    yc = y - mu
    var = _head_sum(yc * yc, ones_bd) * inv_n
    yn = yc * lax.rsqrt(var + RWKV_LN_EPS) * lng_ref[...] + lnb_ref[...]
    kds = jnp.zeros_like(k)
    for d in range(2):
        la = a0_ref[d] + jnp.dot(ra[:, RWKV_A_RANK * d:RWKV_A_RANK * (d + 1)], a2_ref[d],
                                 preferred_element_type=F32)
        kds = kds + k * (1.0 + (jax.nn.sigmoid(la) - 1.0) * kap_ref[...])
    bonus = _head_sum(r * kds * rkp_ref[...], ones_bd) * v_ref[0]
    g = jnp.dot(jax.nn.sigmoid(rg_ref[0]), g2_ref[...], preferred_element_type=F32)
    o_ref[0] = (yn + bonus) * g


def _rwkv_mixer(pcv, p, prm, n_ctx):
    nb, t, _ = pcv.shape
    seg = RWKV_SEG
    nseg = t // seg
    nseg_ctx = n_ctx // seg
    ln = RWKV_LANES
    nhb = RWKV_DIM // ln
    cb0 = GLA_CONV // ln

    def fmap(s):
        return s

    def bmap(s):
        return jnp.where(s < nseg_ctx, nseg_ctx - 1 - s, nseg - 1 - (s - nseg_ctx))

    def tok(off_blocks, smap):
        return pl.BlockSpec((1, seg, ln), lambda b, h, s: (b, smap(s), off_blocks + h))

    def small(off, smap):
        return pl.BlockSpec((1, seg, 128), lambda b, h, s: (b, smap(s), off // 128))

    def dirspecs(smap):
        return [tok(cb0, smap), tok(cb0 + nhb, smap), tok(cb0 + 2 * nhb, smap),
                small(OFF_RW, smap), small(OFF_RA, smap)]

    def vec():
        return pl.BlockSpec((1, ln), lambda b, h, s: (0, h))

    w0 = prm["w0"].reshape(2, 1, RWKV_DIM)
    a0 = prm["a0"].reshape(2, 1, RWKV_DIM)
    kkp = prm["kk"].reshape(1, RWKV_DIM)
    kap = prm["ka"].reshape(1, RWKV_DIM)
    yf, yb = pl.pallas_call(
        _rwkv_group_kernel,
        out_shape=(jax.ShapeDtypeStruct((nb, t, RWKV_DIM), F32),) * 2,
        grid=(nb, nhb, nseg),
        in_specs=dirspecs(fmap) + dirspecs(bmap) + [
            pl.BlockSpec((2, 1, ln), lambda b, h, s: (0, 0, h)),
            pl.BlockSpec((2, RWKV_W_RANK, ln), lambda b, h, s: (0, 0, h)),
            pl.BlockSpec((2, 1, ln), lambda b, h, s: (0, 0, h)),
            pl.BlockSpec((2, RWKV_A_RANK, ln), lambda b, h, s: (0, 0, h)),
            vec(), vec()],
        out_specs=(pl.BlockSpec((1, seg, ln), lambda b, h, s: (b, fmap(s), h)),
                   pl.BlockSpec((1, seg, ln), lambda b, h, s: (b, bmap(s), h))),
        scratch_shapes=[pltpu.VMEM((ln, ln), F32)] * 2,
        compiler_params=_cparams(("arbitrary", "arbitrary", "arbitrary")),
        name="rwkv7_scan",
    )(pcv, pcv, pcv, p, p, pcv, pcv, pcv, p, p, w0, prm["w2"], a0, prm["a2"], kkp, kap)

    tm = _row_tile(t, 1088)

    def ptok(off_blocks):
        return pl.BlockSpec((1, tm, ln), lambda b, i, h: (b, i, off_blocks + h))

    def psmall(off):
        return pl.BlockSpec((1, tm, 128), lambda b, i, h: (b, i, off // 128))

    def pvec():
        return pl.BlockSpec((1, ln), lambda b, i, h: (0, h))

    return pl.pallas_call(
        _rwkv_post_kernel,
        out_shape=jax.ShapeDtypeStruct((nb, t, RWKV_DIM), F32),
        grid=(nb, t // tm, nhb),
        in_specs=[ptok(cb0), ptok(cb0 + nhb), ptok(cb0 + 2 * nhb), psmall(OFF_RA), psmall(OFF_RG),
                  ptok(0), ptok(0),
                  pl.BlockSpec((2, 1, ln), lambda b, i, h: (0, 0, h)),
                  pl.BlockSpec((2, RWKV_A_RANK, ln), lambda b, i, h: (0, 0, h)),
                  pl.BlockSpec((RWKV_G_RANK, ln), lambda b, i, h: (0, h)),
                  pvec(), pvec(), pvec(), pvec()],
        out_specs=ptok(0),
        compiler_params=_cparams(("arbitrary", "arbitrary", "arbitrary")),
        name="rwkv7_post",
    )(pcv, pcv, pcv, p, p, yf, yb, a0, prm["a2"], prm["g2"], kap, prm["rk"].reshape(1, RWKV_DIM),
      prm["ln_g"].reshape(1, RWKV_DIM), prm["ln_b"].reshape(1, RWKV_DIM))


def _s5_operators(lam_re, lam_im, log_dt, b_re, b_im, c_re, c_im):
    g, p_, lc = S5_GROUPS, S5_STATE, S5_CHUNK
    lr, li = lam_re.astype(F32), lam_im.astype(F32)
    br, bi = b_re.astype(F32), b_im.astype(F32)
    den = lr * lr + li * li
    n = jnp.arange(lc + 1, dtype=F32)[:, None, None]
    ein = functools.partial(jnp.einsum, precision=HI)
    out = []
    for d in range(2):
        dt = jnp.exp(log_dt[d].astype(F32))[:, None]
        mag = jnp.exp(n * (lr * dt)[None])
        ang = n * (li * dt)[None]
        pr, pi = mag * jnp.cos(ang), mag * jnp.sin(ang)
        m1 = jnp.exp(lr * dt)
        ar, ai = m1 * jnp.cos(li * dt), m1 * jnp.sin(li * dt)
        fr = ((ar - 1.0) * lr + ai * li) / den
        fi = (ai * lr - (ar - 1.0) * li) / den
        bb_r = fr[..., None] * br - fi[..., None] * bi
        bb_i = fr[..., None] * bi + fi[..., None] * br
        cr, ci = c_re[d].astype(F32), c_im[d].astype(F32)
        ca_r = cr[None] * pr[:, :, None, :] - ci[None] * pi[:, :, None, :]
        ca_i = cr[None] * pi[:, :, None, :] + ci[None] * pr[:, :, None, :]
        kt = ein('tgop,gpc->tgoc', ca_r[:lc], bb_r) - ein('tgop,gpc->tgoc', ca_i[:lc], bb_i)
        s_idx = jnp.arange(lc)[:, None]
        t_idx = jnp.arange(lc)[None, :]
        lag = (t_idx - s_idx) if d == 0 else (s_idx - t_idx)
        valid = lag >= 0
        toep = jnp.where(valid[:, :, None, None, None], kt[jnp.clip(lag, 0, lc - 1)], 0.0)
        toep = toep.transpose(2, 0, 4, 1, 3).reshape(g, lc * S5_GROUP_CH, lc * S5_GROUP_CH)
        e_s = (lc - 1 - jnp.arange(lc)) if d == 0 else jnp.arange(lc)
        ps_r, ps_i = pr[e_s], pi[e_s]
        bst_r = ps_r[..., None] * bb_r[None] - ps_i[..., None] * bb_i[None]
        bst_i = ps_r[..., None] * bb_i[None] + ps_i[..., None] * bb_r[None]
        bst_r = bst_r.transpose(1, 0, 3, 2).reshape(g, lc * S5_GROUP_CH, p_)
        bst_i = bst_i.transpose(1, 0, 3, 2).reshape(g, lc * S5_GROUP_CH, p_)
        f_t = (jnp.arange(lc) + 1) if d == 0 else (lc - jnp.arange(lc))
        co_r = ca_r[f_t].transpose(1, 3, 0, 2).reshape(g, p_, lc * S5_GROUP_CH)
        co_i = (-ca_i[f_t]).transpose(1, 3, 0, 2).reshape(g, p_, lc * S5_GROUP_CH)

        nlg, gl, ch = S5_LANE_GROUPS, S5_GROUPS // S5_LANE_GROUPS, S5_GROUP_CH
        eye = jnp.eye(gl, dtype=F32)
        op = jnp.einsum('lgscto,gh->lsgctho', toep.reshape(nlg, gl, lc, ch, lc, ch), eye)
        op = op.reshape(nlg, lc * gl * ch, lc * gl * ch)

        def rows_op(x):
            y = jnp.einsum('lgscp,gh->lsgchp', x.reshape(nlg, gl, lc, ch, p_), eye)
            return y.reshape(nlg, lc * gl * ch, gl * p_)

        def cols_op(x):
            y = jnp.einsum('lgpto,gh->lgptho', x.reshape(nlg, gl, p_, lc, ch), eye)
            return y.reshape(nlg, gl * p_, lc * gl * ch)

        out.append(dict(
            op=op.astype(BF16), bst_r=rows_op(bst_r).astype(BF16), bst_i=rows_op(bst_i).astype(BF16),
            co_r=cols_op(co_r).astype(BF16), co_i=cols_op(co_i).astype(BF16),
            a_r=pr[lc].reshape(nlg, 1, gl * p_), a_i=pi[lc].reshape(nlg, 1, gl * p_)))
    return {k: jnp.stack([out[0][k], out[1][k]]) for k in out[0]}


def _s5_kernel(u_ref, op_ref, br_ref, bi_ref, cr_ref, ci_ref, ar_ref, ai_ref, o_ref,
               ir_ref, ii_ref, xr_ref, xi_ref, *, nchunk, nchunk_ctx):
    d = pl.program_id(1)
    lc = S5_CHUNK
    dot = functools.partial(jnp.dot, preferred_element_type=F32)
    lhs = jnp.concatenate([u_ref[0, pl.ds(s, nchunk, stride=lc), :] for s in range(lc)], axis=1).astype(BF16)
    ir_ref[...] = dot(lhs, br_ref[0, 0])
    ii_ref[...] = dot(lhs, bi_ref[0, 0])
    ar = ar_ref[0, 0]
    ai = ai_ref[0, 0]

    def step(i, carry):
        sr, si = carry
        rev = jnp.where(i < nchunk_ctx, nchunk_ctx - 1 - i, nchunk + nchunk_ctx - 1 - i)
        ci = jnp.where(d == 0, i, rev)
        row = pl.ds(ci, 1)
        xr_ref[row, :] = sr
        xi_ref[row, :] = si
        nr = ar * sr - ai * si + ir_ref[row, :]
        ni = ar * si + ai * sr + ii_ref[row, :]
        return nr, ni

    z = jnp.zeros((1, ir_ref.shape[1]), F32)
    lax.fori_loop(0, nchunk, step, (z, z))
    y = dot(lhs, op_ref[0, 0]) + dot(xr_ref[...].astype(BF16), cr_ref[0, 0]) \
        + dot(xi_ref[...].astype(BF16), ci_ref[0, 0])
    for t in range(lc):
        o_ref[0, 0, pl.ds(t, nchunk, stride=lc), :] = y[:, t * 128:(t + 1) * 128]


def _s5_mixer(p, ops, n_ctx):
    nb, t, _ = p.shape
    lc = S5_CHUNK
    nchunk = t // lc
    nlg = S5_LANE_GROUPS
    ns = ops["a_r"].shape[-1]

    def wspec(x):
        return pl.BlockSpec((1, 1) + x.shape[2:], lambda g, d, b: (d, g, 0, 0))

    keys = ("op", "bst_r", "bst_i", "co_r", "co_i", "a_r", "a_i")
    kern = functools.partial(_s5_kernel, nchunk=nchunk, nchunk_ctx=n_ctx // lc)
    return pl.pallas_call(
        kern,
        out_shape=jax.ShapeDtypeStruct((2, nb, t, D_MODEL), F32),
        grid=(nlg, 2, nb),
        in_specs=[pl.BlockSpec((1, t, 128), lambda g, d, b: (b, 0, OFF_U // 128 + g))]
        + [wspec(ops[k]) for k in keys],
        out_specs=pl.BlockSpec((1, 1, t, 128), lambda g, d, b: (d, b, 0, g)),
        scratch_shapes=[pltpu.VMEM((nchunk, ns), F32)] * 4,
        compiler_params=_cparams(("arbitrary", "arbitrary", "arbitrary")),
        name="s5_mixer",
    )(p, *[ops[k] for k in keys])


def _merge_kernel(ygla_ref, yrw_ref, ys5f_ref, ys5b_ref, u_ref, ga_ref, gb_ref, gc_ref, x_ref, mod_ref, wb_ref, wo_ref,
                  d_ref, gw_ref, gbias_ref, o_ref, *, tm, n_ctx, nb):
    b = pl.program_id(0)
    i = pl.program_id(1)
    dot = functools.partial(jnp.dot, preferred_element_type=F32)
    z = jax.nn.gelu(ys5f_ref[0, 0] + ys5b_ref[0, 0] + d_ref[...] * u_ref[0])
    zs = z * jax.nn.sigmoid(dot(z.astype(BF16), gw_ref[...]) + gbias_ref[...])
    m = jax.nn.sigmoid(ga_ref[0]) * dot(ygla_ref[0].astype(BF16), wb_ref[0]) \
        + jax.nn.sigmoid(gb_ref[0]) * dot(yrw_ref[0].astype(BF16), wb_ref[1]) \
        + jax.nn.sigmoid(gc_ref[0]) * dot(zs.astype(BF16), wb_ref[2])
    mix = dot(m.astype(BF16), wo_ref[...])
    row = i * tm + lax.broadcasted_iota(jnp.int32, (tm, 1), 0)
    gate = _mod_rows(mod_ref, b, nb, 2 * D_MODEL, row < n_ctx)
    o_ref[0] = x_ref[0] + gate * mix


def _merge(ygla, yrw, ys5, p, resid, mod, w_branch, w_out, s5_d, glu_w, glu_b, n_ctx):
    nb, t, d = resid.shape
    tm = _row_tile(t, 272)
    kern = functools.partial(_merge_kernel, tm=tm, n_ctx=n_ctx, nb=nb)

    def tok():
        return pl.BlockSpec((1, tm, d), lambda b, i: (b, i, 0))

    return pl.pallas_call(
        kern,
        out_shape=jax.ShapeDtypeStruct((nb, t, d), F32),
        grid=(nb, t // tm),
        in_specs=[
            tok(), tok(),
            pl.BlockSpec((1, 1, tm, d), lambda b, i: (0, b, i, 0)),
            pl.BlockSpec((1, 1, tm, d), lambda b, i: (1, b, i, 0)),
            pl.BlockSpec((1, tm, d), lambda b, i: (b, i, OFF_U // d)),
            pl.BlockSpec((1, tm, d), lambda b, i: (b, i, OFF_GATES // d)),
            pl.BlockSpec((1, tm, d), lambda b, i: (b, i, OFF_GATES // d + 1)),
            pl.BlockSpec((1, tm, d), lambda b, i: (b, i, OFF_GATES // d + 2)),
            tok(),
            pl.BlockSpec(mod.shape, lambda b, i: (0, 0)),
            pl.BlockSpec((3, d, d), lambda b, i: (0, 0, 0)),
            pl.BlockSpec((d, d), lambda b, i: (0, 0)),
            pl.BlockSpec((1, d), lambda b, i: (0, 0)),
            pl.BlockSpec((d, d), lambda b, i: (0, 0)),
            pl.BlockSpec((1, d), lambda b, i: (0, 0)),
        ],
        out_specs=tok(),
        compiler_params=_cparams(("arbitrary", "arbitrary")),
        name="merge",
    )(ygla, yrw, ys5, ys5, p, p, p, p, resid, mod, w_branch.astype(BF16), w_out.astype(BF16), s5_d.reshape(1, d),
      glu_w.astype(BF16), glu_b.reshape(1, d))


def _router_kernel(x_ref, nw_ref, mod_ref, rw_ref, rb_ref, h_ref, comb_ref, *, tm, n_ctx, nb):
    b = pl.program_id(0)
    i = pl.program_id(1)
    x = x_ref[0]
    xn = x * lax.rsqrt(jnp.mean(x * x, axis=-1, keepdims=True) + NORM_EPS) * nw_ref[...]
    row = i * tm + lax.broadcasted_iota(jnp.int32, (tm, 1), 0)
    is_ctx = row < n_ctx
    sh = _mod_rows(mod_ref, b, nb, 3 * D_MODEL, is_ctx)
    sc = _mod_rows(mod_ref, b, nb, 4 * D_MODEL, is_ctx)
    h = xn * (1.0 + sc) + sh
    h_ref[0] = h.astype(BF16)
    logits = jnp.dot(h, rw_ref[...], precision=HI, preferred_element_type=F32) + rb_ref[...]
    lane = lax.broadcasted_iota(jnp.int32, logits.shape, 1).astype(F32)
    neg = jnp.float32(-jnp.inf)
    cur = logits
    comb = jnp.zeros_like(logits)
    denom = jnp.zeros((tm, 1), F32)
    top = None
    for kk in range(TOP_K):
        m = jnp.max(cur, axis=-1, keepdims=True)
        idx = jnp.min(jnp.where(cur == m, lane, 128.0), axis=-1, keepdims=True)
        sel = lane == idx
        if kk == 0:
            top = m
        e = jnp.exp(m - top)
        comb = comb + jnp.where(sel, e, 0.0)
        denom = denom + e
        cur = jnp.where(sel, neg, cur)
    comb_ref[0] = comb / denom


def _router(resid, norm_w, mod, router_w, router_b, n_ctx):
    nb, t, d = resid.shape
    tm = _row_tile(t, 544)
    rw = jnp.pad(router_w, ((0, 0), (0, 128 - N_EXPERTS)))
    rb = jnp.pad(router_b, (0, 128 - N_EXPERTS), constant_values=-1e30).reshape(1, 128)
    kern = functools.partial(_router_kernel, tm=tm, n_ctx=n_ctx, nb=nb)
    return pl.pallas_call(
        kern,
        out_shape=(jax.ShapeDtypeStruct((nb, t, d), BF16), jax.ShapeDtypeStruct((nb, t, 128), F32)),
        grid=(nb, t // tm),
        in_specs=[
            pl.BlockSpec((1, tm, d), lambda b, i: (b, i, 0)),
            pl.BlockSpec((1, d), lambda b, i: (0, 0)),
            pl.BlockSpec(mod.shape, lambda b, i: (0, 0)),
            pl.BlockSpec((d, 128), lambda b, i: (0, 0)),
            pl.BlockSpec((1, 128), lambda b, i: (0, 0)),
        ],
        out_specs=(pl.BlockSpec((1, tm, d), lambda b, i: (b, i, 0)),
                   pl.BlockSpec((1, tm, 128), lambda b, i: (b, i, 0))),
        compiler_params=_cparams(("arbitrary", "arbitrary")),
        name="router",
    )(resid, norm_w.reshape(1, d), mod, rw, rb)


def _moe_kernel(h_ref, comb_ref, x_ref, mod_ref, wg_ref, bg_ref, wu_ref, bu_ref, wd_ref, bd_ref, o_ref,
                acc_ref, *, tm, n_ctx, nb):
    b = pl.program_id(0)
    i = pl.program_id(1)
    e = pl.program_id(2)
    dot = functools.partial(jnp.dot, preferred_element_type=F32)

    @pl.when(e == 0)
    def _():
        acc_ref[...] = jnp.zeros_like(acc_ref)

    h = h_ref[0]
    gl = jnp.minimum(dot(h, wg_ref[0]) + bg_ref[0], SWIGLU_LIMIT)
    li = jnp.clip(dot(h, wu_ref[0]) + bu_ref[0], -SWIGLU_LIMIT, SWIGLU_LIMIT)
    act = gl * jax.nn.sigmoid(SWIGLU_ALPHA * gl) * (li + 1.0)
    y = dot(act.astype(BF16), wd_ref[0]) + bd_ref[0]
    comb = comb_ref[0]
    lane = lax.broadcasted_iota(jnp.int32, comb.shape, 1)
    ce = jnp.sum(jnp.where(lane == e, comb, 0.0), axis=-1, keepdims=True)
    acc_ref[...] += ce * y

    @pl.when(e == N_EXPERTS - 1)
    def _():
        row = i * tm + lax.broadcasted_iota(jnp.int32, (tm, 1), 0)
        gate = _mod_rows(mod_ref, b, nb, 5 * D_MODEL, row < n_ctx)
        o_ref[0] = x_ref[0] + gate * acc_ref[...]


def _moe(h, comb, resid, mod, wg, bg, wu, bu, wd, bd, n_ctx):
    nb, t, d = resid.shape
    tm = _row_tile(t, 1088)
    kern = functools.partial(_moe_kernel, tm=tm, n_ctx=n_ctx, nb=nb)

    def wspec():
        return pl.BlockSpec((1, d, d), lambda b, i, e: (e, 0, 0))

    def bspec():
        return pl.BlockSpec((1, 1, d), lambda b, i, e: (e, 0, 0))

    return pl.pallas_call(
        kern,
        out_shape=jax.ShapeDtypeStruct((nb, t, d), F32),
        grid=(nb, t // tm, N_EXPERTS),
        in_specs=[
            pl.BlockSpec((1, tm, d), lambda b, i, e: (b, i, 0)),
            pl.BlockSpec((1, tm, 128), lambda b, i, e: (b, i, 0)),
            pl.BlockSpec((1, tm, d), lambda b, i, e: (b, i, 0)),
            pl.BlockSpec(mod.shape, lambda b, i, e: (0, 0)),
            wspec(), bspec(), wspec(), bspec(), wspec(), bspec(),
        ],
        out_specs=pl.BlockSpec((1, tm, d), lambda b, i, e: (b, i, 0)),
        scratch_shapes=[pltpu.VMEM((tm, d), F32)],
        compiler_params=_cparams(("arbitrary", "arbitrary", "arbitrary")),
        name="moe_experts",
    )(h, comb, resid, mod, wg.astype(BF16), bg.reshape(N_EXPERTS, 1, d), wu.astype(BF16),
      bu.reshape(N_EXPERTS, 1, d), wd.astype(BF16), bd.reshape(N_EXPERTS, 1, d))


def _final_norm_kernel(x_ref, w_ref, o_ref):
    x = x_ref[0]
    o_ref[0] = x * lax.rsqrt(jnp.mean(x * x, axis=-1, keepdims=True) + NORM_EPS) * w_ref[...]


def _final_norm(resid, w, n_ctx):
    nb, t, d = resid.shape
    tm = n_ctx
    return pl.pallas_call(
        _final_norm_kernel,
        out_shape=jax.ShapeDtypeStruct((nb, t - n_ctx, d), F32),
        grid=(nb, (t - n_ctx) // tm),
        in_specs=[pl.BlockSpec((1, tm, d), lambda b, i: (b, i + 1, 0)),
                  pl.BlockSpec((1, d), lambda b, i: (0, 0))],
        out_specs=pl.BlockSpec((1, tm, d), lambda b, i: (b, i, 0)),
        compiler_params=_cparams(("arbitrary", "arbitrary")),
        name="final_norm",
    )(resid, w.reshape(1, d))


def _pack_w_in(w):
    o_gg = CONV_CH + GLA_V
    o_rg = o_gg + 2 * GLA_GATE_RANK
    o_rw = o_rg + RWKV_G_RANK
    o_ra = o_rw + 2 * RWKV_W_RANK
    o_u = o_ra + 2 * RWKV_A_RANK
    o_gt = o_u + D_MODEL
    pad = jnp.zeros((w.shape[0], 128 - 2 * GLA_GATE_RANK), w.dtype)
    return jnp.concatenate([w[:, :o_gg], w[:, o_u:o_gt], w[:, o_gt:], w[:, o_gg:o_rg], pad,
                            w[:, o_rg:o_rw], w[:, o_rw:o_ra], w[:, o_ra:o_u]], axis=1).astype(BF16)


def kernel(x, c, ctx, c_ctx, w_ada, b_ada, norm_mix, w_in, conv_w, gla_wg2, gla_bg, gla_norm, rw_w0, rw_w2, rw_a0, rw_a2, rw_g2, rw_kk, rw_ka, rw_rk, rw_ln_g, rw_ln_b, s5_lam_re, s5_lam_im, s5_log_dt, s5_b_re, s5_b_im, s5_c_re, s5_c_im, s5_d, s5_glu_w, s5_glu_b, w_branch, w_out, norm_ffn, router_w, router_b, exp_w_gate, exp_b_gate, exp_w_up, exp_b_up, exp_w_down, exp_b_down, final_norm):
    depth = w_ada.shape[0]
    nb, _, d = x.shape
    n_ctx = ctx.shape[1]
    assert d == D_MODEL and n_ctx % RWKV_SEG == 0 and x.shape[1] % RWKV_SEG == 0 and nb < MOD_ROWS

    resid = jnp.concatenate([ctx, x], axis=1).astype(F32)
    cc = jnp.concatenate([c, c_ctx[None], jnp.zeros((MOD_ROWS - nb - 1, d), F32)], axis=0)
    mods = _ada_table(cc, w_ada, b_ada)

    for l in range(depth):
        mod = mods[l]
        p = _norm_in_proj(resid, norm_mix[l], mod, _pack_w_in(w_in[l]), n_ctx)
        pcv = _short_conv(p, conv_w[l], n_ctx)
        y_gla = _gla_mixer(pcv, p, gla_wg2[l], gla_bg[l], gla_norm[l], n_ctx)
        prm = dict(w0=rw_w0[l], w2=rw_w2[l], a0=rw_a0[l], a2=rw_a2[l], g2=rw_g2[l], kk=rw_kk[l], ka=rw_ka[l],
                   rk=rw_rk[l].reshape(-1), ln_g=rw_ln_g[l], ln_b=rw_ln_b[l])
        y_rw = _rwkv_mixer(pcv, p, prm, n_ctx)
        ops = _s5_operators(s5_lam_re[l], s5_lam_im[l], s5_log_dt[l], s5_b_re[l], s5_b_im[l],
                            s5_c_re[l], s5_c_im[l])
        y_s5 = _s5_mixer(p, ops, n_ctx)
        resid = _merge(y_gla, y_rw, y_s5, p, resid, mod, w_branch[l], w_out[l], s5_d[l], s5_glu_w[l],
                       s5_glu_b[l], n_ctx)
        h, comb = _router(resid, norm_ffn[l], mod, router_w[l], router_b[l], n_ctx)
        resid = _moe(h, comb, resid, mod, exp_w_gate[l], exp_b_gate[l], exp_w_up[l], exp_b_up[l],
                     exp_w_down[l], exp_b_down[l], n_ctx)
    return _final_norm(resid, final_norm, n_ctx)
```

```python
import functools
import math

import jax
import jax.numpy as jnp
from jax import lax
from jax.experimental import pallas as pl
from jax.experimental.pallas import tpu as pltpu

F32 = jnp.float32
BF16 = jnp.bfloat16
HI = lax.Precision.HIGHEST

D_MODEL = 1024
GRID_W = 64

GLA_HEADS = 4
GLA_DK = 128
GLA_DV = 256
GLA_QK = GLA_HEADS * GLA_DK
GLA_V = GLA_HEADS * GLA_DV
GLA_GATE_RANK = 16
GLA_GATE_NORM = 16.0
CHUNK = 64
GLA_GROUP = 4

RWKV_HEAD = 64
RWKV_DIM = 1024
RWKV_W_RANK = 64
RWKV_A_RANK = 64
RWKV_G_RANK = 128
RWKV_W_SCALE = math.exp(-0.5)
RWKV_LN_EPS = 64e-5
RWKV_LANES = 256
RWKV_HPB = RWKV_LANES // RWKV_HEAD
RWKV_SEG = 256

S5_GROUP_CH = 16
S5_GROUPS = 64
S5_STATE = 64
S5_CHUNK = 16
S5_LANE_GROUPS = 8
MOD_ROWS = 8

N_EXPERTS = 32
TOP_K = 4
SWIGLU_LIMIT = 7.0
SWIGLU_ALPHA = 1.702
NORM_EPS = 1e-6

GLA_CONV = 2 * GLA_QK + GLA_V
CONV_CH = GLA_CONV + 3 * RWKV_DIM

OFF_PC = 0
OFF_OG = CONV_CH
OFF_U = OFF_OG + GLA_V
OFF_GATES = OFF_U + D_MODEL
OFF_GG = OFF_GATES + 3 * D_MODEL
OFF_RG = OFF_GG + 128
OFF_RW = OFF_RG + 128
OFF_RA = OFF_RW + 128
NP = OFF_RA + 128

VMEM_LIMIT = 56 * 1024 * 1024


def _cparams(sem):
    return pltpu.CompilerParams(dimension_semantics=sem, vmem_limit_bytes=VMEM_LIMIT)


def _row_tile(t, cap):
    best = 8
    for d in range(8, min(t, cap) + 1, 8):
        if t % d == 0:
            best = d
    return best


def _silu(x):
    return x * jax.nn.sigmoid(x)


def _log_sigmoid(z):
    return jnp.minimum(z, 0.0) - jnp.log(1.0 + jnp.exp(-jnp.abs(z)))


def _tri_matmul(keep, x):
    n = x.shape[1]
    h1 = x.astype(BF16)
    r1 = x - h1.astype(F32)
    h2 = r1.astype(BF16)
    h3 = (r1 - h2.astype(F32)).astype(BF16)
    y = jnp.dot(keep.astype(BF16), jnp.concatenate([h1, h2, h3], axis=1), preferred_element_type=F32)
    return y[:, 0:n] + y[:, n:2 * n] + y[:, 2 * n:3 * n]


def _row_to_col(row):
    return jnp.transpose(jnp.broadcast_to(row, (128, row.shape[1])))[:, 0:1]


def _mod_rows(mod_ref, b, nb, off, is_ctx):
    vb = mod_ref[pl.ds(b, 1), off:off + D_MODEL]
    vc = mod_ref[nb:nb + 1, off:off + D_MODEL]
    return jnp.where(is_ctx, vc, vb)


def _ada_kernel(c_ref, w_ref, b_ref, o_ref):
    a = _silu(c_ref[...])
    o_ref[0] = jnp.dot(a, w_ref[0], precision=HI, preferred_element_type=F32) + b_ref[0]


def _ada_table(cc, w_ada, b_ada):
    depth, d, n = w_ada.shape
    tn = 1536
    return pl.pallas_call(
        _ada_kernel,
        out_shape=jax.ShapeDtypeStruct((depth, cc.shape[0], n), F32),
        grid=(depth, n // tn),
        in_specs=[
            pl.BlockSpec((cc.shape[0], d), lambda l, j: (0, 0)),
            pl.BlockSpec((1, d, tn), lambda l, j: (l, 0, j)),
            pl.BlockSpec((1, 1, tn), lambda l, j: (l, 0, j)),
        ],
        out_specs=pl.BlockSpec((1, cc.shape[0], tn), lambda l, j: (l, 0, j)),
        compiler_params=_cparams(("arbitrary", "arbitrary")),
        name="ada_table",
    )(cc, w_ada, b_ada.reshape(depth, 1, n))


def _norm_mm_kernel(x_ref, nw_ref, mod_ref, w_ref, o_ref, h_ref, *, tm, n_ctx, nb):
    b = pl.program_id(0)
    i = pl.program_id(1)

    @pl.when(pl.program_id(2) == 0)
    def _():
        x = x_ref[0]
        xn = x * lax.rsqrt(jnp.mean(x * x, axis=-1, keepdims=True) + NORM_EPS) * nw_ref[...]
        row = i * tm + lax.broadcasted_iota(jnp.int32, (tm, 1), 0)
        is_ctx = row < n_ctx
        sh = _mod_rows(mod_ref, b, nb, 0, is_ctx)
        sc = _mod_rows(mod_ref, b, nb, D_MODEL, is_ctx)
        h_ref[...] = (xn * (1.0 + sc) + sh).astype(BF16)

    o_ref[0] = jnp.dot(h_ref[...], w_ref[...], preferred_element_type=F32)


def _norm_in_proj(resid, norm_w, mod, w_in_p, n_ctx):
    nb, t, d = resid.shape
    tm = _row_tile(t, 1088)
    tn = 768
    kern = functools.partial(_norm_mm_kernel, tm=tm, n_ctx=n_ctx, nb=nb)
    return pl.pallas_call(
        kern,
        out_shape=jax.ShapeDtypeStruct((nb, t, NP), F32),
        grid=(nb, t // tm, NP // tn),
        in_specs=[
            pl.BlockSpec((1, tm, d), lambda b, i, j: (b, i, 0)),
            pl.BlockSpec((1, d), lambda b, i, j: (0, 0)),
            pl.BlockSpec(mod.shape, lambda b, i, j: (0, 0)),
            pl.BlockSpec((d, tn), lambda b, i, j: (0, j)),
        ],
        out_specs=pl.BlockSpec((1, tm, tn), lambda b, i, j: (b, i, j)),
        scratch_shapes=[pltpu.VMEM((tm, d), BF16)],
        compiler_params=_cparams(("arbitrary", "arbitrary", "arbitrary")),
        name="norm_in_proj",
    )(resid, norm_w.reshape(1, d), mod, w_in_p)


def _conv_kernel(x_ref, w_ref, o_ref, xp_ref, xl_ref, xr_ref, *, n_ctx, rows):
    tc = x_ref.shape[-1]
    gw = GRID_W

    def wrow(k):
        return w_ref[k:k + 1, :]

    xc = x_ref[0, 0:n_ctx, :]
    ci = lax.broadcasted_iota(jnp.int32, (n_ctx, tc), 0)
    left = jnp.where(ci > 0, pltpu.roll(xc, 1, 0), 0.0)
    right = jnp.where(ci < n_ctx - 1, pltpu.roll(xc, n_ctx - 1, 0), 0.0)
    o_ref[0, 0:n_ctx, :] = left * wrow(3) + xc * wrow(4) + right * wrow(5)

    zeros = jnp.zeros((gw, tc), F32)
    for ref in (xp_ref, xl_ref, xr_ref):
        ref[0:gw, :] = zeros
        ref[(rows + 1) * gw:(rows + 2) * gw, :] = zeros
    wi = lax.broadcasted_iota(jnp.int32, (gw, tc), 0)

    def fill(r, carry):
        base = pl.multiple_of(r * gw, gw)
        xrow = x_ref[0, pl.ds(n_ctx + base, gw), :]
        xp_ref[pl.ds(gw + base, gw), :] = xrow
        xl_ref[pl.ds(gw + base, gw), :] = jnp.where(wi > 0, pltpu.roll(xrow, 1, 0), 0.0)
        xr_ref[pl.ds(gw + base, gw), :] = jnp.where(wi < gw - 1, pltpu.roll(xrow, gw - 1, 0), 0.0)
        return carry

    lax.fori_loop(0, rows, fill, 0)

    def comp(r, carry):
        base = pl.multiple_of(r * gw, gw)
        acc = jnp.zeros((gw, tc), F32)
        for dr in range(3):
            sl = pl.ds(base + gw * dr, gw)
            acc = acc + xl_ref[sl, :] * wrow(3 * dr) + xp_ref[sl, :] * wrow(3 * dr + 1) \
                + xr_ref[sl, :] * wrow(3 * dr + 2)
        o_ref[0, pl.ds(n_ctx + base, gw), :] = acc
        return carry

    lax.fori_loop(0, rows, comp, 0)


def _short_conv(p, conv_w, n_ctx):
    nb, t, _ = p.shape
    rows = (t - n_ctx) // GRID_W
    tc = 256
    kern = functools.partial(_conv_kernel, n_ctx=n_ctx, rows=rows)
    pad_rows = (rows + 2) * GRID_W
    return pl.pallas_call(
        kern,
        out_shape=jax.ShapeDtypeStruct((nb, t, CONV_CH), F32),
        grid=(nb, CONV_CH // tc),
        in_specs=[
            pl.BlockSpec((1, t, tc), lambda b, j: (b, 0, j)),
            pl.BlockSpec((9, tc), lambda b, j: (0, j)),
        ],
        out_specs=pl.BlockSpec((1, t, tc), lambda b, j: (b, 0, j)),
        scratch_shapes=[pltpu.VMEM((pad_rows, tc), F32)] * 3,
        compiler_params=_cparams(("arbitrary", "arbitrary")),
        name="short_conv",
    )(p, conv_w.reshape(9, CONV_CH))


def _gla_kernel(q_ref, k_ref, v_ref, gg_ref, og_ref, wg_ref, bg_ref, ng_ref, o_ref, ob_ref, *, n_ctx, t):
    c = CHUNK
    nch = t // c
    nctx = n_ctx // c
    ii = lax.broadcasted_iota(jnp.int32, (c, c), 0)
    jj = lax.broadcasted_iota(jnp.int32, (c, c), 1)
    scale = GLA_DK ** -0.5

    nt = (((1,), (1,)), ((), ()))
    tn = (((0,), (0,)), ((), ()))
    dot = functools.partial(jnp.dot, preferred_element_type=F32)
    grp = GLA_GROUP
    assert nctx % grp == 0 and nch % grp == 0

    def group(g, carry):
        chains = []
        for d in range(2):
            for u in range(grp):
                i = g * grp + u
                ci = i if d == 0 else jnp.where(i < nctx, nctx - 1 - i, nch + nctx - 1 - i)
                chains.append((d, pl.ds(pl.multiple_of(ci * c, c), c)))
        keeps = [(ii >= jj) if d == 0 else (ii <= jj) for d, _ in chains]
        q = [_silu(q_ref[0, r, :]) * scale for _, r in chains]
        k = [_silu(k_ref[0, r, :]) for _, r in chains]
        vb = [_silu(v_ref[0, r, :]).astype(BF16) for _, r in chains]
        z = [dot(gg_ref[0, r, :][:, GLA_GATE_RANK * d:GLA_GATE_RANK * (d + 1)], wg_ref[d]) + bg_ref[d]
             for d, r in chains]
        la = [_log_sigmoid(x) * (1.0 / GLA_GATE_NORM) for x in z]
        bcum = [_tri_matmul(kp, x) for kp, x in zip(keeps, la)]
        blast = [b[c - 1:c, :] if d == 0 else b[0:1, :] for (d, _), b in zip(chains, bcum)]
        q_dec = [(x * jnp.exp(b)).astype(BF16) for x, b in zip(q, bcum)]
        k_inv = [(x * jnp.exp(-b)).astype(BF16) for x, b in zip(k, bcum)]
        k_end = [(x * jnp.exp(bl - b)).astype(BF16) for x, b, bl in zip(k, bcum, blast)]
        sc = [lax.dot_general(a, b, nt, preferred_element_type=F32) for a, b in zip(q_dec, k_inv)]
        sc = [jnp.where(kp, x, 0.0).astype(BF16) for kp, x in zip(keeps, sc)]
        intra = [dot(a, b) for a, b in zip(sc, vb)]
        kv = [lax.dot_general(a, b, tn, preferred_element_type=F32) for a, b in zip(k_end, vb)]
        dec = [_row_to_col(jnp.exp(bl)) for bl in blast]
        states = list(carry)
        for n, (d, r) in enumerate(chains):
            s = states[d]
            o = intra[n] + dot(q_dec[n], s.astype(BF16))
            if d == 0:
                o_ref[0, r, :] = o
            else:
                ob_ref[r, :] = o
            states[d] = dec[n] * s + kv[n]
        return tuple(states)

    s0 = jnp.zeros((GLA_DK, GLA_DV), F32)
    lax.fori_loop(0, nch // grp, group, (s0, s0))

    def finish(ci, carry):
        rows = pl.ds(pl.multiple_of(ci * c, c), c)
        o = o_ref[0, rows, :] + ob_ref[rows, :]
        o = o * lax.rsqrt(jnp.mean(o * o, axis=-1, keepdims=True) + 1e-5) * ng_ref[...]
        o_ref[0, rows, :] = o * _silu(og_ref[0, rows, :])
        return carry

    lax.fori_loop(0, nch, finish, 0, unroll=2)


def _gla_mixer(pcv, p, wg2, bg, norm_g, n_ctx):
    nb, t, _ = pcv.shape
    kern = functools.partial(_gla_kernel, n_ctx=n_ctx, t=t)
    return pl.pallas_call(
        kern,
        out_shape=jax.ShapeDtypeStruct((nb, t, GLA_V), F32),
        grid=(nb, GLA_HEADS),
        in_specs=[
            pl.BlockSpec((1, t, GLA_DK), lambda b, h: (b, 0, h)),
            pl.BlockSpec((1, t, GLA_DK), lambda b, h: (b, 0, GLA_HEADS + h)),
            pl.BlockSpec((1, t, GLA_DV), lambda b, h: (b, 0, GLA_HEADS + h)),
            pl.BlockSpec((1, t, 128), lambda b, h: (b, 0, OFF_GG // 128)),
            pl.BlockSpec((1, t, GLA_DV), lambda b, h: (b, 0, OFF_OG // GLA_DV + h)),
            pl.BlockSpec((2, GLA_GATE_RANK, GLA_DK), lambda b, h: (0, 0, h)),
            pl.BlockSpec((2, 1, GLA_DK), lambda b, h: (0, 0, h)),
            pl.BlockSpec((1, GLA_DV), lambda b, h: (0, 0)),
        ],
        out_specs=pl.BlockSpec((1, t, GLA_DV), lambda b, h: (b, 0, h)),
        scratch_shapes=[pltpu.VMEM((t, GLA_DV), F32)],
        compiler_params=_cparams(("arbitrary", "arbitrary")),
        name="gla_mixer",
    )(pcv, pcv, pcv, p, p, wg2, bg.reshape(2, 1, GLA_QK), norm_g.reshape(1, GLA_DV))


def _bd_mask():
    shift = RWKV_HEAD.bit_length() - 1
    r = lax.broadcasted_iota(jnp.int32, (RWKV_LANES, RWKV_LANES), 0) >> shift
    c = lax.broadcasted_iota(jnp.int32, (RWKV_LANES, RWKV_LANES), 1) >> shift
    return r == c


def _bd(x, mask):
    xb = x.astype(BF16)
    return jnp.where(mask, jnp.concatenate([xb] * RWKV_HPB, axis=0), jnp.zeros((), BF16))


def _head_sum(x, ones_bd):
    hi = x.astype(BF16)
    lo = (x - hi.astype(F32)).astype(BF16)
    return jnp.dot(hi, ones_bd, preferred_element_type=F32) + jnp.dot(lo, ones_bd, preferred_element_type=F32)


def _rwkv_inputs(r_ref, k_ref, rw_ref, ra_ref, w0_ref, w2_ref, a0_ref, a2_ref, kkp_ref, kap_ref, rows, d,
                 ones_bd):
    r = r_ref[0, rows, :]
    k = k_ref[0, rows, :]
    pw = rw_ref[0, rows, :][:, RWKV_W_RANK * d:RWKV_W_RANK * (d + 1)]
    pa = ra_ref[0, rows, :][:, RWKV_A_RANK * d:RWKV_A_RANK * (d + 1)]
    lw = w0_ref[d] + jnp.dot(jnp.tanh(pw), w2_ref[d], preferred_element_type=F32)
    logw = -RWKV_W_SCALE * jax.nn.sigmoid(lw)
    la = a0_ref[d] + jnp.dot(pa, a2_ref[d], preferred_element_type=F32)
    a = jax.nn.sigmoid(la)
    kkv = k * kkp_ref[...]
    nrm = jnp.maximum(jnp.sqrt(_head_sum(kkv * kkv, ones_bd)), 1e-12)
    kk = kkv / nrm
    kd = k * (1.0 + (a - 1.0) * kap_ref[...])
    return r, logw, kd, kk, kk * a


def _rwkv_chunk(st, r, logw, kd, v, kk, kka, d, mask_bd):
    c = CHUNK
    ii = lax.broadcasted_iota(jnp.int32, (c, c), 0)
    jj = lax.broadcasted_iota(jnp.int32, (c, c), 1)
    keep_sq = (ii >= jj) if d == 0 else (ii <= jj)
    ti = lax.broadcasted_iota(jnp.int32, (c, RWKV_LANES), 0)
    si = lax.broadcasted_iota(jnp.int32, (c, RWKV_LANES), 1) & (c - 1)
    incl = (si <= ti) if d == 0 else (si >= ti)
    strict = (si < ti) if d == 0 else (si > ti)
    eye = (si == ti).astype(F32)

    cb = _tri_matmul(keep_sq, logw)
    cblast = cb[c - 1:c, :] if d == 0 else cb[0:1, :]
    ad = -kk * jnp.exp(cb - logw)
    rd = r * jnp.exp(cb)
    einv = jnp.exp(-cb)
    bi = kka * einv
    ki = kd * einv
    eend = jnp.exp(cblast - cb)
    bend = kka * eend
    kend = kd * eend

    nt = (((1,), (1,)), ((), ()))
    tn = (((0,), (0,)), ((), ()))
    lhs = jnp.concatenate([ad, rd], axis=0).astype(BF16)
    pb = lax.dot_general(lhs, _bd(bi, mask_bd), nt, preferred_element_type=F32)
    pk = lax.dot_general(lhs, _bd(ki, mask_bd), nt, preferred_element_type=F32)
    a_ab = jnp.where(strict, pb[0:c], 0.0)
    r_ab = jnp.where(incl, pb[c:2 * c], 0.0)
    a_ak = jnp.where(strict, pk[0:c], 0.0)
    r_ak = jnp.where(incl, pk[c:2 * c], 0.0)

    def mm(x, y):
        return jnp.dot(x.astype(BF16), _bd(y, mask_bd), preferred_element_type=F32)

    tinv = eye + a_ab
    pw = a_ab
    for _ in range(5):
        pw = mm(pw, pw)
        tinv = tinv + mm(pw, tinv)

    stb = st.astype(BF16)
    vbd = _bd(v, mask_bd)
    x = jnp.dot(ad.astype(BF16), stb, preferred_element_type=F32) \
        + jnp.dot(a_ak.astype(BF16), vbd, preferred_element_type=F32)
    u = mm(tinv, x)
    y = jnp.dot(rd.astype(BF16), stb, preferred_element_type=F32) + mm(r_ab, u) \
        + jnp.dot(r_ak.astype(BF16), vbd, preferred_element_type=F32)
    dec = _row_to_col(jnp.exp(cblast))
    upd = lax.dot_general(jnp.concatenate([bend, kend], axis=0).astype(BF16),
                          jnp.concatenate([u, v], axis=0).astype(BF16), tn, preferred_element_type=F32)
    st = dec * st + jnp.where(mask_bd, upd, 0.0)
    return y, st


def _rwkv_scan_kernel(rf_ref, kf_ref, vf_ref, rwf_ref, raf_ref, rb_ref, kb_ref, vb_ref, rwb_ref, rab_ref,
                      w0_ref, w2_ref, a0_ref, a2_ref, kkp_ref, kap_ref, yf_ref, yb_ref, stf_ref, stb_ref):
    c = CHUNK
    nchunk = RWKV_SEG // c
    mask_bd = _bd_mask()
    ones_bd = mask_bd.astype(BF16)

    @pl.when(pl.program_id(2) == 0)
    def _():
        stf_ref[...] = jnp.zeros_like(stf_ref)
        stb_ref[...] = jnp.zeros_like(stb_ref)

    sf = stf_ref[...]
    sb = stb_ref[...]
    for ci in range(nchunk):
        rows = slice(ci * c, (ci + 1) * c)
        r, logw, kd, kk, kka = _rwkv_inputs(rf_ref, kf_ref, rwf_ref, raf_ref, w0_ref, w2_ref, a0_ref, a2_ref,
                                            kkp_ref, kap_ref, rows, 0, ones_bd)
        y, sf = _rwkv_chunk(sf, r, logw, kd, vf_ref[0, rows, :], kk, kka, 0, mask_bd)
        yf_ref[0, rows, :] = y
        cj = nchunk - 1 - ci
        rows = slice(cj * c, (cj + 1) * c)
        r, logw, kd, kk, kka = _rwkv_inputs(rb_ref, kb_ref, rwb_ref, rab_ref, w0_ref, w2_ref, a0_ref, a2_ref,
                                            kkp_ref, kap_ref, rows, 1, ones_bd)
        y, sb = _rwkv_chunk(sb, r, logw, kd, vb_ref[0, rows, :], kk, kka, 1, mask_bd)
        yb_ref[0, rows, :] = y
    stf_ref[...] = sf
    stb_ref[...] = sb


def _rwkv_group_kernel(rf_ref, kf_ref, vf_ref, rwf_ref, raf_ref, rb_ref, kb_ref, vb_ref, rwb_ref, rab_ref,
                       w0_ref, w2_ref, a0_ref, a2_ref, kkp_ref, kap_ref, yf_ref, yb_ref, stf_ref, stb_ref):
    c = CHUNK
    nchunk = RWKV_SEG // c
    mask_bd = _bd_mask()
    ones_bd = mask_bd.astype(BF16)
    dot = functools.partial(jnp.dot, preferred_element_type=F32)
    nt = (((1,), (1,)), ((), ()))
    tn = (((0,), (0,)), ((), ()))

    @pl.when(pl.program_id(2) == 0)
    def _():
        stf_ref[...] = jnp.zeros_like(stf_ref)
        stb_ref[...] = jnp.zeros_like(stb_ref)

    chains = [(0, slice(u * c, (u + 1) * c)) for u in range(nchunk)] \
        + [(1, slice(u * c, (u + 1) * c)) for u in range(nchunk - 1, -1, -1)]
    refs = ((rf_ref, kf_ref, vf_ref, rwf_ref, raf_ref), (rb_ref, kb_ref, vb_ref, rwb_ref, rab_ref))

    def each(f, *lists):
        return [f(*xs) for xs in zip(*lists)]

    ds = [d for d, _ in chains]
    r = [refs[d][0][0, rows, :] for d, rows in chains]
    k = [refs[d][1][0, rows, :] for d, rows in chains]
    v = [refs[d][2][0, rows, :] for d, rows in chains]
    pw = [refs[d][3][0, rows, :][:, RWKV_W_RANK * d:RWKV_W_RANK * (d + 1)] for d, rows in chains]
    pa = [refs[d][4][0, rows, :][:, RWKV_A_RANK * d:RWKV_A_RANK * (d + 1)] for d, rows in chains]
    lw = each(lambda d, x: w0_ref[d] + dot(jnp.tanh(x), w2_ref[d]), ds, pw)
    logw = [-RWKV_W_SCALE * jax.nn.sigmoid(x) for x in lw]
    la = each(lambda d, x: a0_ref[d] + dot(x, a2_ref[d]), ds, pa)
    a = [jax.nn.sigmoid(x) for x in la]
    kkv = [x * kkp_ref[...] for x in k]
    ss = [_head_sum(x * x, ones_bd) for x in kkv]
    kk = each(lambda x, s: x / jnp.maximum(jnp.sqrt(s), 1e-12), kkv, ss)
    kd = each(lambda x, y: x * (1.0 + (y - 1.0) * kap_ref[...]), k, a)
    kka = each(lambda x, y: x * y, kk, a)

    ii = lax.broadcasted_iota(jnp.int32, (c, c), 0)
    jj = lax.broadcasted_iota(jnp.int32, (c, c), 1)
    ti = lax.broadcasted_iota(jnp.int32, (c, RWKV_LANES), 0)
    si = lax.broadcasted_iota(jnp.int32, (c, RWKV_LANES), 1) & (c - 1)
    keep_sq = [(ii >= jj) if d == 0 else (ii <= jj) for d in ds]
    incl = [(si <= ti) if d == 0 else (si >= ti) for d in ds]
    strict = [(si < ti) if d == 0 else (si > ti) for d in ds]
    eye = (si == ti).astype(F32)

    cb = each(_tri_matmul, keep_sq, logw)
    cblast = each(lambda d, x: x[c - 1:c, :] if d == 0 else x[0:1, :], ds, cb)
    ad = each(lambda x, b, w: -x * jnp.exp(b - w), kk, cb, logw)
    rd = each(lambda x, b: x * jnp.exp(b), r, cb)
    einv = [jnp.exp(-b) for b in cb]
    bi = each(lambda x, e: x * e, kka, einv)
    ki = each(lambda x, e: x * e, kd, einv)
    eend = each(lambda bl, b: jnp.exp(bl - b), cblast, cb)
    bend = each(lambda x, e: x * e, kka, eend)
    kend = each(lambda x, e: x * e, kd, eend)

    def bd(x):
        return _bd(x, mask_bd)

    def mm(x, y_bd):
        return dot(x.astype(BF16), y_bd)

    lhs = each(lambda x, y: jnp.concatenate([x, y], axis=0).astype(BF16), ad, rd)
    pb = each(lambda x, y: lax.dot_general(x, bd(y), nt, preferred_element_type=F32), lhs, bi)
    pk = each(lambda x, y: lax.dot_general(x, bd(y), nt, preferred_element_type=F32), lhs, ki)
    a_ab = each(lambda m, x: jnp.where(m, x[0:c], 0.0), strict, pb)
    r_ab = each(lambda m, x: jnp.where(m, x[c:2 * c], 0.0), incl, pb)
    a_ak = each(lambda m, x: jnp.where(m, x[0:c], 0.0), strict, pk)
    r_ak = each(lambda m, x: jnp.where(m, x[c:2 * c], 0.0), incl, pk)

    tinv = [eye + x for x in a_ab]
    pwr = a_ab
    for _ in range(5):
        pwr = each(lambda x: mm(x, bd(x)), pwr)
        tinv = each(lambda t_, p_: t_ + mm(p_, bd(t_)), tinv, pwr)

    vbd = [bd(x) for x in v]
    m1 = each(lambda t_, x: mm(t_, bd(x)), tinv, ad)
    u0 = each(lambda t_, x, vb_: mm(t_, bd(dot(x.astype(BF16), vb_))), tinv, a_ak, vbd)
    r1 = each(lambda x, rab, m: (x + mm(rab, bd(m))).astype(BF16), rd, r_ab, m1)
    y0 = each(lambda rab, x, rak, vb_: mm(rab, bd(x)) + dot(rak.astype(BF16), vb_), r_ab, u0, r_ak, vbd)
    g1 = each(lambda x, m: jnp.where(mask_bd, lax.dot_general(x.astype(BF16), m.astype(BF16), tn,
                                                              preferred_element_type=F32), 0.0).astype(BF16),
              bend, m1)
    h = each(lambda x, y, u_, v_: jnp.where(mask_bd, lax.dot_general(
        jnp.concatenate([x, y], axis=0).astype(BF16), jnp.concatenate([u_, v_], axis=0).astype(BF16), tn,
        preferred_element_type=F32), 0.0), bend, kend, u0, v)
    dec = [_row_to_col(jnp.exp(x)) for x in cblast]

    states = [stf_ref[...], stb_ref[...]]
    outs = (yf_ref, yb_ref)
    for n, (d, rows) in enumerate(chains):
        st = states[d]
        stb = st.astype(BF16)
        outs[d][0, rows, :] = dot(r1[n], stb) + y0[n]
        states[d] = dec[n] * st + dot(g1[n], stb) + h[n]
    stf_ref[...] = states[0]
    stb_ref[...] = states[1]


def _rwkv_post_kernel(r_ref, k_ref, v_ref, ra_ref, rg_ref, yf_ref, yb_ref, a0_ref, a2_ref, g2_ref,
                      kap_ref, rkp_ref, lng_ref, lnb_ref, o_ref):
    ones_bd = _bd_mask().astype(BF16)
    r = r_ref[0]
    k = k_ref[0]
    ra = ra_ref[0]
    y = yf_ref[0] + yb_ref[0]
    inv_n = 1.0 / RWKV_HEAD
    mu = _head_sum(y, ones_bd) * inv_n
    yc = y - mu
    var = _head_sum(yc * yc, ones_bd) * inv_n
    yn = yc * lax.rsqrt(var + RWKV_LN_EPS) * lng_ref[...] + lnb_ref[...]
    kds = jnp.zeros_like(k)
    for d in range(2):
        la = a0_ref[d] + jnp.dot(ra[:, RWKV_A_RANK * d:RWKV_A_RANK * (d + 1)], a2_ref[d],
                                 preferred_element_type=F32)
        kds = kds + k * (1.0 + (jax.nn.sigmoid(la) - 1.0) * kap_ref[...])
    bonus = _head_sum(r * kds * rkp_ref[...], ones_bd) * v_ref[0]
    g = jnp.dot(jax.nn.sigmoid(rg_ref[0]), g2_ref[...], preferred_element_type=F32)
    o_ref[0] = (yn + bonus) * g


def _rwkv_mixer(pcv, p, prm, n_ctx):
    nb, t, _ = pcv.shape
    seg = RWKV_SEG
    nseg = t // seg
    nseg_ctx = n_ctx // seg
    ln = RWKV_LANES
    nhb = RWKV_DIM // ln
    cb0 = GLA_CONV // ln

    def fmap(s):
        return s

    def bmap(s):
        return jnp.where(s < nseg_ctx, nseg_ctx - 1 - s, nseg - 1 - (s - nseg_ctx))

    def tok(off_blocks, smap):
        return pl.BlockSpec((1, seg, ln), lambda b, h, s: (b, smap(s), off_blocks + h))

    def small(off, smap):
        return pl.BlockSpec((1, seg, 128), lambda b, h, s: (b, smap(s), off // 128))

    def dirspecs(smap):
        return [tok(cb0, smap), tok(cb0 + nhb, smap), tok(cb0 + 2 * nhb, smap),
                small(OFF_RW, smap), small(OFF_RA, smap)]

    def vec():
        return pl.BlockSpec((1, ln), lambda b, h, s: (0, h))

    w0 = prm["w0"].reshape(2, 1, RWKV_DIM)
    a0 = prm["a0"].reshape(2, 1, RWKV_DIM)
    kkp = prm["kk"].reshape(1, RWKV_DIM)
    kap = prm["ka"].reshape(1, RWKV_DIM)
    yf, yb = pl.pallas_call(
        _rwkv_group_kernel,
        out_shape=(jax.ShapeDtypeStruct((nb, t, RWKV_DIM), F32),) * 2,
        grid=(nb, nhb, nseg),
        in_specs=dirspecs(fmap) + dirspecs(bmap) + [
            pl.BlockSpec((2, 1, ln), lambda b, h, s: (0, 0, h)),
            pl.BlockSpec((2, RWKV_W_RANK, ln), lambda b, h, s: (0, 0, h)),
            pl.BlockSpec((2, 1, ln), lambda b, h, s: (0, 0, h)),
            pl.BlockSpec((2, RWKV_A_RANK, ln), lambda b, h, s: (0, 0, h)),
            vec(), vec()],
        out_specs=(pl.BlockSpec((1, seg, ln), lambda b, h, s: (b, fmap(s), h)),
                   pl.BlockSpec((1, seg, ln), lambda b, h, s: (b, bmap(s), h))),
        scratch_shapes=[pltpu.VMEM((ln, ln), F32)] * 2,
        compiler_params=_cparams(("arbitrary", "arbitrary", "arbitrary")),
        name="rwkv7_scan",
    )(pcv, pcv, pcv, p, p, pcv, pcv, pcv, p, p, w0, prm["w2"], a0, prm["a2"], kkp, kap)

    tm = _row_tile(t, 1088)

    def ptok(off_blocks):
        return pl.BlockSpec((1, tm, ln), lambda b, i, h: (b, i, off_blocks + h))

    def psmall(off):
        return pl.BlockSpec((1, tm, 128), lambda b, i, h: (b, i, off // 128))

    def pvec():
        return pl.BlockSpec((1, ln), lambda b, i, h: (0, h))

    return pl.pallas_call(
        _rwkv_post_kernel,
        out_shape=jax.ShapeDtypeStruct((nb, t, RWKV_DIM), F32),
        grid=(nb, t // tm, nhb),
        in_specs=[ptok(cb0), ptok(cb0 + nhb), ptok(cb0 + 2 * nhb), psmall(OFF_RA), psmall(OFF_RG),
                  ptok(0), ptok(0),
                  pl.BlockSpec((2, 1, ln), lambda b, i, h: (0, 0, h)),
                  pl.BlockSpec((2, RWKV_A_RANK, ln), lambda b, i, h: (0, 0, h)),
                  pl.BlockSpec((RWKV_G_RANK, ln), lambda b, i, h: (0, h)),
                  pvec(), pvec(), pvec(), pvec()],
        out_specs=ptok(0),
        compiler_params=_cparams(("arbitrary", "arbitrary", "arbitrary")),
        name="rwkv7_post",
    )(pcv, pcv, pcv, p, p, yf, yb, a0, prm["a2"], prm["g2"], kap, prm["rk"].reshape(1, RWKV_DIM),
      prm["ln_g"].reshape(1, RWKV_DIM), prm["ln_b"].reshape(1, RWKV_DIM))


def _s5_operators(lam_re, lam_im, log_dt, b_re, b_im, c_re, c_im):
    g, p_, lc = S5_GROUPS, S5_STATE, S5_CHUNK
    lr, li = lam_re.astype(F32), lam_im.astype(F32)
    br, bi = b_re.astype(F32), b_im.astype(F32)
    den = lr * lr + li * li
    n = jnp.arange(lc + 1, dtype=F32)[:, None, None]
    ein = functools.partial(jnp.einsum, precision=HI)
    out = []
    for d in range(2):
        dt = jnp.exp(log_dt[d].astype(F32))[:, None]
        mag = jnp.exp(n * (lr * dt)[None])
        ang = n * (li * dt)[None]
        pr, pi = mag * jnp.cos(ang), mag * jnp.sin(ang)
        m1 = jnp.exp(lr * dt)
        ar, ai = m1 * jnp.cos(li * dt), m1 * jnp.sin(li * dt)
        fr = ((ar - 1.0) * lr + ai * li) / den
        fi = (ai * lr - (ar - 1.0) * li) / den
        bb_r = fr[..., None] * br - fi[..., None] * bi
        bb_i = fr[..., None] * bi + fi[..., None] * br
        cr, ci = c_re[d].astype(F32), c_im[d].astype(F32)
        ca_r = cr[None] * pr[:, :, None, :] - ci[None] * pi[:, :, None, :]
        ca_i = cr[None] * pi[:, :, None, :] + ci[None] * pr[:, :, None, :]
        kt = ein('tgop,gpc->tgoc', ca_r[:lc], bb_r) - ein('tgop,gpc->tgoc', ca_i[:lc], bb_i)
        nlg, gl, ch = S5_LANE_GROUPS, S5_GROUPS // S5_LANE_GROUPS, S5_GROUP_CH
        eye = jnp.eye(gl, dtype=F32)
        ktc = kt.transpose(0, 1, 3, 2).reshape(lc, nlg, gl, ch, ch)
        blk = (ktc[:, :, :, :, None, :] * eye[None, None, :, None, :, None]).reshape(lc, nlg, 128, 128)
        zer = jnp.zeros((lc - 1, nlg, 128, 128), F32)
        lagblk = jnp.concatenate([zer, blk], axis=0) if d == 0 else jnp.concatenate([blk[::-1], zer], axis=0)
        lagblk = lagblk.transpose(1, 0, 2, 3)

        e_s = (lc - 1 - jnp.arange(lc)) if d == 0 else jnp.arange(lc)
        ps_r, ps_i = pr[e_s], pi[e_s]
        bst_r = ps_r[..., None] * bb_r[None] - ps_i[..., None] * bb_i[None]
        bst_i = ps_r[..., None] * bb_i[None] + ps_i[..., None] * bb_r[None]

        def rows_op(x):
            x = x.reshape(lc, nlg, gl, p_, ch).transpose(1, 0, 2, 4, 3)
            y = x[:, :, :, :, None, :] * eye[None, None, :, None, :, None]
            return y.reshape(nlg, lc * gl * ch, gl * p_)

        f_t = (jnp.arange(lc) + 1) if d == 0 else (lc - jnp.arange(lc))

        def cols_op(x):
            x = x.reshape(lc, nlg, gl, ch, p_).transpose(1, 2, 4, 0, 3)
            y = x[:, :, :, :, None, :] * eye[None, :, None, None, :, None]
            return y.reshape(nlg, gl * p_, lc * gl * ch)

        out.append(dict(
            lag=lagblk.astype(BF16), bst_r=rows_op(bst_r).astype(BF16), bst_i=rows_op(bst_i).astype(BF16),
            co_r=cols_op(ca_r[f_t]).astype(BF16), co_i=cols_op(-ca_i[f_t]).astype(BF16),
            a_r=pr[lc].reshape(nlg, 1, gl * p_), a_i=pi[lc].reshape(nlg, 1, gl * p_)))
    return {k: jnp.stack([out[0][k], out[1][k]]) for k in out[0]}


def _s5_kernel(u_ref, lag_ref, br_ref, bi_ref, cr_ref, ci_ref, ar_ref, ai_ref, o_ref,
               ir_ref, ii_ref, xr_ref, xi_ref, op_ref, *, nchunk, nchunk_ctx):
    d = pl.program_id(1)
    lc = S5_CHUNK
    dot = functools.partial(jnp.dot, preferred_element_type=F32)

    @pl.when(pl.program_id(2) == 0)
    def _():
        for s in range(lc):
            for t in range(lc):
                op_ref[s * 128:(s + 1) * 128, t * 128:(t + 1) * 128] = lag_ref[0, 0, t - s + lc - 1]

    lhs = jnp.concatenate([u_ref[0, pl.ds(s, nchunk, stride=lc), :] for s in range(lc)], axis=1).astype(BF16)
    ir_ref[...] = dot(lhs, br_ref[0, 0])
    ii_ref[...] = dot(lhs, bi_ref[0, 0])
    ar = ar_ref[0, 0]
    ai = ai_ref[0, 0]

    def step(i, carry):
        sr, si = carry
        rev = jnp.where(i < nchunk_ctx, nchunk_ctx - 1 - i, nchunk + nchunk_ctx - 1 - i)
        ci = jnp.where(d == 0, i, rev)
        row = pl.ds(ci, 1)
        xr_ref[row, :] = sr
        xi_ref[row, :] = si
        nr = ar * sr - ai * si + ir_ref[row, :]
        ni = ar * si + ai * sr + ii_ref[row, :]
        return nr, ni

    z = jnp.zeros((1, ir_ref.shape[1]), F32)
    lax.fori_loop(0, nchunk, step, (z, z))
    y = dot(lhs, op_ref[...]) + dot(xr_ref[...].astype(BF16), cr_ref[0, 0]) \
        + dot(xi_ref[...].astype(BF16), ci_ref[0, 0])
    for t in range(lc):
        o_ref[0, 0, pl.ds(t, nchunk, stride=lc), :] = y[:, t * 128:(t + 1) * 128]


def _s5_mixer(p, ops, n_ctx):
    nb, t, _ = p.shape
    lc = S5_CHUNK
    nchunk = t // lc
    nlg = S5_LANE_GROUPS
    ns = ops["a_r"].shape[-1]

    def wspec(x):
        nd = x.ndim - 2
        return pl.BlockSpec((1, 1) + x.shape[2:], lambda g, d, b: (d, g) + (0,) * nd)

    keys = ("lag", "bst_r", "bst_i", "co_r", "co_i", "a_r", "a_i")
    nop = lc * 128
    kern = functools.partial(_s5_kernel, nchunk=nchunk, nchunk_ctx=n_ctx // lc)
    return pl.pallas_call(
        kern,
        out_shape=jax.ShapeDtypeStruct((2, nb, t, D_MODEL), F32),
        grid=(nlg, 2, nb),
        in_specs=[pl.BlockSpec((1, t, 128), lambda g, d, b: (b, 0, OFF_U // 128 + g))]
        + [wspec(ops[k]) for k in keys],
        out_specs=pl.BlockSpec((1, 1, t, 128), lambda g, d, b: (d, b, 0, g)),
        scratch_shapes=[pltpu.VMEM((nchunk, ns), F32)] * 4 + [pltpu.VMEM((nop, nop), BF16)],
        compiler_params=_cparams(("arbitrary", "arbitrary", "arbitrary")),
        name="s5_mixer",
    )(p, *[ops[k] for k in keys])


def _merge_kernel(ygla_ref, yrw_ref, ys5f_ref, ys5b_ref, u_ref, ga_ref, gb_ref, gc_ref, x_ref, mod_ref, wb_ref, wo_ref,
                  d_ref, gw_ref, gbias_ref, o_ref, *, tm, n_ctx, nb):
    b = pl.program_id(0)
    i = pl.program_id(1)
    dot = functools.partial(jnp.dot, preferred_element_type=F32)
    z = jax.nn.gelu(ys5f_ref[0, 0] + ys5b_ref[0, 0] + d_ref[...] * u_ref[0])
    zs = z * jax.nn.sigmoid(dot(z.astype(BF16), gw_ref[...]) + gbias_ref[...])
    m = jax.nn.sigmoid(ga_ref[0]) * dot(ygla_ref[0].astype(BF16), wb_ref[0]) \
        + jax.nn.sigmoid(gb_ref[0]) * dot(yrw_ref[0].astype(BF16), wb_ref[1]) \
        + jax.nn.sigmoid(gc_ref[0]) * dot(zs.astype(BF16), wb_ref[2])
    mix = dot(m.astype(BF16), wo_ref[...])
    row = i * tm + lax.broadcasted_iota(jnp.int32, (tm, 1), 0)
    gate = _mod_rows(mod_ref, b, nb, 2 * D_MODEL, row < n_ctx)
    o_ref[0] = x_ref[0] + gate * mix


def _merge(ygla, yrw, ys5, p, resid, mod, w_branch, w_out, s5_d, glu_w, glu_b, n_ctx):
    nb, t, d = resid.shape
    tm = _row_tile(t, 272)
    kern = functools.partial(_merge_kernel, tm=tm, n_ctx=n_ctx, nb=nb)

    def tok():
        return pl.BlockSpec((1, tm, d), lambda b, i: (b, i, 0))

    return pl.pallas_call(
        kern,
        out_shape=jax.ShapeDtypeStruct((nb, t, d), F32),
        grid=(nb, t // tm),
        in_specs=[
            tok(), tok(),
            pl.BlockSpec((1, 1, tm, d), lambda b, i: (0, b, i, 0)),
            pl.BlockSpec((1, 1, tm, d), lambda b, i: (1, b, i, 0)),
            pl.BlockSpec((1, tm, d), lambda b, i: (b, i, OFF_U // d)),
            pl.BlockSpec((1, tm, d), lambda b, i: (b, i, OFF_GATES // d)),
            pl.BlockSpec((1, tm, d), lambda b, i: (b, i, OFF_GATES // d + 1)),
            pl.BlockSpec((1, tm, d), lambda b, i: (b, i, OFF_GATES // d + 2)),
            tok(),
            pl.BlockSpec(mod.shape, lambda b, i: (0, 0)),
            pl.BlockSpec((3, d, d), lambda b, i: (0, 0, 0)),
            pl.BlockSpec((d, d), lambda b, i: (0, 0)),
            pl.BlockSpec((1, d), lambda b, i: (0, 0)),
            pl.BlockSpec((d, d), lambda b, i: (0, 0)),
            pl.BlockSpec((1, d), lambda b, i: (0, 0)),
        ],
        out_specs=tok(),
        compiler_params=_cparams(("arbitrary", "arbitrary")),
        name="merge",
    )(ygla, yrw, ys5, ys5, p, p, p, p, resid, mod, w_branch.astype(BF16), w_out.astype(BF16), s5_d.reshape(1, d),
      glu_w.astype(BF16), glu_b.reshape(1, d))


def _router_kernel(x_ref, nw_ref, mod_ref, rw_ref, rb_ref, h_ref, comb_ref, *, tm, n_ctx, nb):
    b = pl.program_id(0)
    i = pl.program_id(1)
    x = x_ref[0]
    xn = x * lax.rsqrt(jnp.mean(x * x, axis=-1, keepdims=True) + NORM_EPS) * nw_ref[...]
    row = i * tm + lax.broadcasted_iota(jnp.int32, (tm, 1), 0)
    is_ctx = row < n_ctx
    sh = _mod_rows(mod_ref, b, nb, 3 * D_MODEL, is_ctx)
    sc = _mod_rows(mod_ref, b, nb, 4 * D_MODEL, is_ctx)
    h = xn * (1.0 + sc) + sh
    h_ref[0] = h.astype(BF16)
    logits = jnp.dot(h, rw_ref[...], precision=HI, preferred_element_type=F32) + rb_ref[...]
    lane = lax.broadcasted_iota(jnp.int32, logits.shape, 1).astype(F32)
    neg = jnp.float32(-jnp.inf)
    cur = logits
    comb = jnp.zeros_like(logits)
    denom = jnp.zeros((tm, 1), F32)
    top = None
    for kk in range(TOP_K):
        m = jnp.max(cur, axis=-1, keepdims=True)
        idx = jnp.min(jnp.where(cur == m, lane, 128.0), axis=-1, keepdims=True)
        sel = lane == idx
        if kk == 0:
            top = m
        e = jnp.exp(m - top)
        comb = comb + jnp.where(sel, e, 0.0)
        denom = denom + e
        cur = jnp.where(sel, neg, cur)
    comb_ref[0] = comb / denom


def _router(resid, norm_w, mod, router_w, router_b, n_ctx):
    nb, t, d = resid.shape
    tm = _row_tile(t, 544)
    rw = jnp.pad(router_w, ((0, 0), (0, 128 - N_EXPERTS)))
    rb = jnp.pad(router_b, (0, 128 - N_EXPERTS), constant_values=-1e30).reshape(1, 128)
    kern = functools.partial(_router_kernel, tm=tm, n_ctx=n_ctx, nb=nb)
    return pl.pallas_call(
        kern,
        out_shape=(jax.ShapeDtypeStruct((nb, t, d), BF16), jax.ShapeDtypeStruct((nb, t, 128), F32)),
        grid=(nb, t // tm),
        in_specs=[
            pl.BlockSpec((1, tm, d), lambda b, i: (b, i, 0)),
            pl.BlockSpec((1, d), lambda b, i: (0, 0)),
            pl.BlockSpec(mod.shape, lambda b, i: (0, 0)),
            pl.BlockSpec((d, 128), lambda b, i: (0, 0)),
            pl.BlockSpec((1, 128), lambda b, i: (0, 0)),
        ],
        out_specs=(pl.BlockSpec((1, tm, d), lambda b, i: (b, i, 0)),
                   pl.BlockSpec((1, tm, 128), lambda b, i: (b, i, 0))),
        compiler_params=_cparams(("arbitrary", "arbitrary")),
        name="router",
    )(resid, norm_w.reshape(1, d), mod, rw, rb)


def _moe_kernel(h_ref, comb_ref, x_ref, mod_ref, wg_ref, bg_ref, wu_ref, bu_ref, wd_ref, bd_ref, o_ref,
                acc_ref, *, tm, n_ctx, nb):
    b = pl.program_id(0)
    i = pl.program_id(1)
    e = pl.program_id(2)
    dot = functools.partial(jnp.dot, preferred_element_type=F32)

    @pl.when(e == 0)
    def _():
        acc_ref[...] = jnp.zeros_like(acc_ref)

    h = h_ref[0]
    gl = jnp.minimum(dot(h, wg_ref[0]) + bg_ref[0], SWIGLU_LIMIT)
    li = jnp.clip(dot(h, wu_ref[0]) + bu_ref[0], -SWIGLU_LIMIT, SWIGLU_LIMIT)
    act = gl * jax.nn.sigmoid(SWIGLU_ALPHA * gl) * (li + 1.0)
    y = dot(act.astype(BF16), wd_ref[0]) + bd_ref[0]
    comb = comb_ref[0]
    lane = lax.broadcasted_iota(jnp.int32, comb.shape, 1)
    ce = jnp.sum(jnp.where(lane == e, comb, 0.0), axis=-1, keepdims=True)
    acc_ref[...] += ce * y

    @pl.when(e == N_EXPERTS - 1)
    def _():
        row = i * tm + lax.broadcasted_iota(jnp.int32, (tm, 1), 0)
        gate = _mod_rows(mod_ref, b, nb, 5 * D_MODEL, row < n_ctx)
        o_ref[0] = x_ref[0] + gate * acc_ref[...]


def _moe(h, comb, resid, mod, wg, bg, wu, bu, wd, bd, n_ctx):
    nb, t, d = resid.shape
    tm = _row_tile(t, 1088)
    kern = functools.partial(_moe_kernel, tm=tm, n_ctx=n_ctx, nb=nb)

    def wspec():
        return pl.BlockSpec((1, d, d), lambda b, i, e: (e, 0, 0))

    def bspec():
        return pl.BlockSpec((1, 1, d), lambda b, i, e: (e, 0, 0))

    return pl.pallas_call(
        kern,
        out_shape=jax.ShapeDtypeStruct((nb, t, d), F32),
        grid=(nb, t // tm, N_EXPERTS),
        in_specs=[
            pl.BlockSpec((1, tm, d), lambda b, i, e: (b, i, 0)),
            pl.BlockSpec((1, tm, 128), lambda b, i, e: (b, i, 0)),
            pl.BlockSpec((1, tm, d), lambda b, i, e: (b, i, 0)),
            pl.BlockSpec(mod.shape, lambda b, i, e: (0, 0)),
            wspec(), bspec(), wspec(), bspec(), wspec(), bspec(),
        ],
        out_specs=pl.BlockSpec((1, tm, d), lambda b, i, e: (b, i, 0)),
        scratch_shapes=[pltpu.VMEM((tm, d), F32)],
        compiler_params=_cparams(("arbitrary", "arbitrary", "arbitrary")),
        name="moe_experts",
    )(h, comb, resid, mod, wg.astype(BF16), bg.reshape(N_EXPERTS, 1, d), wu.astype(BF16),
      bu.reshape(N_EXPERTS, 1, d), wd.astype(BF16), bd.reshape(N_EXPERTS, 1, d))


MOE_TILES = 8
MOE_FFN_ROWS = 160


def _moe_routed_kernel(off_ref, h_ref, comb_ref, combt_ref, rank_ref, rankt_ref, wg_ref, bg_ref, wu_ref, bu_ref,
                       wd_ref, bd_ref, o_ref, g_ref, y_ref, *, qt):
    q = pl.program_id(0)
    e = pl.program_id(1)
    ne = pl.num_programs(1)
    tile = qt // MOE_TILES
    win = tile + 8
    fb = MOE_FFN_ROWS
    dot = functools.partial(jnp.dot, preferred_element_type=F32)

    @pl.when(e == 0)
    def _():
        o_ref[...] = jnp.zeros_like(o_ref)
        g_ref[...] = jnp.zeros_like(g_ref)
        y_ref[...] = jnp.zeros_like(y_ref)

    base = (q * ne + e) * (MOE_TILES + 1)
    wrow = combt_ref[0, pl.ds(e, 1), :]
    rrow = rankt_ref[0, pl.ds(e, 1), :]
    sub_i = lax.broadcasted_iota(jnp.int32, (win, 1), 0)
    sub_f = lax.broadcasted_iota(jnp.int32, (win, tile), 0).astype(F32)
    lane_f = lax.broadcasted_iota(jnp.int32, (tile, win), 1).astype(F32)
    lane_e = lax.broadcasted_iota(jnp.int32, (tile, 128), 1) == e

    for j in range(MOE_TILES):
        off = off_ref[base + j]
        nxt = off_ref[base + j + 1]
        ws = pl.multiple_of((off // 8) * 8, 8)
        rj = rrow[:, j * tile:(j + 1) * tile] - ws.astype(F32)
        wj = wrow[:, j * tile:(j + 1) * tile]
        onehot = jnp.where((sub_f == rj) & (wj > 0.0), 1.0, 0.0).astype(BF16)
        new = dot(onehot, h_ref[0, j * tile:(j + 1) * tile, :])
        slot = sub_i + ws
        hit = (slot >= off) & (slot < nxt)
        g_ref[pl.ds(ws, win), :] = jnp.where(hit, new, g_ref[pl.ds(ws, win), :])

    n = off_ref[base + MOE_TILES]

    def ffn(i, carry):
        rows = pl.ds(pl.multiple_of(i * fb, 8), fb)
        x = g_ref[rows, :].astype(BF16)
        gl = jnp.minimum(dot(x, wg_ref[0]) + bg_ref[0], SWIGLU_LIMIT)
        li = jnp.clip(dot(x, wu_ref[0]) + bu_ref[0], -SWIGLU_LIMIT, SWIGLU_LIMIT)
        act = gl * jax.nn.sigmoid(SWIGLU_ALPHA * gl) * (li + 1.0)
        y_ref[rows, :] = dot(act.astype(BF16), wd_ref[0]) + bd_ref[0]
        return carry

    lax.fori_loop(0, (n + fb - 1) // fb, ffn, 0)

    for j in range(MOE_TILES):
        off = off_ref[base + j]
        ws = pl.multiple_of((off // 8) * 8, 8)
        rows = slice(j * tile, (j + 1) * tile)
        wcol = jnp.sum(jnp.where(lane_e, comb_ref[0, rows, :], 0.0), axis=-1, keepdims=True)
        rcol = jnp.sum(jnp.where(lane_e, rank_ref[0, rows, :], 0.0), axis=-1, keepdims=True)
        onehot = jnp.where((lane_f == rcol - ws.astype(F32)) & (wcol > 0.0), 1.0, 0.0).astype(BF16)
        contrib = dot(onehot, y_ref[pl.ds(ws, win), :].astype(BF16))
        o_ref[0, rows, :] += wcol * contrib


def _moe_routed(h, comb, wg, bg, wu, bu, wd, bd):
    nb, t, d = h.shape
    qt = _row_tile(t, 1088)
    assert qt % (8 * MOE_TILES) == 0
    nq = nb * (t // qt)
    tile = qt // MOE_TILES
    ne = N_EXPERTS
    combq = comb.reshape(nq, qt, 128)
    sel = (combq > 0.0).astype(jnp.int32)
    incl = jnp.cumsum(sel, axis=1)
    rank = (incl - sel).astype(F32)
    tile_off = rank[:, ::tile, :ne].astype(jnp.int32)
    total = incl[:, -1:, :ne]
    off = jnp.concatenate([tile_off, total], axis=1).transpose(0, 2, 1).reshape(-1)
    combt = combq.transpose(0, 2, 1)
    rankt = rank.transpose(0, 2, 1)
    gr = qt + MOE_FFN_ROWS + 16
    kern = functools.partial(_moe_routed_kernel, qt=qt)

    def wspec():
        return pl.BlockSpec((1, d, d), lambda q, e, off: (e, 0, 0))

    def bspec():
        return pl.BlockSpec((1, 1, d), lambda q, e, off: (e, 0, 0))

    def tokspec(w):
        return pl.BlockSpec((1, qt, w), lambda q, e, off: (q, 0, 0))

    def tspec():
        return pl.BlockSpec((1, 128, qt), lambda q, e, off: (q, 0, 0))

    out = pl.pallas_call(
        kern,
        out_shape=jax.ShapeDtypeStruct((nq, qt, d), F32),
        grid_spec=pltpu.PrefetchScalarGridSpec(
            num_scalar_prefetch=1,
            grid=(nq, ne),
            in_specs=[tokspec(d), tokspec(128), tspec(), tokspec(128), tspec(),
                      wspec(), bspec(), wspec(), bspec(), wspec(), bspec()],
            out_specs=tokspec(d),
            scratch_shapes=[pltpu.VMEM((gr, d), F32), pltpu.VMEM((gr, d), F32)]),
        compiler_params=_cparams(("arbitrary", "arbitrary")),
        name="moe_routed",
    )(off, h.reshape(nq, qt, d), combq, combt, rank, rankt, wg.astype(BF16), bg.reshape(ne, 1, d),
      wu.astype(BF16), bu.reshape(ne, 1, d), wd.astype(BF16), bd.reshape(ne, 1, d))
    return out.reshape(nb, t, d)


def _resid_gate_kernel(x_ref, y_ref, mod_ref, o_ref, *, tm, n_ctx, nb, off):
    b = pl.program_id(0)
    i = pl.program_id(1)
    row = i * tm + lax.broadcasted_iota(jnp.int32, (tm, 1), 0)
    gate = _mod_rows(mod_ref, b, nb, off, row < n_ctx)
    o_ref[0] = x_ref[0] + gate * y_ref[0]


def _resid_gate(resid, y, mod, off, n_ctx):
    nb, t, d = resid.shape
    tm = _row_tile(t, 1088)
    kern = functools.partial(_resid_gate_kernel, tm=tm, n_ctx=n_ctx, nb=nb, off=off)

    def tok():
        return pl.BlockSpec((1, tm, d), lambda b, i: (b, i, 0))

    return pl.pallas_call(
        kern,
        out_shape=jax.ShapeDtypeStruct((nb, t, d), F32),
        grid=(nb, t // tm),
        in_specs=[tok(), tok(), pl.BlockSpec(mod.shape, lambda b, i: (0, 0))],
        out_specs=tok(),
        compiler_params=_cparams(("arbitrary", "arbitrary")),
        name="resid_gate",
    )(resid, y, mod)


def _final_norm_kernel(x_ref, w_ref, o_ref):
    x = x_ref[0]
    o_ref[0] = x * lax.rsqrt(jnp.mean(x * x, axis=-1, keepdims=True) + NORM_EPS) * w_ref[...]


def _final_norm(resid, w, n_ctx):
    nb, t, d = resid.shape
    tm = n_ctx
    return pl.pallas_call(
        _final_norm_kernel,
        out_shape=jax.ShapeDtypeStruct((nb, t - n_ctx, d), F32),
        grid=(nb, (t - n_ctx) // tm),
        in_specs=[pl.BlockSpec((1, tm, d), lambda b, i: (b, i + 1, 0)),
                  pl.BlockSpec((1, d), lambda b, i: (0, 0))],
        out_specs=pl.BlockSpec((1, tm, d), lambda b, i: (b, i, 0)),
        compiler_params=_cparams(("arbitrary", "arbitrary")),
        name="final_norm",
    )(resid, w.reshape(1, d))


def _pack_w_in(w):
    o_gg = CONV_CH + GLA_V
    o_rg = o_gg + 2 * GLA_GATE_RANK
    o_rw = o_rg + RWKV_G_RANK
    o_ra = o_rw + 2 * RWKV_W_RANK
    o_u = o_ra + 2 * RWKV_A_RANK
    o_gt = o_u + D_MODEL
    pad = jnp.zeros((w.shape[0], 128 - 2 * GLA_GATE_RANK), w.dtype)
    return jnp.concatenate([w[:, :o_gg], w[:, o_u:o_gt], w[:, o_gt:], w[:, o_gg:o_rg], pad,
                            w[:, o_rg:o_rw], w[:, o_rw:o_ra], w[:, o_ra:o_u]], axis=1).astype(BF16)


def kernel(x, c, ctx, c_ctx, w_ada, b_ada, norm_mix, w_in, conv_w, gla_wg2, gla_bg, gla_norm, rw_w0, rw_w2, rw_a0, rw_a2, rw_g2, rw_kk, rw_ka, rw_rk, rw_ln_g, rw_ln_b, s5_lam_re, s5_lam_im, s5_log_dt, s5_b_re, s5_b_im, s5_c_re, s5_c_im, s5_d, s5_glu_w, s5_glu_b, w_branch, w_out, norm_ffn, router_w, router_b, exp_w_gate, exp_b_gate, exp_w_up, exp_b_up, exp_w_down, exp_b_down, final_norm):
    depth = w_ada.shape[0]
    nb, _, d = x.shape
    n_ctx = ctx.shape[1]
    assert d == D_MODEL and n_ctx % RWKV_SEG == 0 and x.shape[1] % RWKV_SEG == 0 and nb < MOD_ROWS

    resid = jnp.concatenate([ctx, x], axis=1).astype(F32)
    cc = jnp.concatenate([c, c_ctx[None], jnp.zeros((MOD_ROWS - nb - 1, d), F32)], axis=0)
    mods = _ada_table(cc, w_ada, b_ada)

    for l in range(depth):
        mod = mods[l]
        p = _norm_in_proj(resid, norm_mix[l], mod, _pack_w_in(w_in[l]), n_ctx)
        pcv = _short_conv(p, conv_w[l], n_ctx)
        y_gla = _gla_mixer(pcv, p, gla_wg2[l], gla_bg[l], gla_norm[l], n_ctx)
        prm = dict(w0=rw_w0[l], w2=rw_w2[l], a0=rw_a0[l], a2=rw_a2[l], g2=rw_g2[l], kk=rw_kk[l], ka=rw_ka[l],
                   rk=rw_rk[l].reshape(-1), ln_g=rw_ln_g[l], ln_b=rw_ln_b[l])
        y_rw = _rwkv_mixer(pcv, p, prm, n_ctx)
        ops = _s5_operators(s5_lam_re[l], s5_lam_im[l], s5_log_dt[l], s5_b_re[l], s5_b_im[l],
                            s5_c_re[l], s5_c_im[l])
        y_s5 = _s5_mixer(p, ops, n_ctx)
        resid = _merge(y_gla, y_rw, y_s5, p, resid, mod, w_branch[l], w_out[l], s5_d[l], s5_glu_w[l],
                       s5_glu_b[l], n_ctx)
        h, comb = _router(resid, norm_ffn[l], mod, router_w[l], router_b[l], n_ctx)
        y_moe = _moe_routed(h, comb, exp_w_gate[l], exp_b_gate[l], exp_w_up[l], exp_b_up[l],
                            exp_w_down[l], exp_b_down[l])
        resid = _resid_gate(resid, y_moe, mod, 5 * D_MODEL, n_ctx)
    return _final_norm(resid, final_norm, n_ctx)
```

```python
import functools
import math

import jax
import jax.numpy as jnp
from jax import lax
from jax.experimental import pallas as pl
from jax.experimental.pallas import tpu as pltpu

F32 = jnp.float32
BF16 = jnp.bfloat16
HI = lax.Precision.HIGHEST

D_MODEL = 1024
GRID_W = 64

GLA_HEADS = 4
GLA_DK = 128
GLA_DV = 256
GLA_QK = GLA_HEADS * GLA_DK
GLA_V = GLA_HEADS * GLA_DV
GLA_GATE_RANK = 16
GLA_GATE_NORM = 16.0
CHUNK = 64
GLA_GROUP = 4

RWKV_HEAD = 64
RWKV_DIM = 1024
RWKV_W_RANK = 64
RWKV_A_RANK = 64
RWKV_G_RANK = 128
RWKV_W_SCALE = math.exp(-0.5)
RWKV_LN_EPS = 64e-5
RWKV_LANES = 256
RWKV_HPB = RWKV_LANES // RWKV_HEAD
RWKV_SEG = 256

S5_GROUP_CH = 16
S5_GROUPS = 64
S5_STATE = 64
S5_CHUNK = 16
S5_LANE_GROUPS = 8
MOD_ROWS = 8

N_EXPERTS = 32
TOP_K = 4
SWIGLU_LIMIT = 7.0
SWIGLU_ALPHA = 1.702
NORM_EPS = 1e-6

GLA_CONV = 2 * GLA_QK + GLA_V
CONV_CH = GLA_CONV + 3 * RWKV_DIM

OFF_PC = 0
OFF_OG = CONV_CH
OFF_U = OFF_OG + GLA_V
OFF_GATES = OFF_U + D_MODEL
OFF_GG = OFF_GATES + 3 * D_MODEL
OFF_RG = OFF_GG + 128
OFF_RW = OFF_RG + 128
OFF_RA = OFF_RW + 128
NP = OFF_RA + 128

VMEM_LIMIT = 56 * 1024 * 1024


def _cparams(sem):
    return pltpu.CompilerParams(dimension_semantics=sem, vmem_limit_bytes=VMEM_LIMIT)


def _row_tile(t, cap):
    best = 8
    for d in range(8, min(t, cap) + 1, 8):
        if t % d == 0:
            best = d
    return best


def _silu(x):
    return x * jax.nn.sigmoid(x)


def _log_sigmoid(z):
    return jnp.minimum(z, 0.0) - jnp.log(1.0 + jnp.exp(-jnp.abs(z)))


def _tri_matmul(keep, x):
    n = x.shape[1]
    h1 = x.astype(BF16)
    r1 = x - h1.astype(F32)
    h2 = r1.astype(BF16)
    h3 = (r1 - h2.astype(F32)).astype(BF16)
    y = jnp.dot(keep.astype(BF16), jnp.concatenate([h1, h2, h3], axis=1), preferred_element_type=F32)
    return y[:, 0:n] + y[:, n:2 * n] + y[:, 2 * n:3 * n]


def _row_to_col(row):
    return jnp.transpose(jnp.broadcast_to(row, (128, row.shape[1])))[:, 0:1]


def _mod_rows(mod_ref, b, nb, off, is_ctx):
    vb = mod_ref[pl.ds(b, 1), off:off + D_MODEL]
    vc = mod_ref[nb:nb + 1, off:off + D_MODEL]
    return jnp.where(is_ctx, vc, vb)


def _ada_kernel(c_ref, w_ref, b_ref, o_ref):
    a = _silu(c_ref[...])
    o_ref[0] = jnp.dot(a, w_ref[0], precision=HI, preferred_element_type=F32) + b_ref[0]


def _ada_table(cc, w_ada, b_ada):
    depth, d, n = w_ada.shape
    tn = 1536
    return pl.pallas_call(
        _ada_kernel,
        out_shape=jax.ShapeDtypeStruct((depth, cc.shape[0], n), F32),
        grid=(depth, n // tn),
        in_specs=[
            pl.BlockSpec((cc.shape[0], d), lambda l, j: (0, 0)),
            pl.BlockSpec((1, d, tn), lambda l, j: (l, 0, j)),
            pl.BlockSpec((1, 1, tn), lambda l, j: (l, 0, j)),
        ],
        out_specs=pl.BlockSpec((1, cc.shape[0], tn), lambda l, j: (l, 0, j)),
        compiler_params=_cparams(("arbitrary", "arbitrary")),
        name="ada_table",
    )(cc, w_ada, b_ada.reshape(depth, 1, n))


def _norm_mm_kernel(x_ref, nw_ref, mod_ref, w_ref, o_ref, h_ref, *, tm, n_ctx, nb):
    b = pl.program_id(0)
    i = pl.program_id(1)

    @pl.when(pl.program_id(2) == 0)
    def _():
        x = x_ref[0]
        xn = x * lax.rsqrt(jnp.mean(x * x, axis=-1, keepdims=True) + NORM_EPS) * nw_ref[...]
        row = i * tm + lax.broadcasted_iota(jnp.int32, (tm, 1), 0)
        is_ctx = row < n_ctx
        sh = _mod_rows(mod_ref, b, nb, 0, is_ctx)
        sc = _mod_rows(mod_ref, b, nb, D_MODEL, is_ctx)
        h_ref[...] = (xn * (1.0 + sc) + sh).astype(BF16)

    o_ref[0] = jnp.dot(h_ref[...], w_ref[...], preferred_element_type=F32)


def _norm_in_proj(resid, norm_w, mod, w_in_p, n_ctx):
    nb, t, d = resid.shape
    tm = _row_tile(t, 1088)
    tn = 768
    kern = functools.partial(_norm_mm_kernel, tm=tm, n_ctx=n_ctx, nb=nb)
    return pl.pallas_call(
        kern,
        out_shape=jax.ShapeDtypeStruct((nb, t, NP), F32),
        grid=(nb, t // tm, NP // tn),
        in_specs=[
            pl.BlockSpec((1, tm, d), lambda b, i, j: (b, i, 0)),
            pl.BlockSpec((1, d), lambda b, i, j: (0, 0)),
            pl.BlockSpec(mod.shape, lambda b, i, j: (0, 0)),
            pl.BlockSpec((d, tn), lambda b, i, j: (0, j)),
        ],
        out_specs=pl.BlockSpec((1, tm, tn), lambda b, i, j: (b, i, j)),
        scratch_shapes=[pltpu.VMEM((tm, d), BF16)],
        compiler_params=_cparams(("arbitrary", "arbitrary", "arbitrary")),
        name="norm_in_proj",
    )(resid, norm_w.reshape(1, d), mod, w_in_p)


def _conv_kernel(x_ref, w_ref, o_ref, xp_ref, xl_ref, xr_ref, *, n_ctx, rows):
    tc = x_ref.shape[-1]
    gw = GRID_W

    def wrow(k):
        return w_ref[k:k + 1, :]

    xc = x_ref[0, 0:n_ctx, :]
    ci = lax.broadcasted_iota(jnp.int32, (n_ctx, tc), 0)
    left = jnp.where(ci > 0, pltpu.roll(xc, 1, 0), 0.0)
    right = jnp.where(ci < n_ctx - 1, pltpu.roll(xc, n_ctx - 1, 0), 0.0)
    o_ref[0, 0:n_ctx, :] = left * wrow(3) + xc * wrow(4) + right * wrow(5)

    zeros = jnp.zeros((gw, tc), F32)
    for ref in (xp_ref, xl_ref, xr_ref):
        ref[0:gw, :] = zeros
        ref[(rows + 1) * gw:(rows + 2) * gw, :] = zeros
    wi = lax.broadcasted_iota(jnp.int32, (gw, tc), 0)

    def fill(r, carry):
        base = pl.multiple_of(r * gw, gw)
        xrow = x_ref[0, pl.ds(n_ctx + base, gw), :]
        xp_ref[pl.ds(gw + base, gw), :] = xrow
        xl_ref[pl.ds(gw + base, gw), :] = jnp.where(wi > 0, pltpu.roll(xrow, 1, 0), 0.0)
        xr_ref[pl.ds(gw + base, gw), :] = jnp.where(wi < gw - 1, pltpu.roll(xrow, gw - 1, 0), 0.0)
        return carry

    lax.fori_loop(0, rows, fill, 0)

    def comp(r, carry):
        base = pl.multiple_of(r * gw, gw)
        acc = jnp.zeros((gw, tc), F32)
        for dr in range(3):
            sl = pl.ds(base + gw * dr, gw)
            acc = acc + xl_ref[sl, :] * wrow(3 * dr) + xp_ref[sl, :] * wrow(3 * dr + 1) \
                + xr_ref[sl, :] * wrow(3 * dr + 2)
        o_ref[0, pl.ds(n_ctx + base, gw), :] = acc
        return carry

    lax.fori_loop(0, rows, comp, 0)


def _short_conv(p, conv_w, n_ctx):
    nb, t, _ = p.shape
    rows = (t - n_ctx) // GRID_W
    tc = 256
    kern = functools.partial(_conv_kernel, n_ctx=n_ctx, rows=rows)
    pad_rows = (rows + 2) * GRID_W
    return pl.pallas_call(
        kern,
        out_shape=jax.ShapeDtypeStruct((nb, t, CONV_CH), F32),
        grid=(nb, CONV_CH // tc),
        in_specs=[
            pl.BlockSpec((1, t, tc), lambda b, j: (b, 0, j)),
            pl.BlockSpec((9, tc), lambda b, j: (0, j)),
        ],
        out_specs=pl.BlockSpec((1, t, tc), lambda b, j: (b, 0, j)),
        scratch_shapes=[pltpu.VMEM((pad_rows, tc), F32)] * 3,
        compiler_params=_cparams(("arbitrary", "arbitrary")),
        name="short_conv",
    )(p, conv_w.reshape(9, CONV_CH))


def _gla_kernel(q_ref, k_ref, v_ref, gg_ref, og_ref, wg_ref, bg_ref, ng_ref, o_ref, ob_ref, *, n_ctx, t):
    c = CHUNK
    nch = t // c
    nctx = n_ctx // c
    ii = lax.broadcasted_iota(jnp.int32, (c, c), 0)
    jj = lax.broadcasted_iota(jnp.int32, (c, c), 1)
    scale = GLA_DK ** -0.5

    nt = (((1,), (1,)), ((), ()))
    tn = (((0,), (0,)), ((), ()))
    dot = functools.partial(jnp.dot, preferred_element_type=F32)
    grp = GLA_GROUP
    assert nctx % grp == 0 and nch % grp == 0

    def group(g, carry):
        chains = []
        for d in range(2):
            for u in range(grp):
                i = g * grp + u
                ci = i if d == 0 else jnp.where(i < nctx, nctx - 1 - i, nch + nctx - 1 - i)
                chains.append((d, pl.ds(pl.multiple_of(ci * c, c), c)))
        keeps = [(ii >= jj) if d == 0 else (ii <= jj) for d, _ in chains]
        q = [_silu(q_ref[0, r, :]) * scale for _, r in chains]
        k = [_silu(k_ref[0, r, :]) for _, r in chains]
        vb = [_silu(v_ref[0, r, :]).astype(BF16) for _, r in chains]
        z = [dot(gg_ref[0, r, :][:, GLA_GATE_RANK * d:GLA_GATE_RANK * (d + 1)], wg_ref[d]) + bg_ref[d]
             for d, r in chains]
        la = [_log_sigmoid(x) * (1.0 / GLA_GATE_NORM) for x in z]
        bcum = [_tri_matmul(kp, x) for kp, x in zip(keeps, la)]
        blast = [b[c - 1:c, :] if d == 0 else b[0:1, :] for (d, _), b in zip(chains, bcum)]
        q_dec = [(x * jnp.exp(b)).astype(BF16) for x, b in zip(q, bcum)]
        k_inv = [(x * jnp.exp(-b)).astype(BF16) for x, b in zip(k, bcum)]
        k_end = [(x * jnp.exp(bl - b)).astype(BF16) for x, b, bl in zip(k, bcum, blast)]
        sc = [lax.dot_general(a, b, nt, preferred_element_type=F32) for a, b in zip(q_dec, k_inv)]
        sc = [jnp.where(kp, x, 0.0).astype(BF16) for kp, x in zip(keeps, sc)]
        intra = [dot(a, b) for a, b in zip(sc, vb)]
        kv = [lax.dot_general(a, b, tn, preferred_element_type=F32) for a, b in zip(k_end, vb)]
        dec = [_row_to_col(jnp.exp(bl)) for bl in blast]
        states = list(carry)
        for n, (d, r) in enumerate(chains):
            s = states[d]
            o = intra[n] + dot(q_dec[n], s.astype(BF16))
            if d == 0:
                o_ref[0, r, :] = o
            else:
                ob_ref[r, :] = o
            states[d] = dec[n] * s + kv[n]
        return tuple(states)

    s0 = jnp.zeros((GLA_DK, GLA_DV), F32)
    lax.fori_loop(0, nch // grp, group, (s0, s0))

    def finish(ci, carry):
        rows = pl.ds(pl.multiple_of(ci * c, c), c)
        o = o_ref[0, rows, :] + ob_ref[rows, :]
        o = o * lax.rsqrt(jnp.mean(o * o, axis=-1, keepdims=True) + 1e-5) * ng_ref[...]
        o_ref[0, rows, :] = o * _silu(og_ref[0, rows, :])
        return carry

    lax.fori_loop(0, nch, finish, 0, unroll=2)


def _gla_mixer(pcv, p, wg2, bg, norm_g, n_ctx):
    nb, t, _ = pcv.shape
    kern = functools.partial(_gla_kernel, n_ctx=n_ctx, t=t)
    return pl.pallas_call(
        kern,
        out_shape=jax.ShapeDtypeStruct((nb, t, GLA_V), F32),
        grid=(nb, GLA_HEADS),
        in_specs=[
            pl.BlockSpec((1, t, GLA_DK), lambda b, h: (b, 0, h)),
            pl.BlockSpec((1, t, GLA_DK), lambda b, h: (b, 0, GLA_HEADS + h)),
            pl.BlockSpec((1, t, GLA_DV), lambda b, h: (b, 0, GLA_HEADS + h)),
            pl.BlockSpec((1, t, 128), lambda b, h: (b, 0, OFF_GG // 128)),
            pl.BlockSpec((1, t, GLA_DV), lambda b, h: (b, 0, OFF_OG // GLA_DV + h)),
            pl.BlockSpec((2, GLA_GATE_RANK, GLA_DK), lambda b, h: (0, 0, h)),
            pl.BlockSpec((2, 1, GLA_DK), lambda b, h: (0, 0, h)),
            pl.BlockSpec((1, GLA_DV), lambda b, h: (0, 0)),
        ],
        out_specs=pl.BlockSpec((1, t, GLA_DV), lambda b, h: (b, 0, h)),
        scratch_shapes=[pltpu.VMEM((t, GLA_DV), F32)],
        compiler_params=_cparams(("arbitrary", "arbitrary")),
        name="gla_mixer",
    )(pcv, pcv, pcv, p, p, wg2, bg.reshape(2, 1, GLA_QK), norm_g.reshape(1, GLA_DV))


def _bd_mask():
    shift = RWKV_HEAD.bit_length() - 1
    r = lax.broadcasted_iota(jnp.int32, (RWKV_LANES, RWKV_LANES), 0) >> shift
    c = lax.broadcasted_iota(jnp.int32, (RWKV_LANES, RWKV_LANES), 1) >> shift
    return r == c


def _bd(x, mask):
    xb = x.astype(BF16)
    return jnp.where(mask, jnp.concatenate([xb] * RWKV_HPB, axis=0), jnp.zeros((), BF16))


def _head_sum(x, ones_bd):
    hi = x.astype(BF16)
    lo = (x - hi.astype(F32)).astype(BF16)
    return jnp.dot(hi, ones_bd, preferred_element_type=F32) + jnp.dot(lo, ones_bd, preferred_element_type=F32)


def _rwkv_group_kernel(rf_ref, kf_ref, vf_ref, rwf_ref, raf_ref, rb_ref, kb_ref, vb_ref, rwb_ref, rab_ref,
                       w0_ref, w2_ref, a0_ref, a2_ref, kkp_ref, kap_ref, yf_ref, yb_ref, stf_ref, stb_ref):
    c = CHUNK
    nchunk = RWKV_SEG // c
    mask_bd = _bd_mask()
    ones_bd = mask_bd.astype(BF16)
    dot = functools.partial(jnp.dot, preferred_element_type=F32)
    nt = (((1,), (1,)), ((), ()))
    tn = (((0,), (0,)), ((), ()))

    @pl.when(pl.program_id(2) == 0)
    def _():
        stf_ref[...] = jnp.zeros_like(stf_ref)
        stb_ref[...] = jnp.zeros_like(stb_ref)

    chains = [(0, slice(u * c, (u + 1) * c)) for u in range(nchunk)] \
        + [(1, slice(u * c, (u + 1) * c)) for u in range(nchunk - 1, -1, -1)]
    refs = ((rf_ref, kf_ref, vf_ref, rwf_ref, raf_ref), (rb_ref, kb_ref, vb_ref, rwb_ref, rab_ref))

    def each(f, *lists):
        return [f(*xs) for xs in zip(*lists)]

    ds = [d for d, _ in chains]
    r = [refs[d][0][0, rows, :] for d, rows in chains]
    k = [refs[d][1][0, rows, :] for d, rows in chains]
    v = [refs[d][2][0, rows, :] for d, rows in chains]
    pw = [refs[d][3][0, rows, :][:, RWKV_W_RANK * d:RWKV_W_RANK * (d + 1)] for d, rows in chains]
    pa = [refs[d][4][0, rows, :][:, RWKV_A_RANK * d:RWKV_A_RANK * (d + 1)] for d, rows in chains]
    lw = each(lambda d, x: w0_ref[d] + dot(jnp.tanh(x), w2_ref[d]), ds, pw)
    logw = [-RWKV_W_SCALE * jax.nn.sigmoid(x) for x in lw]
    la = each(lambda d, x: a0_ref[d] + dot(x, a2_ref[d]), ds, pa)
    a = [jax.nn.sigmoid(x) for x in la]
    kkv = [x * kkp_ref[...] for x in k]
    ss = [_head_sum(x * x, ones_bd) for x in kkv]
    kk = each(lambda x, s: x / jnp.maximum(jnp.sqrt(s), 1e-12), kkv, ss)
    kd = each(lambda x, y: x * (1.0 + (y - 1.0) * kap_ref[...]), k, a)
    kka = each(lambda x, y: x * y, kk, a)

    ii = lax.broadcasted_iota(jnp.int32, (c, c), 0)
    jj = lax.broadcasted_iota(jnp.int32, (c, c), 1)
    ti = lax.broadcasted_iota(jnp.int32, (c, RWKV_LANES), 0)
    si = lax.broadcasted_iota(jnp.int32, (c, RWKV_LANES), 1) & (c - 1)
    keep_sq = [(ii >= jj) if d == 0 else (ii <= jj) for d in ds]
    incl = [(si <= ti) if d == 0 else (si >= ti) for d in ds]
    strict = [(si < ti) if d == 0 else (si > ti) for d in ds]
    eye = (si == ti).astype(F32)

    cb = each(_tri_matmul, keep_sq, logw)
    cblast = each(lambda d, x: x[c - 1:c, :] if d == 0 else x[0:1, :], ds, cb)
    ad = each(lambda x, b, w: -x * jnp.exp(b - w), kk, cb, logw)
    rd = each(lambda x, b: x * jnp.exp(b), r, cb)
    einv = [jnp.exp(-b) for b in cb]
    bi = each(lambda x, e: x * e, kka, einv)
    ki = each(lambda x, e: x * e, kd, einv)
    eend = each(lambda bl, b: jnp.exp(bl - b), cblast, cb)
    bend = each(lambda x, e: x * e, kka, eend)
    kend = each(lambda x, e: x * e, kd, eend)

    def bd(x):
        return _bd(x, mask_bd)

    def mm(x, y_bd):
        return dot(x.astype(BF16), y_bd)

    lhs = each(lambda x, y: jnp.concatenate([x, y], axis=0).astype(BF16), ad, rd)
    pb = each(lambda x, y: lax.dot_general(x, bd(y), nt, preferred_element_type=F32), lhs, bi)
    pk = each(lambda x, y: lax.dot_general(x, bd(y), nt, preferred_element_type=F32), lhs, ki)
    a_ab = each(lambda m, x: jnp.where(m, x[0:c], 0.0), strict, pb)
    r_ab = each(lambda m, x: jnp.where(m, x[c:2 * c], 0.0), incl, pb)
    a_ak = each(lambda m, x: jnp.where(m, x[0:c], 0.0), strict, pk)
    r_ak = each(lambda m, x: jnp.where(m, x[c:2 * c], 0.0), incl, pk)

    tinv = [eye + x for x in a_ab]
    pwr = each(lambda x: mm(x, bd(x)), a_ab)
    for lvl in range(5):
        if lvl < 4:
            both = each(lambda t_, p_: mm(jnp.concatenate([t_, p_], axis=0), bd(p_)), tinv, pwr)
            tinv = each(lambda t_, b: t_ + b[0:c], tinv, both)
            pwr = [b[c:2 * c] for b in both]
        else:
            tinv = each(lambda t_, p_: t_ + mm(t_, bd(p_)), tinv, pwr)

    ln = RWKV_LANES
    akv = each(lambda x, y, v_: mm(jnp.concatenate([x, y], axis=0), bd(v_)), a_ak, r_ak, v)
    mu = each(lambda t_, x, av: mm(t_, jnp.concatenate([bd(x), bd(av[0:c])], axis=1)), tinv, ad, akv)
    m1 = [x[:, 0:ln] for x in mu]
    u0 = [x[:, ln:2 * ln] for x in mu]
    ry = each(lambda rab, m, u_: mm(rab, jnp.concatenate([bd(m), bd(u_)], axis=1)), r_ab, m1, u0)
    r1 = each(lambda x, z: (x + z[:, 0:ln]).astype(BF16), rd, ry)
    y0 = each(lambda z, av: z[:, ln:2 * ln] + av[c:2 * c], ry, akv)
    g1 = each(lambda x, m: jnp.where(mask_bd, lax.dot_general(x.astype(BF16), m.astype(BF16), tn,
                                                              preferred_element_type=F32), 0.0).astype(BF16),
              bend, m1)
    h = each(lambda x, y, u_, v_: jnp.where(mask_bd, lax.dot_general(
        jnp.concatenate([x, y], axis=0).astype(BF16), jnp.concatenate([u_, v_], axis=0).astype(BF16), tn,
        preferred_element_type=F32), 0.0), bend, kend, u0, v)
    dec = [_row_to_col(jnp.exp(x)) for x in cblast]

    states = [stf_ref[...], stb_ref[...]]
    outs = (yf_ref, yb_ref)
    for n, (d, rows) in enumerate(chains):
        st = states[d]
        stb = st.astype(BF16)
        outs[d][0, rows, :] = dot(r1[n], stb) + y0[n]
        states[d] = dec[n] * st + dot(g1[n], stb) + h[n]
    stf_ref[...] = states[0]
    stb_ref[...] = states[1]


def _rwkv_post_kernel(r_ref, k_ref, v_ref, ra_ref, rg_ref, yf_ref, yb_ref, a0_ref, a2_ref, g2_ref,
                      kap_ref, rkp_ref, lng_ref, lnb_ref, o_ref):
    ones_bd = _bd_mask().astype(BF16)
    r = r_ref[0]
    k = k_ref[0]
    ra = ra_ref[0]
    y = yf_ref[0] + yb_ref[0]
    inv_n = 1.0 / RWKV_HEAD
    mu = _head_sum(y, ones_bd) * inv_n
    yc = y - mu
    var = _head_sum(yc * yc, ones_bd) * inv_n
    yn = yc * lax.rsqrt(var + RWKV_LN_EPS) * lng_ref[...] + lnb_ref[...]
    kds = jnp.zeros_like(k)
    for d in range(2):
        la = a0_ref[d] + jnp.dot(ra[:, RWKV_A_RANK * d:RWKV_A_RANK * (d + 1)], a2_ref[d],
                                 preferred_element_type=F32)
        kds = kds + k * (1.0 + (jax.nn.sigmoid(la) - 1.0) * kap_ref[...])
    bonus = _head_sum(r * kds * rkp_ref[...], ones_bd) * v_ref[0]
    g = jnp.dot(jax.nn.sigmoid(rg_ref[0]), g2_ref[...], preferred_element_type=F32)
    o_ref[0] = (yn + bonus) * g


def _rwkv_mixer(pcv, p, prm, n_ctx):
    nb, t, _ = pcv.shape
    seg = RWKV_SEG
    nseg = t // seg
    nseg_ctx = n_ctx // seg
    ln = RWKV_LANES
    nhb = RWKV_DIM // ln
    cb0 = GLA_CONV // ln

    def fmap(s):
        return s

    def bmap(s):
        return jnp.where(s < nseg_ctx, nseg_ctx - 1 - s, nseg - 1 - (s - nseg_ctx))

    def tok(off_blocks, smap):
        return pl.BlockSpec((1, seg, ln), lambda b, h, s: (b, smap(s), off_blocks + h))

    def small(off, smap):
        return pl.BlockSpec((1, seg, 128), lambda b, h, s: (b, smap(s), off // 128))

    def dirspecs(smap):
        return [tok(cb0, smap), tok(cb0 + nhb, smap), tok(cb0 + 2 * nhb, smap),
                small(OFF_RW, smap), small(OFF_RA, smap)]

    def vec():
        return pl.BlockSpec((1, ln), lambda b, h, s: (0, h))

    w0 = prm["w0"].reshape(2, 1, RWKV_DIM)
    a0 = prm["a0"].reshape(2, 1, RWKV_DIM)
    kkp = prm["kk"].reshape(1, RWKV_DIM)
    kap = prm["ka"].reshape(1, RWKV_DIM)
    yf, yb = pl.pallas_call(
        _rwkv_group_kernel,
        out_shape=(jax.ShapeDtypeStruct((nb, t, RWKV_DIM), F32),) * 2,
        grid=(nb, nhb, nseg),
        in_specs=dirspecs(fmap) + dirspecs(bmap) + [
            pl.BlockSpec((2, 1, ln), lambda b, h, s: (0, 0, h)),
            pl.BlockSpec((2, RWKV_W_RANK, ln), lambda b, h, s: (0, 0, h)),
            pl.BlockSpec((2, 1, ln), lambda b, h, s: (0, 0, h)),
            pl.BlockSpec((2, RWKV_A_RANK, ln), lambda b, h, s: (0, 0, h)),
            vec(), vec()],
        out_specs=(pl.BlockSpec((1, seg, ln), lambda b, h, s: (b, fmap(s), h)),
                   pl.BlockSpec((1, seg, ln), lambda b, h, s: (b, bmap(s), h))),
        scratch_shapes=[pltpu.VMEM((ln, ln), F32)] * 2,
        compiler_params=_cparams(("arbitrary", "arbitrary", "arbitrary")),
        name="rwkv7_scan",
    )(pcv, pcv, pcv, p, p, pcv, pcv, pcv, p, p, w0, prm["w2"], a0, prm["a2"], kkp, kap)

    tm = _row_tile(t, 1088)

    def ptok(off_blocks):
        return pl.BlockSpec((1, tm, ln), lambda b, i, h: (b, i, off_blocks + h))

    def psmall(off):
        return pl.BlockSpec((1, tm, 128), lambda b, i, h: (b, i, off // 128))

    def pvec():
        return pl.BlockSpec((1, ln), lambda b, i, h: (0, h))

    return pl.pallas_call(
        _rwkv_post_kernel,
        out_shape=jax.ShapeDtypeStruct((nb, t, RWKV_DIM), F32),
        grid=(nb, t // tm, nhb),
        in_specs=[ptok(cb0), ptok(cb0 + nhb), ptok(cb0 + 2 * nhb), psmall(OFF_RA), psmall(OFF_RG),
                  ptok(0), ptok(0),
                  pl.BlockSpec((2, 1, ln), lambda b, i, h: (0, 0, h)),
                  pl.BlockSpec((2, RWKV_A_RANK, ln), lambda b, i, h: (0, 0, h)),
                  pl.BlockSpec((RWKV_G_RANK, ln), lambda b, i, h: (0, h)),
                  pvec(), pvec(), pvec(), pvec()],
        out_specs=ptok(0),
        compiler_params=_cparams(("arbitrary", "arbitrary", "arbitrary")),
        name="rwkv7_post",
    )(pcv, pcv, pcv, p, p, yf, yb, a0, prm["a2"], prm["g2"], kap, prm["rk"].reshape(1, RWKV_DIM),
      prm["ln_g"].reshape(1, RWKV_DIM), prm["ln_b"].reshape(1, RWKV_DIM))


def _s5_operators(lam_re, lam_im, log_dt, b_re, b_im, c_re, c_im):
    g, p_, lc = S5_GROUPS, S5_STATE, S5_CHUNK
    lr, li = lam_re.astype(F32), lam_im.astype(F32)
    br, bi = b_re.astype(F32), b_im.astype(F32)
    den = lr * lr + li * li
    n = jnp.arange(lc + 1, dtype=F32)[:, None, None]
    ein = functools.partial(jnp.einsum, precision=HI)
    out = []
    for d in range(2):
        dt = jnp.exp(log_dt[d].astype(F32))[:, None]
        mag = jnp.exp(n * (lr * dt)[None])
        ang = n * (li * dt)[None]
        pr, pi = mag * jnp.cos(ang), mag * jnp.sin(ang)
        m1 = jnp.exp(lr * dt)
        ar, ai = m1 * jnp.cos(li * dt), m1 * jnp.sin(li * dt)
        fr = ((ar - 1.0) * lr + ai * li) / den
        fi = (ai * lr - (ar - 1.0) * li) / den
        bb_r = fr[..., None] * br - fi[..., None] * bi
        bb_i = fr[..., None] * bi + fi[..., None] * br
        cr, ci = c_re[d].astype(F32), c_im[d].astype(F32)
        ca_r = cr[None] * pr[:, :, None, :] - ci[None] * pi[:, :, None, :]
        ca_i = cr[None] * pi[:, :, None, :] + ci[None] * pr[:, :, None, :]
        kt = ein('tgop,gpc->tgoc', ca_r[:lc], bb_r) - ein('tgop,gpc->tgoc', ca_i[:lc], bb_i)
        nlg, gl, ch = S5_LANE_GROUPS, S5_GROUPS // S5_LANE_GROUPS, S5_GROUP_CH

        def block_diag(x, wr, wc):
            rg = jnp.arange(gl * wr)[:, None] // wr
            cg = jnp.arange(gl * wc)[None, :] // wc
            return jnp.where(rg == cg, jnp.tile(x, (1,) * (x.ndim - 1) + (gl,)), 0.0)

        ktc = kt.transpose(1, 0, 3, 2).reshape(nlg, gl, lc, ch, ch).transpose(0, 2, 1, 3, 4)
        blk = block_diag(ktc.reshape(nlg, lc, gl * ch, ch), ch, ch)
        zer = jnp.zeros((nlg, lc - 1, 128, 128), F32)
        lagblk = jnp.concatenate([zer, blk], axis=1) if d == 0 else jnp.concatenate([blk[:, ::-1], zer], axis=1)

        e_s = (lc - 1 - jnp.arange(lc)) if d == 0 else jnp.arange(lc)
        ps_r, ps_i = pr[e_s], pi[e_s]
        bst_r = ps_r[..., None] * bb_r[None] - ps_i[..., None] * bb_i[None]
        bst_i = ps_r[..., None] * bb_i[None] + ps_i[..., None] * bb_r[None]

        def rows_op(x):
            x = x.reshape(lc, nlg, gl, p_, ch).transpose(1, 0, 2, 4, 3)
            return block_diag(x.reshape(nlg, lc, gl * ch, p_), ch, p_).reshape(nlg, lc * gl * ch, gl * p_)

        f_t = (jnp.arange(lc) + 1) if d == 0 else (lc - jnp.arange(lc))

        def cols_op(x):
            x = x.reshape(lc, nlg, gl, ch, p_).transpose(1, 0, 2, 4, 3)
            return block_diag(x.reshape(nlg, lc, gl * p_, ch), p_, ch)

        out.append(dict(
            lag=lagblk.astype(BF16), bst_r=rows_op(bst_r).astype(BF16), bst_i=rows_op(bst_i).astype(BF16),
            co_r=cols_op(ca_r[f_t]).astype(BF16), co_i=cols_op(-ca_i[f_t]).astype(BF16),
            a_r=pr[lc].reshape(nlg, 1, gl * p_), a_i=pi[lc].reshape(nlg, 1, gl * p_)))
    return {k: jnp.stack([out[0][k], out[1][k]]) for k in out[0]}


def _s5_kernel(u_ref, lag_ref, br_ref, bi_ref, cr_ref, ci_ref, ar_ref, ai_ref, o_ref,
               ir_ref, ii_ref, xr_ref, xi_ref, op_ref, cor_ref, coi_ref, *, nchunk, nchunk_ctx):
    d = pl.program_id(1)
    lc = S5_CHUNK
    dot = functools.partial(jnp.dot, preferred_element_type=F32)

    @pl.when(pl.program_id(2) == 0)
    def _():
        for t in range(lc):
            cor_ref[:, t * 128:(t + 1) * 128] = cr_ref[0, 0, t]
            coi_ref[:, t * 128:(t + 1) * 128] = ci_ref[0, 0, t]
            for s in range(lc):
                op_ref[s * 128:(s + 1) * 128, t * 128:(t + 1) * 128] = lag_ref[0, 0, t - s + lc - 1]

    lhs = jnp.concatenate([u_ref[0, pl.ds(s, nchunk, stride=lc), :] for s in range(lc)], axis=1).astype(BF16)
    ir_ref[...] = dot(lhs, br_ref[0, 0])
    ii_ref[...] = dot(lhs, bi_ref[0, 0])
    ar = ar_ref[0, 0]
    ai = ai_ref[0, 0]

    def step(i, carry):
        sr, si = carry
        rev = jnp.where(i < nchunk_ctx, nchunk_ctx - 1 - i, nchunk + nchunk_ctx - 1 - i)
        ci = jnp.where(d == 0, i, rev)
        row = pl.ds(ci, 1)
        xr_ref[row, :] = sr
        xi_ref[row, :] = si
        nr = ar * sr - ai * si + ir_ref[row, :]
        ni = ar * si + ai * sr + ii_ref[row, :]
        return nr, ni

    z = jnp.zeros((1, ir_ref.shape[1]), F32)
    lax.fori_loop(0, nchunk, step, (z, z), unroll=8)
    y = dot(lhs, op_ref[...]) + dot(xr_ref[...].astype(BF16), cor_ref[...]) \
        + dot(xi_ref[...].astype(BF16), coi_ref[...])
    for t in range(lc):
        o_ref[0, 0, pl.ds(t, nchunk, stride=lc), :] = y[:, t * 128:(t + 1) * 128]


def _s5_mixer(p, ops, n_ctx):
    nb, t, _ = p.shape
    lc = S5_CHUNK
    nchunk = t // lc
    nlg = S5_LANE_GROUPS
    ns = ops["a_r"].shape[-1]

    def wspec(x):
        nd = x.ndim - 2
        return pl.BlockSpec((1, 1) + x.shape[2:], lambda g, d, b: (d, g) + (0,) * nd)

    keys = ("lag", "bst_r", "bst_i", "co_r", "co_i", "a_r", "a_i")
    nop = lc * 128
    kern = functools.partial(_s5_kernel, nchunk=nchunk, nchunk_ctx=n_ctx // lc)
    return pl.pallas_call(
        kern,
        out_shape=jax.ShapeDtypeStruct((2, nb, t, D_MODEL), F32),
        grid=(nlg, 2, nb),
        in_specs=[pl.BlockSpec((1, t, 128), lambda g, d, b: (b, 0, OFF_U // 128 + g))]
        + [wspec(ops[k]) for k in keys],
        out_specs=pl.BlockSpec((1, 1, t, 128), lambda g, d, b: (d, b, 0, g)),
        scratch_shapes=[pltpu.VMEM((nchunk, ns), F32)] * 4
        + [pltpu.VMEM((nop, nop), BF16), pltpu.VMEM((ns, nop), BF16), pltpu.VMEM((ns, nop), BF16)],
        compiler_params=_cparams(("arbitrary", "arbitrary", "arbitrary")),
        name="s5_mixer",
    )(p, *[ops[k] for k in keys])


def _merge_kernel(ygla_ref, yrw_ref, ys5f_ref, ys5b_ref, u_ref, ga_ref, gb_ref, gc_ref, x_ref, mod_ref, wb_ref, wo_ref,
                  d_ref, gw_ref, gbias_ref, o_ref, *, tm, n_ctx, nb):
    b = pl.program_id(0)
    i = pl.program_id(1)
    dot = functools.partial(jnp.dot, preferred_element_type=F32)
    z = jax.nn.gelu(ys5f_ref[0, 0] + ys5b_ref[0, 0] + d_ref[...] * u_ref[0])
    zs = z * jax.nn.sigmoid(dot(z.astype(BF16), gw_ref[...]) + gbias_ref[...])
    m = jax.nn.sigmoid(ga_ref[0]) * dot(ygla_ref[0].astype(BF16), wb_ref[0]) \
        + jax.nn.sigmoid(gb_ref[0]) * dot(yrw_ref[0].astype(BF16), wb_ref[1]) \
        + jax.nn.sigmoid(gc_ref[0]) * dot(zs.astype(BF16), wb_ref[2])
    mix = dot(m.astype(BF16), wo_ref[...])
    row = i * tm + lax.broadcasted_iota(jnp.int32, (tm, 1), 0)
    gate = _mod_rows(mod_ref, b, nb, 2 * D_MODEL, row < n_ctx)
    o_ref[0] = x_ref[0] + gate * mix


def _merge(ygla, yrw, ys5, p, resid, mod, w_branch, w_out, s5_d, glu_w, glu_b, n_ctx):
    nb, t, d = resid.shape
    tm = _row_tile(t, 272)
    kern = functools.partial(_merge_kernel, tm=tm, n_ctx=n_ctx, nb=nb)

    def tok():
        return pl.BlockSpec((1, tm, d), lambda b, i: (b, i, 0))

    return pl.pallas_call(
        kern,
        out_shape=jax.ShapeDtypeStruct((nb, t, d), F32),
        grid=(nb, t // tm),
        in_specs=[
            tok(), tok(),
            pl.BlockSpec((1, 1, tm, d), lambda b, i: (0, b, i, 0)),
            pl.BlockSpec((1, 1, tm, d), lambda b, i: (1, b, i, 0)),
            pl.BlockSpec((1, tm, d), lambda b, i: (b, i, OFF_U // d)),
            pl.BlockSpec((1, tm, d), lambda b, i: (b, i, OFF_GATES // d)),
            pl.BlockSpec((1, tm, d), lambda b, i: (b, i, OFF_GATES // d + 1)),
            pl.BlockSpec((1, tm, d), lambda b, i: (b, i, OFF_GATES // d + 2)),
            tok(),
            pl.BlockSpec(mod.shape, lambda b, i: (0, 0)),
            pl.BlockSpec((3, d, d), lambda b, i: (0, 0, 0)),
            pl.BlockSpec((d, d), lambda b, i: (0, 0)),
            pl.BlockSpec((1, d), lambda b, i: (0, 0)),
            pl.BlockSpec((d, d), lambda b, i: (0, 0)),
            pl.BlockSpec((1, d), lambda b, i: (0, 0)),
        ],
        out_specs=tok(),
        compiler_params=_cparams(("arbitrary", "arbitrary")),
        name="merge",
    )(ygla, yrw, ys5, ys5, p, p, p, p, resid, mod, w_branch.astype(BF16), w_out.astype(BF16), s5_d.reshape(1, d),
      glu_w.astype(BF16), glu_b.reshape(1, d))


def _router_kernel(x_ref, nw_ref, mod_ref, rw_ref, rb_ref, h_ref, comb_ref, *, tm, n_ctx, nb):
    b = pl.program_id(0)
    i = pl.program_id(1)
    x = x_ref[0]
    xn = x * lax.rsqrt(jnp.mean(x * x, axis=-1, keepdims=True) + NORM_EPS) * nw_ref[...]
    row = i * tm + lax.broadcasted_iota(jnp.int32, (tm, 1), 0)
    is_ctx = row < n_ctx
    sh = _mod_rows(mod_ref, b, nb, 3 * D_MODEL, is_ctx)
    sc = _mod_rows(mod_ref, b, nb, 4 * D_MODEL, is_ctx)
    h = xn * (1.0 + sc) + sh
    h_ref[0] = h.astype(BF16)
    logits = jnp.dot(h, rw_ref[...], precision=HI, preferred_element_type=F32) + rb_ref[...]
    lane = lax.broadcasted_iota(jnp.int32, logits.shape, 1).astype(F32)
    neg = jnp.float32(-jnp.inf)
    cur = logits
    comb = jnp.zeros_like(logits)
    denom = jnp.zeros((tm, 1), F32)
    top = None
    for kk in range(TOP_K):
        m = jnp.max(cur, axis=-1, keepdims=True)
        idx = jnp.min(jnp.where(cur == m, lane, 128.0), axis=-1, keepdims=True)
        sel = lane == idx
        if kk == 0:
            top = m
        e = jnp.exp(m - top)
        comb = comb + jnp.where(sel, e, 0.0)
        denom = denom + e
        cur = jnp.where(sel, neg, cur)
    comb_ref[0] = comb / denom


def _router(resid, norm_w, mod, router_w, router_b, n_ctx):
    nb, t, d = resid.shape
    tm = _row_tile(t, 544)
    rw = jnp.pad(router_w, ((0, 0), (0, 128 - N_EXPERTS)))
    rb = jnp.pad(router_b, (0, 128 - N_EXPERTS), constant_values=-1e30).reshape(1, 128)
    kern = functools.partial(_router_kernel, tm=tm, n_ctx=n_ctx, nb=nb)
    return pl.pallas_call(
        kern,
        out_shape=(jax.ShapeDtypeStruct((nb, t, d), BF16), jax.ShapeDtypeStruct((nb, t, 128), F32)),
        grid=(nb, t // tm),
        in_specs=[
            pl.BlockSpec((1, tm, d), lambda b, i: (b, i, 0)),
            pl.BlockSpec((1, d), lambda b, i: (0, 0)),
            pl.BlockSpec(mod.shape, lambda b, i: (0, 0)),
            pl.BlockSpec((d, 128), lambda b, i: (0, 0)),
            pl.BlockSpec((1, 128), lambda b, i: (0, 0)),
        ],
        out_specs=(pl.BlockSpec((1, tm, d), lambda b, i: (b, i, 0)),
                   pl.BlockSpec((1, tm, 128), lambda b, i: (b, i, 0))),
        compiler_params=_cparams(("arbitrary", "arbitrary")),
        name="router",
    )(resid, norm_w.reshape(1, d), mod, rw, rb)


MOE_TILES = 8
MOE_FFN_ROWS = 160


def _moe_routed_kernel(off_ref, h_ref, comb_ref, combt_ref, rank_ref, rankt_ref, wg_ref, bg_ref, wu_ref, bu_ref,
                       wd_ref, bd_ref, o_ref, g_ref, y_ref, *, qt):
    q = pl.program_id(0)
    e = pl.program_id(1)
    ne = pl.num_programs(1)
    tile = qt // MOE_TILES
    win = tile + 8
    fb = MOE_FFN_ROWS
    dot = functools.partial(jnp.dot, preferred_element_type=F32)

    @pl.when(e == 0)
    def _():
        o_ref[...] = jnp.zeros_like(o_ref)
        g_ref[...] = jnp.zeros_like(g_ref)
        y_ref[...] = jnp.zeros_like(y_ref)

    base = (q * ne + e) * (MOE_TILES + 1)
    wrow = combt_ref[0, pl.ds(e, 1), :]
    rrow = rankt_ref[0, pl.ds(e, 1), :]
    sub_i = lax.broadcasted_iota(jnp.int32, (win, 1), 0)
    sub_f = lax.broadcasted_iota(jnp.int32, (win, tile), 0).astype(F32)
    lane_f = lax.broadcasted_iota(jnp.int32, (tile, win), 1).astype(F32)
    lane_e = lax.broadcasted_iota(jnp.int32, (tile, 128), 1) == e

    for j in range(MOE_TILES):
        off = off_ref[base + j]
        nxt = off_ref[base + j + 1]
        ws = pl.multiple_of((off // 8) * 8, 8)
        rj = rrow[:, j * tile:(j + 1) * tile] - ws.astype(F32)
        wj = wrow[:, j * tile:(j + 1) * tile]
        onehot = jnp.where((sub_f == rj) & (wj > 0.0), 1.0, 0.0).astype(BF16)
        new = dot(onehot, h_ref[0, j * tile:(j + 1) * tile, :])
        slot = sub_i + ws
        hit = (slot >= off) & (slot < nxt)
        g_ref[pl.ds(ws, win), :] = jnp.where(hit, new, g_ref[pl.ds(ws, win), :])

    n = off_ref[base + MOE_TILES]

    def ffn(i, carry):
        rows = pl.ds(pl.multiple_of(i * fb, 8), fb)
        x = g_ref[rows, :].astype(BF16)
        gl = jnp.minimum(dot(x, wg_ref[0]) + bg_ref[0], SWIGLU_LIMIT)
        li = jnp.clip(dot(x, wu_ref[0]) + bu_ref[0], -SWIGLU_LIMIT, SWIGLU_LIMIT)
        act = gl * jax.nn.sigmoid(SWIGLU_ALPHA * gl) * (li + 1.0)
        y_ref[rows, :] = dot(act.astype(BF16), wd_ref[0]) + bd_ref[0]
        return carry

    lax.fori_loop(0, (n + fb - 1) // fb, ffn, 0)

    for j in range(MOE_TILES):
        off = off_ref[base + j]
        ws = pl.multiple_of((off // 8) * 8, 8)
        rows = slice(j * tile, (j + 1) * tile)
        wcol = jnp.sum(jnp.where(lane_e, comb_ref[0, rows, :], 0.0), axis=-1, keepdims=True)
        rcol = jnp.sum(jnp.where(lane_e, rank_ref[0, rows, :], 0.0), axis=-1, keepdims=True)
        onehot = jnp.where((lane_f == rcol - ws.astype(F32)) & (wcol > 0.0), 1.0, 0.0).astype(BF16)
        contrib = dot(onehot, y_ref[pl.ds(ws, win), :].astype(BF16))
        o_ref[0, rows, :] += wcol * contrib


def _moe_routed(h, comb, wg, bg, wu, bu, wd, bd):
    nb, t, d = h.shape
    qt = _row_tile(t, 1088)
    assert qt % (8 * MOE_TILES) == 0
    nq = nb * (t // qt)
    tile = qt // MOE_TILES
    ne = N_EXPERTS
    combq = comb.reshape(nq, qt, 128)
    sel = (combq > 0.0).astype(jnp.int32)
    incl = jnp.cumsum(sel, axis=1)
    rank = (incl - sel).astype(F32)
    tile_off = rank[:, ::tile, :ne].astype(jnp.int32)
    total = incl[:, -1:, :ne]
    off = jnp.concatenate([tile_off, total], axis=1).transpose(0, 2, 1).reshape(-1)
    combt = combq.transpose(0, 2, 1)
    rankt = rank.transpose(0, 2, 1)
    gr = qt + MOE_FFN_ROWS + 16
    kern = functools.partial(_moe_routed_kernel, qt=qt)

    def wspec():
        return pl.BlockSpec((1, d, d), lambda q, e, off: (e, 0, 0))

    def bspec():
        return pl.BlockSpec((1, 1, d), lambda q, e, off: (e, 0, 0))

    def tokspec(w):
        return pl.BlockSpec((1, qt, w), lambda q, e, off: (q, 0, 0))

    def tspec():
        return pl.BlockSpec((1, 128, qt), lambda q, e, off: (q, 0, 0))

    out = pl.pallas_call(
        kern,
        out_shape=jax.ShapeDtypeStruct((nq, qt, d), F32),
        grid_spec=pltpu.PrefetchScalarGridSpec(
            num_scalar_prefetch=1,
            grid=(nq, ne),
            in_specs=[tokspec(d), tokspec(128), tspec(), tokspec(128), tspec(),
                      wspec(), bspec(), wspec(), bspec(), wspec(), bspec()],
            out_specs=tokspec(d),
            scratch_shapes=[pltpu.VMEM((gr, d), F32), pltpu.VMEM((gr, d), F32)]),
        compiler_params=_cparams(("arbitrary", "arbitrary")),
        name="moe_routed",
    )(off, h.reshape(nq, qt, d), combq, combt, rank, rankt, wg.astype(BF16), bg.reshape(ne, 1, d),
      wu.astype(BF16), bu.reshape(ne, 1, d), wd.astype(BF16), bd.reshape(ne, 1, d))
    return out.reshape(nb, t, d)


def _resid_gate_kernel(x_ref, y_ref, mod_ref, o_ref, *, tm, n_ctx, nb, off):
    b = pl.program_id(0)
    i = pl.program_id(1)
    row = i * tm + lax.broadcasted_iota(jnp.int32, (tm, 1), 0)
    gate = _mod_rows(mod_ref, b, nb, off, row < n_ctx)
    o_ref[0] = x_ref[0] + gate * y_ref[0]


def _resid_gate(resid, y, mod, off, n_ctx):
    nb, t, d = resid.shape
    tm = _row_tile(t, 1088)
    kern = functools.partial(_resid_gate_kernel, tm=tm, n_ctx=n_ctx, nb=nb, off=off)

    def tok():
        return pl.BlockSpec((1, tm, d), lambda b, i: (b, i, 0))

    return pl.pallas_call(
        kern,
        out_shape=jax.ShapeDtypeStruct((nb, t, d), F32),
        grid=(nb, t // tm),
        in_specs=[tok(), tok(), pl.BlockSpec(mod.shape, lambda b, i: (0, 0))],
        out_specs=tok(),
        compiler_params=_cparams(("arbitrary", "arbitrary")),
        name="resid_gate",
    )(resid, y, mod)


def _final_norm_kernel(x_ref, w_ref, o_ref):
    x = x_ref[0]
    o_ref[0] = x * lax.rsqrt(jnp.mean(x * x, axis=-1, keepdims=True) + NORM_EPS) * w_ref[...]


def _final_norm(resid, w, n_ctx):
    nb, t, d = resid.shape
    tm = n_ctx
    return pl.pallas_call(
        _final_norm_kernel,
        out_shape=jax.ShapeDtypeStruct((nb, t - n_ctx, d), F32),
        grid=(nb, (t - n_ctx) // tm),
        in_specs=[pl.BlockSpec((1, tm, d), lambda b, i: (b, i + 1, 0)),
                  pl.BlockSpec((1, d), lambda b, i: (0, 0))],
        out_specs=pl.BlockSpec((1, tm, d), lambda b, i: (b, i, 0)),
        compiler_params=_cparams(("arbitrary", "arbitrary")),
        name="final_norm",
    )(resid, w.reshape(1, d))


def _pack_w_in(w):
    o_gg = CONV_CH + GLA_V
    o_rg = o_gg + 2 * GLA_GATE_RANK
    o_rw = o_rg + RWKV_G_RANK
    o_ra = o_rw + 2 * RWKV_W_RANK
    o_u = o_ra + 2 * RWKV_A_RANK
    o_gt = o_u + D_MODEL
    pad = jnp.zeros((w.shape[0], 128 - 2 * GLA_GATE_RANK), w.dtype)
    return jnp.concatenate([w[:, :o_gg], w[:, o_u:o_gt], w[:, o_gt:], w[:, o_gg:o_rg], pad,
                            w[:, o_rg:o_rw], w[:, o_rw:o_ra], w[:, o_ra:o_u]], axis=1).astype(BF16)


def kernel(x, c, ctx, c_ctx, w_ada, b_ada, norm_mix, w_in, conv_w, gla_wg2, gla_bg, gla_norm, rw_w0, rw_w2, rw_a0, rw_a2, rw_g2, rw_kk, rw_ka, rw_rk, rw_ln_g, rw_ln_b, s5_lam_re, s5_lam_im, s5_log_dt, s5_b_re, s5_b_im, s5_c_re, s5_c_im, s5_d, s5_glu_w, s5_glu_b, w_branch, w_out, norm_ffn, router_w, router_b, exp_w_gate, exp_b_gate, exp_w_up, exp_b_up, exp_w_down, exp_b_down, final_norm):
    depth = w_ada.shape[0]
    nb, _, d = x.shape
    n_ctx = ctx.shape[1]
    assert d == D_MODEL and n_ctx % RWKV_SEG == 0 and x.shape[1] % RWKV_SEG == 0 and nb < MOD_ROWS

    resid = jnp.concatenate([ctx, x], axis=1).astype(F32)
    cc = jnp.concatenate([c, c_ctx[None], jnp.zeros((MOD_ROWS - nb - 1, d), F32)], axis=0)
    mods = _ada_table(cc, w_ada, b_ada)

    for l in range(depth):
        mod = mods[l]
        p = _norm_in_proj(resid, norm_mix[l], mod, _pack_w_in(w_in[l]), n_ctx)
        pcv = _short_conv(p, conv_w[l], n_ctx)
        y_gla = _gla_mixer(pcv, p, gla_wg2[l], gla_bg[l], gla_norm[l], n_ctx)
        prm = dict(w0=rw_w0[l], w2=rw_w2[l], a0=rw_a0[l], a2=rw_a2[l], g2=rw_g2[l], kk=rw_kk[l], ka=rw_ka[l],
                   rk=rw_rk[l].reshape(-1), ln_g=rw_ln_g[l], ln_b=rw_ln_b[l])
        y_rw = _rwkv_mixer(pcv, p, prm, n_ctx)
        ops = _s5_operators(s5_lam_re[l], s5_lam_im[l], s5_log_dt[l], s5_b_re[l], s5_b_im[l],
                            s5_c_re[l], s5_c_im[l])
        y_s5 = _s5_mixer(p, ops, n_ctx)
        resid = _merge(y_gla, y_rw, y_s5, p, resid, mod, w_branch[l], w_out[l], s5_d[l], s5_glu_w[l],
                       s5_glu_b[l], n_ctx)
        h, comb = _router(resid, norm_ffn[l], mod, router_w[l], router_b[l], n_ctx)
        y_moe = _moe_routed(h, comb, exp_w_gate[l], exp_b_gate[l], exp_w_up[l], exp_b_up[l],
                            exp_w_down[l], exp_b_down[l])
        resid = _resid_gate(resid, y_moe, mod, 5 * D_MODEL, n_ctx)
    return _final_norm(resid, final_norm, n_ctx)
```

```python
import functools
import math

import jax
import jax.numpy as jnp
from jax import lax
from jax.experimental import pallas as pl
from jax.experimental.pallas import tpu as pltpu

F32 = jnp.float32
BF16 = jnp.bfloat16
HI = lax.Precision.HIGHEST

D_MODEL = 1024
GRID_W = 64

GLA_HEADS = 4
GLA_DK = 128
GLA_DV = 256
GLA_QK = GLA_HEADS * GLA_DK
GLA_V = GLA_HEADS * GLA_DV
GLA_GATE_RANK = 16
GLA_GATE_NORM = 16.0
CHUNK = 64
GLA_GROUP = 4

RWKV_HEAD = 64
RWKV_DIM = 1024
RWKV_W_RANK = 64
RWKV_A_RANK = 64
RWKV_G_RANK = 128
RWKV_W_SCALE = math.exp(-0.5)
RWKV_LN_EPS = 64e-5
RWKV_LANES = 256
RWKV_HPB = RWKV_LANES // RWKV_HEAD
RWKV_SEG = 256

S5_GROUP_CH = 16
S5_GROUPS = 64
S5_STATE = 64
S5_CHUNK = 16
S5_LANE_GROUPS = 8
MOD_ROWS = 8

N_EXPERTS = 32
TOP_K = 4
SWIGLU_LIMIT = 7.0
SWIGLU_ALPHA = 1.702
NORM_EPS = 1e-6

GLA_CONV = 2 * GLA_QK + GLA_V
CONV_CH = GLA_CONV + 3 * RWKV_DIM

OFF_PC = 0
OFF_OG = CONV_CH
OFF_U = OFF_OG + GLA_V
OFF_GATES = OFF_U + D_MODEL
OFF_GG = OFF_GATES + 3 * D_MODEL
OFF_RG = OFF_GG + 128
OFF_RW = OFF_RG + 128
OFF_RA = OFF_RW + 128
NP = OFF_RA + 128

VMEM_LIMIT = 56 * 1024 * 1024


def _cparams(sem):
    return pltpu.CompilerParams(dimension_semantics=sem, vmem_limit_bytes=VMEM_LIMIT)


def _row_tile(t, cap):
    best = 8
    for d in range(8, min(t, cap) + 1, 8):
        if t % d == 0:
            best = d
    return best


def _silu(x):
    return x * jax.nn.sigmoid(x)


def _log_sigmoid(z):
    return jnp.minimum(z, 0.0) - jnp.log(1.0 + jnp.exp(-jnp.abs(z)))


def _tri_matmul(keep, x):
    n = x.shape[1]
    h1 = x.astype(BF16)
    r1 = x - h1.astype(F32)
    h2 = r1.astype(BF16)
    h3 = (r1 - h2.astype(F32)).astype(BF16)
    y = jnp.dot(keep.astype(BF16), jnp.concatenate([h1, h2, h3], axis=1), preferred_element_type=F32)
    return y[:, 0:n] + y[:, n:2 * n] + y[:, 2 * n:3 * n]


def _row_to_col(row):
    return jnp.transpose(jnp.broadcast_to(row, (128, row.shape[1])))[:, 0:1]


def _mod_rows(mod_ref, b, nb, off, is_ctx):
    vb = mod_ref[pl.ds(b, 1), off:off + D_MODEL]
    vc = mod_ref[nb:nb + 1, off:off + D_MODEL]
    return jnp.where(is_ctx, vc, vb)


def _ada_kernel(c_ref, w_ref, b_ref, o_ref):
    a = _silu(c_ref[...])
    o_ref[0] = jnp.dot(a, w_ref[0], precision=HI, preferred_element_type=F32) + b_ref[0]


def _ada_table(cc, w_ada, b_ada):
    depth, d, n = w_ada.shape
    tn = 1536
    return pl.pallas_call(
        _ada_kernel,
        out_shape=jax.ShapeDtypeStruct((depth, cc.shape[0], n), F32),
        grid=(depth, n // tn),
        in_specs=[
            pl.BlockSpec((cc.shape[0], d), lambda l, j: (0, 0)),
            pl.BlockSpec((1, d, tn), lambda l, j: (l, 0, j)),
            pl.BlockSpec((1, 1, tn), lambda l, j: (l, 0, j)),
        ],
        out_specs=pl.BlockSpec((1, cc.shape[0], tn), lambda l, j: (l, 0, j)),
        compiler_params=_cparams(("arbitrary", "arbitrary")),
        name="ada_table",
    )(cc, w_ada, b_ada.reshape(depth, 1, n))


def _norm_mm_kernel(x_ref, nw_ref, mod_ref, w_ref, o_ref, h_ref, *, tm, n_ctx, nb):
    b = pl.program_id(0)
    i = pl.program_id(1)

    @pl.when(pl.program_id(2) == 0)
    def _():
        x = x_ref[0]
        xn = x * lax.rsqrt(jnp.mean(x * x, axis=-1, keepdims=True) + NORM_EPS) * nw_ref[...]
        row = i * tm + lax.broadcasted_iota(jnp.int32, (tm, 1), 0)
        is_ctx = row < n_ctx
        sh = _mod_rows(mod_ref, b, nb, 0, is_ctx)
        sc = _mod_rows(mod_ref, b, nb, D_MODEL, is_ctx)
        h_ref[...] = (xn * (1.0 + sc) + sh).astype(BF16)

    o_ref[0] = jnp.dot(h_ref[...], w_ref[...], preferred_element_type=F32)


def _norm_in_proj(resid, norm_w, mod, w_in_p, n_ctx):
    nb, t, d = resid.shape
    tm = _row_tile(t, 1088)
    tn = 1536
    kern = functools.partial(_norm_mm_kernel, tm=tm, n_ctx=n_ctx, nb=nb)
    return pl.pallas_call(
        kern,
        out_shape=jax.ShapeDtypeStruct((nb, t, NP), F32),
        grid=(nb, t // tm, NP // tn),
        in_specs=[
            pl.BlockSpec((1, tm, d), lambda b, i, j: (b, i, 0)),
            pl.BlockSpec((1, d), lambda b, i, j: (0, 0)),
            pl.BlockSpec(mod.shape, lambda b, i, j: (0, 0)),
            pl.BlockSpec((d, tn), lambda b, i, j: (0, j)),
        ],
        out_specs=pl.BlockSpec((1, tm, tn), lambda b, i, j: (b, i, j)),
        scratch_shapes=[pltpu.VMEM((tm, d), BF16)],
        compiler_params=_cparams(("arbitrary", "arbitrary", "arbitrary")),
        name="norm_in_proj",
    )(resid, norm_w.reshape(1, d), mod, w_in_p)


def _conv_kernel(x_ref, w_ref, o_ref, xp_ref, xl_ref, xr_ref, *, n_ctx, rows):
    tc = x_ref.shape[-1]
    gw = GRID_W

    def wrow(k):
        return w_ref[k:k + 1, :]

    xc = x_ref[0, 0:n_ctx, :]
    ci = lax.broadcasted_iota(jnp.int32, (n_ctx, tc), 0)
    left = jnp.where(ci > 0, pltpu.roll(xc, 1, 0), 0.0)
    right = jnp.where(ci < n_ctx - 1, pltpu.roll(xc, n_ctx - 1, 0), 0.0)
    o_ref[0, 0:n_ctx, :] = left * wrow(3) + xc * wrow(4) + right * wrow(5)

    zeros = jnp.zeros((gw, tc), F32)
    for ref in (xp_ref, xl_ref, xr_ref):
        ref[0:gw, :] = zeros
        ref[(rows + 1) * gw:(rows + 2) * gw, :] = zeros
    wi = lax.broadcasted_iota(jnp.int32, (gw, tc), 0)

    def fill(r, carry):
        base = pl.multiple_of(r * gw, gw)
        xrow = x_ref[0, pl.ds(n_ctx + base, gw), :]
        xp_ref[pl.ds(gw + base, gw), :] = xrow
        xl_ref[pl.ds(gw + base, gw), :] = jnp.where(wi > 0, pltpu.roll(xrow, 1, 0), 0.0)
        xr_ref[pl.ds(gw + base, gw), :] = jnp.where(wi < gw - 1, pltpu.roll(xrow, gw - 1, 0), 0.0)
        return carry

    lax.fori_loop(0, rows, fill, 0)

    def comp(r, carry):
        base = pl.multiple_of(r * gw, gw)
        acc = jnp.zeros((gw, tc), F32)
        for dr in range(3):
            sl = pl.ds(base + gw * dr, gw)
            acc = acc + xl_ref[sl, :] * wrow(3 * dr) + xp_ref[sl, :] * wrow(3 * dr + 1) \
                + xr_ref[sl, :] * wrow(3 * dr + 2)
        o_ref[0, pl.ds(n_ctx + base, gw), :] = acc
        return carry

    lax.fori_loop(0, rows, comp, 0)


def _short_conv(p, conv_w, n_ctx):
    nb, t, _ = p.shape
    rows = (t - n_ctx) // GRID_W
    tc = 256
    kern = functools.partial(_conv_kernel, n_ctx=n_ctx, rows=rows)
    pad_rows = (rows + 2) * GRID_W
    return pl.pallas_call(
        kern,
        out_shape=jax.ShapeDtypeStruct((nb, t, CONV_CH), F32),
        grid=(nb, CONV_CH // tc),
        in_specs=[
            pl.BlockSpec((1, t, tc), lambda b, j: (b, 0, j)),
            pl.BlockSpec((9, tc), lambda b, j: (0, j)),
        ],
        out_specs=pl.BlockSpec((1, t, tc), lambda b, j: (b, 0, j)),
        scratch_shapes=[pltpu.VMEM((pad_rows, tc), F32)] * 3,
        compiler_params=_cparams(("arbitrary", "arbitrary")),
        name="short_conv",
    )(p, conv_w.reshape(9, CONV_CH))


def _gla_kernel(q_ref, k_ref, v_ref, gg_ref, og_ref, wg_ref, bg_ref, ng_ref, o_ref, ob_ref, *, n_ctx, t):
    c = CHUNK
    nch = t // c
    nctx = n_ctx // c
    ii = lax.broadcasted_iota(jnp.int32, (c, c), 0)
    jj = lax.broadcasted_iota(jnp.int32, (c, c), 1)
    scale = GLA_DK ** -0.5

    nt = (((1,), (1,)), ((), ()))
    tn = (((0,), (0,)), ((), ()))
    dot = functools.partial(jnp.dot, preferred_element_type=F32)
    grp = GLA_GROUP
    assert nctx % grp == 0 and nch % grp == 0

    def group(g, carry):
        chains = []
        for d in range(2):
            for u in range(grp):
                i = g * grp + u
                ci = i if d == 0 else jnp.where(i < nctx, nctx - 1 - i, nch + nctx - 1 - i)
                chains.append((d, pl.ds(pl.multiple_of(ci * c, c), c)))
        keeps = [(ii >= jj) if d == 0 else (ii <= jj) for d, _ in chains]
        q = [_silu(q_ref[0, r, :]) * scale for _, r in chains]
        k = [_silu(k_ref[0, r, :]) for _, r in chains]
        vb = [_silu(v_ref[0, r, :]).astype(BF16) for _, r in chains]
        z = [dot(gg_ref[0, r, :][:, GLA_GATE_RANK * d:GLA_GATE_RANK * (d + 1)], wg_ref[d]) + bg_ref[d]
             for d, r in chains]
        la = [_log_sigmoid(x) * (1.0 / GLA_GATE_NORM) for x in z]
        bcum = [_tri_matmul(kp, x) for kp, x in zip(keeps, la)]
        blast = [b[c - 1:c, :] if d == 0 else b[0:1, :] for (d, _), b in zip(chains, bcum)]
        q_dec = [(x * jnp.exp(b)).astype(BF16) for x, b in zip(q, bcum)]
        k_inv = [(x * jnp.exp(-b)).astype(BF16) for x, b in zip(k, bcum)]
        k_end = [(x * jnp.exp(bl - b)).astype(BF16) for x, b, bl in zip(k, bcum, blast)]
        sc = [lax.dot_general(a, b, nt, preferred_element_type=F32) for a, b in zip(q_dec, k_inv)]
        sc = [jnp.where(kp, x, 0.0).astype(BF16) for kp, x in zip(keeps, sc)]
        intra = [dot(a, b) for a, b in zip(sc, vb)]
        kv = [lax.dot_general(a, b, tn, preferred_element_type=F32) for a, b in zip(k_end, vb)]
        dec = [_row_to_col(jnp.exp(bl)) for bl in blast]
        states = list(carry)
        for n, (d, r) in enumerate(chains):
            s = states[d]
            o = intra[n] + dot(q_dec[n], s.astype(BF16))
            if d == 0:
                o_ref[0, r, :] = o
            else:
                ob_ref[r, :] = o
            states[d] = dec[n] * s + kv[n]
        return tuple(states)

    s0 = jnp.zeros((GLA_DK, GLA_DV), F32)
    lax.fori_loop(0, nch // grp, group, (s0, s0))

    def finish(ci, carry):
        rows = pl.ds(pl.multiple_of(ci * c, c), c)
        o = o_ref[0, rows, :] + ob_ref[rows, :]
        o = o * lax.rsqrt(jnp.mean(o * o, axis=-1, keepdims=True) + 1e-5) * ng_ref[...]
        o_ref[0, rows, :] = o * _silu(og_ref[0, rows, :])
        return carry

    lax.fori_loop(0, nch, finish, 0, unroll=2)


def _gla_mixer(pcv, p, wg2, bg, norm_g, n_ctx):
    nb, t, _ = pcv.shape
    kern = functools.partial(_gla_kernel, n_ctx=n_ctx, t=t)
    return pl.pallas_call(
        kern,
        out_shape=jax.ShapeDtypeStruct((nb, t, GLA_V), F32),
        grid=(nb, GLA_HEADS),
        in_specs=[
            pl.BlockSpec((1, t, GLA_DK), lambda b, h: (b, 0, h)),
            pl.BlockSpec((1, t, GLA_DK), lambda b, h: (b, 0, GLA_HEADS + h)),
            pl.BlockSpec((1, t, GLA_DV), lambda b, h: (b, 0, GLA_HEADS + h)),
            pl.BlockSpec((1, t, 128), lambda b, h: (b, 0, OFF_GG // 128)),
            pl.BlockSpec((1, t, GLA_DV), lambda b, h: (b, 0, OFF_OG // GLA_DV + h)),
            pl.BlockSpec((2, GLA_GATE_RANK, GLA_DK), lambda b, h: (0, 0, h)),
            pl.BlockSpec((2, 1, GLA_DK), lambda b, h: (0, 0, h)),
            pl.BlockSpec((1, GLA_DV), lambda b, h: (0, 0)),
        ],
        out_specs=pl.BlockSpec((1, t, GLA_DV), lambda b, h: (b, 0, h)),
        scratch_shapes=[pltpu.VMEM((t, GLA_DV), F32)],
        compiler_params=_cparams(("arbitrary", "arbitrary")),
        name="gla_mixer",
    )(pcv, pcv, pcv, p, p, wg2, bg.reshape(2, 1, GLA_QK), norm_g.reshape(1, GLA_DV))


def _bd_mask():
    shift = RWKV_HEAD.bit_length() - 1
    r = lax.broadcasted_iota(jnp.int32, (RWKV_LANES, RWKV_LANES), 0) >> shift
    c = lax.broadcasted_iota(jnp.int32, (RWKV_LANES, RWKV_LANES), 1) >> shift
    return r == c


def _bd(x, mask):
    xb = x.astype(BF16)
    return jnp.where(mask, jnp.concatenate([xb] * RWKV_HPB, axis=0), jnp.zeros((), BF16))


def _head_sum(x, ones_bd):
    hi = x.astype(BF16)
    lo = (x - hi.astype(F32)).astype(BF16)
    return jnp.dot(hi, ones_bd, preferred_element_type=F32) + jnp.dot(lo, ones_bd, preferred_element_type=F32)


def _rwkv_group_kernel(rf_ref, kf_ref, vf_ref, rwf_ref, raf_ref, rb_ref, kb_ref, vb_ref, rwb_ref, rab_ref,
                       w0_ref, w2_ref, a0_ref, a2_ref, kkp_ref, kap_ref, yf_ref, yb_ref, stf_ref, stb_ref):
    c = CHUNK
    nchunk = RWKV_SEG // c
    mask_bd = _bd_mask()
    ones_bd = mask_bd.astype(BF16)
    dot = functools.partial(jnp.dot, preferred_element_type=F32)
    nt = (((1,), (1,)), ((), ()))
    tn = (((0,), (0,)), ((), ()))

    @pl.when(pl.program_id(2) == 0)
    def _():
        stf_ref[...] = jnp.zeros_like(stf_ref)
        stb_ref[...] = jnp.zeros_like(stb_ref)

    chains = [(0, slice(u * c, (u + 1) * c)) for u in range(nchunk)] \
        + [(1, slice(u * c, (u + 1) * c)) for u in range(nchunk - 1, -1, -1)]
    refs = ((rf_ref, kf_ref, vf_ref, rwf_ref, raf_ref), (rb_ref, kb_ref, vb_ref, rwb_ref, rab_ref))

    def each(f, *lists):
        return [f(*xs) for xs in zip(*lists)]

    ds = [d for d, _ in chains]
    r = [refs[d][0][0, rows, :] for d, rows in chains]
    k = [refs[d][1][0, rows, :] for d, rows in chains]
    v = [refs[d][2][0, rows, :] for d, rows in chains]
    pw = [refs[d][3][0, rows, :][:, RWKV_W_RANK * d:RWKV_W_RANK * (d + 1)] for d, rows in chains]
    pa = [refs[d][4][0, rows, :][:, RWKV_A_RANK * d:RWKV_A_RANK * (d + 1)] for d, rows in chains]
    lw = each(lambda d, x: w0_ref[d] + dot(jnp.tanh(x), w2_ref[d]), ds, pw)
    logw = [-RWKV_W_SCALE * jax.nn.sigmoid(x) for x in lw]
    la = each(lambda d, x: a0_ref[d] + dot(x, a2_ref[d]), ds, pa)
    a = [jax.nn.sigmoid(x) for x in la]
    kkv = [x * kkp_ref[...] for x in k]
    ss = [_head_sum(x * x, ones_bd) for x in kkv]
    kk = each(lambda x, s: x / jnp.maximum(jnp.sqrt(s), 1e-12), kkv, ss)
    kd = each(lambda x, y: x * (1.0 + (y - 1.0) * kap_ref[...]), k, a)
    kka = each(lambda x, y: x * y, kk, a)

    ii = lax.broadcasted_iota(jnp.int32, (c, c), 0)
    jj = lax.broadcasted_iota(jnp.int32, (c, c), 1)
    ti = lax.broadcasted_iota(jnp.int32, (c, RWKV_LANES), 0)
    si = lax.broadcasted_iota(jnp.int32, (c, RWKV_LANES), 1) & (c - 1)
    keep_sq = [(ii >= jj) if d == 0 else (ii <= jj) for d in ds]
    incl = [(si <= ti) if d == 0 else (si >= ti) for d in ds]
    strict = [(si < ti) if d == 0 else (si > ti) for d in ds]
    eye = (si == ti).astype(F32)

    cb = each(_tri_matmul, keep_sq, logw)
    cblast = each(lambda d, x: x[c - 1:c, :] if d == 0 else x[0:1, :], ds, cb)
    ad = each(lambda x, b, w: -x * jnp.exp(b - w), kk, cb, logw)
    rd = each(lambda x, b: x * jnp.exp(b), r, cb)
    einv = [jnp.exp(-b) for b in cb]
    bi = each(lambda x, e: x * e, kka, einv)
    ki = each(lambda x, e: x * e, kd, einv)
    eend = each(lambda bl, b: jnp.exp(bl - b), cblast, cb)
    bend = each(lambda x, e: x * e, kka, eend)
    kend = each(lambda x, e: x * e, kd, eend)

    def bd(x):
        return _bd(x, mask_bd)

    def mm(x, y_bd):
        return dot(x.astype(BF16), y_bd)

    lhs = each(lambda x, y: jnp.concatenate([x, y], axis=0).astype(BF16), ad, rd)
    pb = each(lambda x, y: lax.dot_general(x, bd(y), nt, preferred_element_type=F32), lhs, bi)
    pk = each(lambda x, y: lax.dot_general(x, bd(y), nt, preferred_element_type=F32), lhs, ki)
    a_ab = each(lambda m, x: jnp.where(m, x[0:c], 0.0), strict, pb)
    r_ab = each(lambda m, x: jnp.where(m, x[c:2 * c], 0.0), incl, pb)
    a_ak = each(lambda m, x: jnp.where(m, x[0:c], 0.0), strict, pk)
    r_ak = each(lambda m, x: jnp.where(m, x[c:2 * c], 0.0), incl, pk)

    tinv = [eye + x for x in a_ab]
    pwr = each(lambda x: mm(x, bd(x)), a_ab)
    for lvl in range(5):
        if lvl < 4:
            both = each(lambda t_, p_: mm(jnp.concatenate([t_, p_], axis=0), bd(p_)), tinv, pwr)
            tinv = each(lambda t_, b: t_ + b[0:c], tinv, both)
            pwr = [b[c:2 * c] for b in both]
        else:
            tinv = each(lambda t_, p_: t_ + mm(t_, bd(p_)), tinv, pwr)

    ln = RWKV_LANES
    akv = each(lambda x, y, v_: mm(jnp.concatenate([x, y], axis=0), bd(v_)), a_ak, r_ak, v)
    mu = each(lambda t_, x, av: mm(t_, jnp.concatenate([bd(x), bd(av[0:c])], axis=1)), tinv, ad, akv)
    m1 = [x[:, 0:ln] for x in mu]
    u0 = [x[:, ln:2 * ln] for x in mu]
    ry = each(lambda rab, m, u_: mm(rab, jnp.concatenate([bd(m), bd(u_)], axis=1)), r_ab, m1, u0)
    r1 = each(lambda x, z: (x + z[:, 0:ln]).astype(BF16), rd, ry)
    y0 = each(lambda z, av: z[:, ln:2 * ln] + av[c:2 * c], ry, akv)
    g1 = each(lambda x, m: jnp.where(mask_bd, lax.dot_general(x.astype(BF16), m.astype(BF16), tn,
                                                              preferred_element_type=F32), 0.0).astype(BF16),
              bend, m1)
    h = each(lambda x, y, u_, v_: jnp.where(mask_bd, lax.dot_general(
        jnp.concatenate([x, y], axis=0).astype(BF16), jnp.concatenate([u_, v_], axis=0).astype(BF16), tn,
        preferred_element_type=F32), 0.0), bend, kend, u0, v)
    dec = [_row_to_col(jnp.exp(x)) for x in cblast]

    states = [stf_ref[...], stb_ref[...]]
    outs = (yf_ref, yb_ref)
    for n, (d, rows) in enumerate(chains):
        st = states[d]
        stb = st.astype(BF16)
        outs[d][0, rows, :] = dot(r1[n], stb) + y0[n]
        states[d] = dec[n] * st + dot(g1[n], stb) + h[n]
    stf_ref[...] = states[0]
    stb_ref[...] = states[1]


def _rwkv_post_kernel(r_ref, k_ref, v_ref, ra_ref, rg_ref, yf_ref, yb_ref, a0_ref, a2_ref, g2_ref,
                      kap_ref, rkp_ref, lng_ref, lnb_ref, o_ref):
    ones_bd = _bd_mask().astype(BF16)
    r = r_ref[0]
    k = k_ref[0]
    ra = ra_ref[0]
    y = yf_ref[0] + yb_ref[0]
    inv_n = 1.0 / RWKV_HEAD
    mu = _head_sum(y, ones_bd) * inv_n
    yc = y - mu
    var = _head_sum(yc * yc, ones_bd) * inv_n
    yn = yc * lax.rsqrt(var + RWKV_LN_EPS) * lng_ref[...] + lnb_ref[...]
    kds = jnp.zeros_like(k)
    for d in range(2):
        la = a0_ref[d] + jnp.dot(ra[:, RWKV_A_RANK * d:RWKV_A_RANK * (d + 1)], a2_ref[d],
                                 preferred_element_type=F32)
        kds = kds + k * (1.0 + (jax.nn.sigmoid(la) - 1.0) * kap_ref[...])
    bonus = _head_sum(r * kds * rkp_ref[...], ones_bd) * v_ref[0]
    g = jnp.dot(jax.nn.sigmoid(rg_ref[0]), g2_ref[...], preferred_element_type=F32)
    o_ref[0] = (yn + bonus) * g


def _rwkv_mixer(pcv, p, prm, n_ctx):
    nb, t, _ = pcv.shape
    seg = RWKV_SEG
    nseg = t // seg
    nseg_ctx = n_ctx // seg
    ln = RWKV_LANES
    nhb = RWKV_DIM // ln
    cb0 = GLA_CONV // ln

    def fmap(s):
        return s

    def bmap(s):
        return jnp.where(s < nseg_ctx, nseg_ctx - 1 - s, nseg - 1 - (s - nseg_ctx))

    def tok(off_blocks, smap):
        return pl.BlockSpec((1, seg, ln), lambda b, h, s: (b, smap(s), off_blocks + h))

    def small(off, smap):
        return pl.BlockSpec((1, seg, 128), lambda b, h, s: (b, smap(s), off // 128))

    def dirspecs(smap):
        return [tok(cb0, smap), tok(cb0 + nhb, smap), tok(cb0 + 2 * nhb, smap),
                small(OFF_RW, smap), small(OFF_RA, smap)]

    def vec():
        return pl.BlockSpec((1, ln), lambda b, h, s: (0, h))

    w0 = prm["w0"].reshape(2, 1, RWKV_DIM)
    a0 = prm["a0"].reshape(2, 1, RWKV_DIM)
    kkp = prm["kk"].reshape(1, RWKV_DIM)
    kap = prm["ka"].reshape(1, RWKV_DIM)
    yf, yb = pl.pallas_call(
        _rwkv_group_kernel,
        out_shape=(jax.ShapeDtypeStruct((nb, t, RWKV_DIM), F32),) * 2,
        grid=(nb, nhb, nseg),
        in_specs=dirspecs(fmap) + dirspecs(bmap) + [
            pl.BlockSpec((2, 1, ln), lambda b, h, s: (0, 0, h)),
            pl.BlockSpec((2, RWKV_W_RANK, ln), lambda b, h, s: (0, 0, h)),
            pl.BlockSpec((2, 1, ln), lambda b, h, s: (0, 0, h)),
            pl.BlockSpec((2, RWKV_A_RANK, ln), lambda b, h, s: (0, 0, h)),
            vec(), vec()],
        out_specs=(pl.BlockSpec((1, seg, ln), lambda b, h, s: (b, fmap(s), h)),
                   pl.BlockSpec((1, seg, ln), lambda b, h, s: (b, bmap(s), h))),
        scratch_shapes=[pltpu.VMEM((ln, ln), F32)] * 2,
        compiler_params=_cparams(("arbitrary", "arbitrary", "arbitrary")),
        name="rwkv7_scan",
    )(pcv, pcv, pcv, p, p, pcv, pcv, pcv, p, p, w0, prm["w2"], a0, prm["a2"], kkp, kap)

    tm = _row_tile(t, 1088)

    def ptok(off_blocks):
        return pl.BlockSpec((1, tm, ln), lambda b, i, h: (b, i, off_blocks + h))

    def psmall(off):
        return pl.BlockSpec((1, tm, 128), lambda b, i, h: (b, i, off // 128))

    def pvec():
        return pl.BlockSpec((1, ln), lambda b, i, h: (0, h))

    return pl.pallas_call(
        _rwkv_post_kernel,
        out_shape=jax.ShapeDtypeStruct((nb, t, RWKV_DIM), F32),
        grid=(nb, t // tm, nhb),
        in_specs=[ptok(cb0), ptok(cb0 + nhb), ptok(cb0 + 2 * nhb), psmall(OFF_RA), psmall(OFF_RG),
                  ptok(0), ptok(0),
                  pl.BlockSpec((2, 1, ln), lambda b, i, h: (0, 0, h)),
                  pl.BlockSpec((2, RWKV_A_RANK, ln), lambda b, i, h: (0, 0, h)),
                  pl.BlockSpec((RWKV_G_RANK, ln), lambda b, i, h: (0, h)),
                  pvec(), pvec(), pvec(), pvec()],
        out_specs=ptok(0),
        compiler_params=_cparams(("arbitrary", "arbitrary", "arbitrary")),
        name="rwkv7_post",
    )(pcv, pcv, pcv, p, p, yf, yb, a0, prm["a2"], prm["g2"], kap, prm["rk"].reshape(1, RWKV_DIM),
      prm["ln_g"].reshape(1, RWKV_DIM), prm["ln_b"].reshape(1, RWKV_DIM))


def _s5_operators(lam_re, lam_im, log_dt, b_re, b_im, c_re, c_im):
    g, p_, lc = S5_GROUPS, S5_STATE, S5_CHUNK
    lr, li = lam_re.astype(F32), lam_im.astype(F32)
    br, bi = b_re.astype(F32), b_im.astype(F32)
    den = lr * lr + li * li
    n = jnp.arange(lc + 1, dtype=F32)[:, None, None]
    ein = functools.partial(jnp.einsum, precision=HI)
    out = []
    for d in range(2):
        dt = jnp.exp(log_dt[d].astype(F32))[:, None]
        mag = jnp.exp(n * (lr * dt)[None])
        ang = n * (li * dt)[None]
        pr, pi = mag * jnp.cos(ang), mag * jnp.sin(ang)
        m1 = jnp.exp(lr * dt)
        ar, ai = m1 * jnp.cos(li * dt), m1 * jnp.sin(li * dt)
        fr = ((ar - 1.0) * lr + ai * li) / den
        fi = (ai * lr - (ar - 1.0) * li) / den
        bb_r = fr[..., None] * br - fi[..., None] * bi
        bb_i = fr[..., None] * bi + fi[..., None] * br
        cr, ci = c_re[d].astype(F32), c_im[d].astype(F32)
        ca_r = cr[None] * pr[:, :, None, :] - ci[None] * pi[:, :, None, :]
        ca_i = cr[None] * pi[:, :, None, :] + ci[None] * pr[:, :, None, :]
        kt = ein('tgop,gpc->tgoc', ca_r[:lc], bb_r) - ein('tgop,gpc->tgoc', ca_i[:lc], bb_i)
        nlg, gl, ch = S5_LANE_GROUPS, S5_GROUPS // S5_LANE_GROUPS, S5_GROUP_CH

        def block_diag(x, wr, wc):
            rg = jnp.arange(gl * wr)[:, None] // wr
            cg = jnp.arange(gl * wc)[None, :] // wc
            return jnp.where(rg == cg, jnp.tile(x, (1,) * (x.ndim - 1) + (gl,)), 0.0)

        ktc = kt.transpose(1, 0, 3, 2).reshape(nlg, gl, lc, ch, ch).transpose(0, 2, 1, 3, 4)
        blk = block_diag(ktc.reshape(nlg, lc, gl * ch, ch), ch, ch)
        zer = jnp.zeros((nlg, lc - 1, 128, 128), F32)
        lagblk = jnp.concatenate([zer, blk], axis=1) if d == 0 else jnp.concatenate([blk[:, ::-1], zer], axis=1)

        e_s = (lc - 1 - jnp.arange(lc)) if d == 0 else jnp.arange(lc)
        ps_r, ps_i = pr[e_s], pi[e_s]
        bst_r = ps_r[..., None] * bb_r[None] - ps_i[..., None] * bb_i[None]
        bst_i = ps_r[..., None] * bb_i[None] + ps_i[..., None] * bb_r[None]

        def rows_op(x):
            x = x.reshape(lc, nlg, gl, p_, ch).transpose(1, 0, 2, 4, 3)
            return block_diag(x.reshape(nlg, lc, gl * ch, p_), ch, p_).reshape(nlg, lc * gl * ch, gl * p_)

        f_t = (jnp.arange(lc) + 1) if d == 0 else (lc - jnp.arange(lc))

        def cols_op(x):
            x = x.reshape(lc, nlg, gl, ch, p_).transpose(1, 0, 2, 4, 3)
            return block_diag(x.reshape(nlg, lc, gl * p_, ch), p_, ch)

        out.append(dict(
            lag=lagblk.astype(BF16), bst_r=rows_op(bst_r).astype(BF16), bst_i=rows_op(bst_i).astype(BF16),
            co_r=cols_op(ca_r[f_t]).astype(BF16), co_i=cols_op(-ca_i[f_t]).astype(BF16),
            a_r=pr[lc].reshape(nlg, 1, gl * p_), a_i=pi[lc].reshape(nlg, 1, gl * p_)))
    return {k: jnp.stack([out[0][k], out[1][k]]) for k in out[0]}


def _s5_kernel(u_ref, lag_ref, br_ref, bi_ref, cr_ref, ci_ref, ar_ref, ai_ref, o_ref,
               ir_ref, ii_ref, xr_ref, xi_ref, op_ref, cor_ref, coi_ref, *, nchunk, nchunk_ctx):
    d = pl.program_id(1)
    lc = S5_CHUNK
    dot = functools.partial(jnp.dot, preferred_element_type=F32)

    @pl.when(pl.program_id(2) == 0)
    def _():
        for t in range(lc):
            cor_ref[:, t * 128:(t + 1) * 128] = cr_ref[0, 0, t]
            coi_ref[:, t * 128:(t + 1) * 128] = ci_ref[0, 0, t]
            for s in range(lc):
                op_ref[s * 128:(s + 1) * 128, t * 128:(t + 1) * 128] = lag_ref[0, 0, t - s + lc - 1]

    lhs = jnp.concatenate([u_ref[0, pl.ds(s, nchunk, stride=lc), :] for s in range(lc)], axis=1).astype(BF16)
    ir_ref[...] = dot(lhs, br_ref[0, 0])
    ii_ref[...] = dot(lhs, bi_ref[0, 0])
    ar = ar_ref[0, 0]
    ai = ai_ref[0, 0]

    def step(i, carry):
        sr, si = carry
        rev = jnp.where(i < nchunk_ctx, nchunk_ctx - 1 - i, nchunk + nchunk_ctx - 1 - i)
        ci = jnp.where(d == 0, i, rev)
        row = pl.ds(ci, 1)
        xr_ref[row, :] = sr
        xi_ref[row, :] = si
        nr = ar * sr - ai * si + ir_ref[row, :]
        ni = ar * si + ai * sr + ii_ref[row, :]
        return nr, ni

    z = jnp.zeros((1, ir_ref.shape[1]), F32)
    lax.fori_loop(0, nchunk, step, (z, z), unroll=8)
    y = dot(lhs, op_ref[...]) + dot(xr_ref[...].astype(BF16), cor_ref[...]) \
        + dot(xi_ref[...].astype(BF16), coi_ref[...])
    for t in range(lc):
        o_ref[0, 0, pl.ds(t, nchunk, stride=lc), :] = y[:, t * 128:(t + 1) * 128]


def _s5_mixer(p, ops, n_ctx):
    nb, t, _ = p.shape
    lc = S5_CHUNK
    nchunk = t // lc
    nlg = S5_LANE_GROUPS
    ns = ops["a_r"].shape[-1]

    def wspec(x):
        nd = x.ndim - 2
        return pl.BlockSpec((1, 1) + x.shape[2:], lambda g, d, b: (d, g) + (0,) * nd)

    keys = ("lag", "bst_r", "bst_i", "co_r", "co_i", "a_r", "a_i")
    nop = lc * 128
    kern = functools.partial(_s5_kernel, nchunk=nchunk, nchunk_ctx=n_ctx // lc)
    return pl.pallas_call(
        kern,
        out_shape=jax.ShapeDtypeStruct((2, nb, t, D_MODEL), F32),
        grid=(nlg, 2, nb),
        in_specs=[pl.BlockSpec((1, t, 128), lambda g, d, b: (b, 0, OFF_U // 128 + g))]
        + [wspec(ops[k]) for k in keys],
        out_specs=pl.BlockSpec((1, 1, t, 128), lambda g, d, b: (d, b, 0, g)),
        scratch_shapes=[pltpu.VMEM((nchunk, ns), F32)] * 4
        + [pltpu.VMEM((nop, nop), BF16), pltpu.VMEM((ns, nop), BF16), pltpu.VMEM((ns, nop), BF16)],
        compiler_params=_cparams(("arbitrary", "arbitrary", "arbitrary")),
        name="s5_mixer",
    )(p, *[ops[k] for k in keys])


def _merge_kernel(ygla_ref, yrw_ref, ys5f_ref, ys5b_ref, u_ref, ga_ref, gb_ref, gc_ref, x_ref, mod_ref, wb_ref, wo_ref,
                  d_ref, gw_ref, gbias_ref, o_ref, *, tm, n_ctx, nb):
    b = pl.program_id(0)
    i = pl.program_id(1)
    dot = functools.partial(jnp.dot, preferred_element_type=F32)
    z = jax.nn.gelu(ys5f_ref[0, 0] + ys5b_ref[0, 0] + d_ref[...] * u_ref[0])
    zs = z * jax.nn.sigmoid(dot(z.astype(BF16), gw_ref[...]) + gbias_ref[...])
    m = jax.nn.sigmoid(ga_ref[0]) * dot(ygla_ref[0].astype(BF16), wb_ref[0]) \
        + jax.nn.sigmoid(gb_ref[0]) * dot(yrw_ref[0].astype(BF16), wb_ref[1]) \
        + jax.nn.sigmoid(gc_ref[0]) * dot(zs.astype(BF16), wb_ref[2])
    mix = dot(m.astype(BF16), wo_ref[...])
    row = i * tm + lax.broadcasted_iota(jnp.int32, (tm, 1), 0)
    gate = _mod_rows(mod_ref, b, nb, 2 * D_MODEL, row < n_ctx)
    o_ref[0] = x_ref[0] + gate * mix


def _merge(ygla, yrw, ys5, p, resid, mod, w_branch, w_out, s5_d, glu_w, glu_b, n_ctx):
    nb, t, d = resid.shape
    tm = _row_tile(t, 272)
    kern = functools.partial(_merge_kernel, tm=tm, n_ctx=n_ctx, nb=nb)

    def tok():
        return pl.BlockSpec((1, tm, d), lambda b, i: (b, i, 0))

    return pl.pallas_call(
        kern,
        out_shape=jax.ShapeDtypeStruct((nb, t, d), F32),
        grid=(nb, t // tm),
        in_specs=[
            tok(), tok(),
            pl.BlockSpec((1, 1, tm, d), lambda b, i: (0, b, i, 0)),
            pl.BlockSpec((1, 1, tm, d), lambda b, i: (1, b, i, 0)),
            pl.BlockSpec((1, tm, d), lambda b, i: (b, i, OFF_U // d)),
            pl.BlockSpec((1, tm, d), lambda b, i: (b, i, OFF_GATES // d)),
            pl.BlockSpec((1, tm, d), lambda b, i: (b, i, OFF_GATES // d + 1)),
            pl.BlockSpec((1, tm, d), lambda b, i: (b, i, OFF_GATES // d + 2)),
            tok(),
            pl.BlockSpec(mod.shape, lambda b, i: (0, 0)),
            pl.BlockSpec((3, d, d), lambda b, i: (0, 0, 0)),
            pl.BlockSpec((d, d), lambda b, i: (0, 0)),
            pl.BlockSpec((1, d), lambda b, i: (0, 0)),
            pl.BlockSpec((d, d), lambda b, i: (0, 0)),
            pl.BlockSpec((1, d), lambda b, i: (0, 0)),
        ],
        out_specs=tok(),
        compiler_params=_cparams(("arbitrary", "arbitrary")),
        name="merge",
    )(ygla, yrw, ys5, ys5, p, p, p, p, resid, mod, w_branch.astype(BF16), w_out.astype(BF16), s5_d.reshape(1, d),
      glu_w.astype(BF16), glu_b.reshape(1, d))


def _router_kernel(x_ref, nw_ref, mod_ref, rw_ref, rb_ref, h_ref, comb_ref, combt_ref, rank_ref, rankt_ref,
                   offs_ref, *, tm, n_ctx, nb):
    b = pl.program_id(0)
    i = pl.program_id(1)
    x = x_ref[0]
    xn = x * lax.rsqrt(jnp.mean(x * x, axis=-1, keepdims=True) + NORM_EPS) * nw_ref[...]
    row = i * tm + lax.broadcasted_iota(jnp.int32, (tm, 1), 0)
    is_ctx = row < n_ctx
    sh = _mod_rows(mod_ref, b, nb, 3 * D_MODEL, is_ctx)
    sc = _mod_rows(mod_ref, b, nb, 4 * D_MODEL, is_ctx)
    h = xn * (1.0 + sc) + sh
    h_ref[0] = h.astype(BF16)
    logits = jnp.dot(h, rw_ref[...], precision=HI, preferred_element_type=F32) + rb_ref[...]
    lane = lax.broadcasted_iota(jnp.int32, logits.shape, 1).astype(F32)
    neg = jnp.float32(-jnp.inf)
    cur = logits
    comb = jnp.zeros_like(logits)
    denom = jnp.zeros((tm, 1), F32)
    top = None
    for kk in range(TOP_K):
        m = jnp.max(cur, axis=-1, keepdims=True)
        idx = jnp.min(jnp.where(cur == m, lane, 128.0), axis=-1, keepdims=True)
        sel = lane == idx
        if kk == 0:
            top = m
        e = jnp.exp(m - top)
        comb = comb + jnp.where(sel, e, 0.0)
        denom = denom + e
        cur = jnp.where(sel, neg, cur)
    comb = comb / denom
    comb_ref[0] = comb

    chosen = jnp.where(comb > 0.0, 1.0, 0.0).astype(BF16)
    ri = lax.broadcasted_iota(jnp.int32, (tm, tm), 0)
    ci = lax.broadcasted_iota(jnp.int32, (tm, tm), 1)
    rank = jnp.dot((ci < ri).astype(BF16), chosen, preferred_element_type=F32)
    rank_ref[0] = rank
    rankt_ref[0] = lax.dot_general(chosen, (ri < ci).astype(BF16), (((0,), (0,)), ((), ())),
                                   preferred_element_type=F32)
    eye = (lax.broadcasted_iota(jnp.int32, (128, 128), 0)
           == lax.broadcasted_iota(jnp.int32, (128, 128), 1)).astype(BF16)
    h1 = comb.astype(BF16)
    r1 = comb - h1.astype(F32)
    h2 = r1.astype(BF16)
    h3 = (r1 - h2.astype(F32)).astype(BF16)
    nt = (((1,), (1,)), ((), ()))
    combt_ref[0] = lax.dot_general(eye, h1, nt, preferred_element_type=F32) \
        + lax.dot_general(eye, h2, nt, preferred_element_type=F32) \
        + lax.dot_general(eye, h3, nt, preferred_element_type=F32)
    tile = tm // MOE_TILES
    rows = [rank[j * tile:j * tile + 1, :] for j in range(MOE_TILES)]
    rows.append(rank[tm - 1:tm, :] + chosen[tm - 1:tm, :].astype(F32))
    rows.append(jnp.zeros((16 - len(rows), 128), F32))
    offs_ref[0] = jnp.concatenate(rows, axis=0)


def _router(resid, norm_w, mod, router_w, router_b, n_ctx):
    nb, t, d = resid.shape
    tm = _row_tile(t, 1088)
    nper = t // tm
    nq = nb * nper
    rw = jnp.pad(router_w, ((0, 0), (0, 128 - N_EXPERTS)))
    rb = jnp.pad(router_b, (0, 128 - N_EXPERTS), constant_values=-1e30).reshape(1, 128)
    kern = functools.partial(_router_kernel, tm=tm, n_ctx=n_ctx, nb=nb)

    def qspec(r, c):
        return pl.BlockSpec((1, r, c), lambda b, i: (b * nper + i, 0, 0))

    return pl.pallas_call(
        kern,
        out_shape=(jax.ShapeDtypeStruct((nb, t, d), BF16),
                   jax.ShapeDtypeStruct((nq, tm, 128), F32), jax.ShapeDtypeStruct((nq, 128, tm), F32),
                   jax.ShapeDtypeStruct((nq, tm, 128), F32), jax.ShapeDtypeStruct((nq, 128, tm), F32),
                   jax.ShapeDtypeStruct((nq, 16, 128), F32)),
        grid=(nb, nper),
        in_specs=[
            pl.BlockSpec((1, tm, d), lambda b, i: (b, i, 0)),
            pl.BlockSpec((1, d), lambda b, i: (0, 0)),
            pl.BlockSpec(mod.shape, lambda b, i: (0, 0)),
            pl.BlockSpec((d, 128), lambda b, i: (0, 0)),
            pl.BlockSpec((1, 128), lambda b, i: (0, 0)),
        ],
        out_specs=(pl.BlockSpec((1, tm, d), lambda b, i: (b, i, 0)),
                   qspec(tm, 128), qspec(128, tm), qspec(tm, 128), qspec(128, tm), qspec(16, 128)),
        compiler_params=_cparams(("arbitrary", "arbitrary")),
        name="router",
    )(resid, norm_w.reshape(1, d), mod, rw, rb)


MOE_TILES = 8
MOE_FFN_ROWS = 160


def _moe_routed_kernel(off_ref, h_ref, comb_ref, combt_ref, rank_ref, rankt_ref, wg_ref, bg_ref, wu_ref, bu_ref,
                       wd_ref, bd_ref, o_ref, g_ref, y_ref, *, qt):
    q = pl.program_id(0)
    e = pl.program_id(1)
    ne = pl.num_programs(1)
    tile = qt // MOE_TILES
    win = tile + 8
    fb = MOE_FFN_ROWS
    dot = functools.partial(jnp.dot, preferred_element_type=F32)

    @pl.when(e == 0)
    def _():
        o_ref[...] = jnp.zeros_like(o_ref)
        g_ref[...] = jnp.zeros_like(g_ref)
        y_ref[...] = jnp.zeros_like(y_ref)

    base = (q * ne + e) * (MOE_TILES + 1)
    wrow = combt_ref[0, pl.ds(e, 1), :]
    rrow = rankt_ref[0, pl.ds(e, 1), :]
    sub_i = lax.broadcasted_iota(jnp.int32, (win, 1), 0)
    sub_f = lax.broadcasted_iota(jnp.int32, (win, tile), 0).astype(F32)
    lane_f = lax.broadcasted_iota(jnp.int32, (tile, win), 1).astype(F32)
    lane_e = lax.broadcasted_iota(jnp.int32, (tile, 128), 1) == e

    for j in range(MOE_TILES):
        off = off_ref[base + j]
        nxt = off_ref[base + j + 1]
        ws = pl.multiple_of((off // 8) * 8, 8)
        rj = rrow[:, j * tile:(j + 1) * tile] - ws.astype(F32)
        wj = wrow[:, j * tile:(j + 1) * tile]
        onehot = jnp.where((sub_f == rj) & (wj > 0.0), 1.0, 0.0).astype(BF16)
        new = dot(onehot, h_ref[0, j * tile:(j + 1) * tile, :])
        slot = sub_i + ws
        hit = (slot >= off) & (slot < nxt)
        g_ref[pl.ds(ws, win), :] = jnp.where(hit, new, g_ref[pl.ds(ws, win), :])

    n = off_ref[base + MOE_TILES]

    def ffn(i, carry):
        rows = pl.ds(pl.multiple_of(i * fb, 8), fb)
        x = g_ref[rows, :].astype(BF16)
        gl = jnp.minimum(dot(x, wg_ref[0]) + bg_ref[0], SWIGLU_LIMIT)
        li = jnp.clip(dot(x, wu_ref[0]) + bu_ref[0], -SWIGLU_LIMIT, SWIGLU_LIMIT)
        act = gl * jax.nn.sigmoid(SWIGLU_ALPHA * gl) * (li + 1.0)
        y_ref[rows, :] = dot(act.astype(BF16), wd_ref[0]) + bd_ref[0]
        return carry

    lax.fori_loop(0, (n + fb - 1) // fb, ffn, 0)

    for j in range(MOE_TILES):
        off = off_ref[base + j]
        ws = pl.multiple_of((off // 8) * 8, 8)
        rows = slice(j * tile, (j + 1) * tile)
        wcol = jnp.sum(jnp.where(lane_e, comb_ref[0, rows, :], 0.0), axis=-1, keepdims=True)
        rcol = jnp.sum(jnp.where(lane_e, rank_ref[0, rows, :], 0.0), axis=-1, keepdims=True)
        onehot = jnp.where((lane_f == rcol - ws.astype(F32)) & (wcol > 0.0), 1.0, 0.0).astype(BF16)
        contrib = dot(onehot, y_ref[pl.ds(ws, win), :].astype(BF16))
        o_ref[0, rows, :] += wcol * contrib


def _moe_routed(h, combq, combt, rank, rankt, offs, wg, bg, wu, bu, wd, bd):
    nb, t, d = h.shape
    nq, qt, _ = combq.shape
    assert qt % (8 * MOE_TILES) == 0
    ne = N_EXPERTS
    off = offs[:, :MOE_TILES + 1, :ne].astype(jnp.int32).transpose(0, 2, 1).reshape(-1)
    gr = qt + MOE_FFN_ROWS + 16
    kern = functools.partial(_moe_routed_kernel, qt=qt)

    def wspec():
        return pl.BlockSpec((1, d, d), lambda q, e, off: (e, 0, 0))

    def bspec():
        return pl.BlockSpec((1, 1, d), lambda q, e, off: (e, 0, 0))

    def tokspec(w):
        return pl.BlockSpec((1, qt, w), lambda q, e, off: (q, 0, 0))

    def tspec():
        return pl.BlockSpec((1, 128, qt), lambda q, e, off: (q, 0, 0))

    out = pl.pallas_call(
        kern,
        out_shape=jax.ShapeDtypeStruct((nq, qt, d), F32),
        grid_spec=pltpu.PrefetchScalarGridSpec(
            num_scalar_prefetch=1,
            grid=(nq, ne),
            in_specs=[tokspec(d), tokspec(128), tspec(), tokspec(128), tspec(),
                      wspec(), bspec(), wspec(), bspec(), wspec(), bspec()],
            out_specs=tokspec(d),
            scratch_shapes=[pltpu.VMEM((gr, d), F32), pltpu.VMEM((gr, d), F32)]),
        compiler_params=_cparams(("arbitrary", "arbitrary")),
        name="moe_routed",
    )(off, h.reshape(nq, qt, d), combq, combt, rank, rankt, wg.astype(BF16), bg.reshape(ne, 1, d),
      wu.astype(BF16), bu.reshape(ne, 1, d), wd.astype(BF16), bd.reshape(ne, 1, d))
    return out.reshape(nb, t, d)


def _resid_gate_kernel(x_ref, y_ref, mod_ref, o_ref, *, tm, n_ctx, nb, off):
    b = pl.program_id(0)
    i = pl.program_id(1)
    row = i * tm + lax.broadcasted_iota(jnp.int32, (tm, 1), 0)
    gate = _mod_rows(mod_ref, b, nb, off, row < n_ctx)
    o_ref[0] = x_ref[0] + gate * y_ref[0]


def _resid_gate(resid, y, mod, off, n_ctx):
    nb, t, d = resid.shape
    tm = _row_tile(t, 1088)
    kern = functools.partial(_resid_gate_kernel, tm=tm, n_ctx=n_ctx, nb=nb, off=off)

    def tok():
        return pl.BlockSpec((1, tm, d), lambda b, i: (b, i, 0))

    return pl.pallas_call(
        kern,
        out_shape=jax.ShapeDtypeStruct((nb, t, d), F32),
        grid=(nb, t // tm),
        in_specs=[tok(), tok(), pl.BlockSpec(mod.shape, lambda b, i: (0, 0))],
        out_specs=tok(),
        compiler_params=_cparams(("arbitrary", "arbitrary")),
        name="resid_gate",
    )(resid, y, mod)


def _final_norm_kernel(x_ref, w_ref, o_ref):
    x = x_ref[0]
    o_ref[0] = x * lax.rsqrt(jnp.mean(x * x, axis=-1, keepdims=True) + NORM_EPS) * w_ref[...]


def _final_norm(resid, w, n_ctx):
    nb, t, d = resid.shape
    tm = n_ctx
    return pl.pallas_call(
        _final_norm_kernel,
        out_shape=jax.ShapeDtypeStruct((nb, t - n_ctx, d), F32),
        grid=(nb, (t - n_ctx) // tm),
        in_specs=[pl.BlockSpec((1, tm, d), lambda b, i: (b, i + 1, 0)),
                  pl.BlockSpec((1, d), lambda b, i: (0, 0))],
        out_specs=pl.BlockSpec((1, tm, d), lambda b, i: (b, i, 0)),
        compiler_params=_cparams(("arbitrary", "arbitrary")),
        name="final_norm",
    )(resid, w.reshape(1, d))


def _pack_w_in(w):
    o_gg = CONV_CH + GLA_V
    o_rg = o_gg + 2 * GLA_GATE_RANK
    o_rw = o_rg + RWKV_G_RANK
    o_ra = o_rw + 2 * RWKV_W_RANK
    o_u = o_ra + 2 * RWKV_A_RANK
    o_gt = o_u + D_MODEL
    pad = jnp.zeros((w.shape[0], 128 - 2 * GLA_GATE_RANK), w.dtype)
    return jnp.concatenate([w[:, :o_gg], w[:, o_u:o_gt], w[:, o_gt:], w[:, o_gg:o_rg], pad,
                            w[:, o_rg:o_rw], w[:, o_rw:o_ra], w[:, o_ra:o_u]], axis=1).astype(BF16)


def kernel(x, c, ctx, c_ctx, w_ada, b_ada, norm_mix, w_in, conv_w, gla_wg2, gla_bg, gla_norm, rw_w0, rw_w2, rw_a0, rw_a2, rw_g2, rw_kk, rw_ka, rw_rk, rw_ln_g, rw_ln_b, s5_lam_re, s5_lam_im, s5_log_dt, s5_b_re, s5_b_im, s5_c_re, s5_c_im, s5_d, s5_glu_w, s5_glu_b, w_branch, w_out, norm_ffn, router_w, router_b, exp_w_gate, exp_b_gate, exp_w_up, exp_b_up, exp_w_down, exp_b_down, final_norm):
    depth = w_ada.shape[0]
    nb, _, d = x.shape
    n_ctx = ctx.shape[1]
    assert d == D_MODEL and n_ctx % RWKV_SEG == 0 and x.shape[1] % RWKV_SEG == 0 and nb < MOD_ROWS

    resid = jnp.concatenate([ctx, x], axis=1).astype(F32)
    cc = jnp.concatenate([c, c_ctx[None], jnp.zeros((MOD_ROWS - nb - 1, d), F32)], axis=0)
    mods = _ada_table(cc, w_ada, b_ada)

    for l in range(depth):
        mod = mods[l]
        p = _norm_in_proj(resid, norm_mix[l], mod, _pack_w_in(w_in[l]), n_ctx)
        pcv = _short_conv(p, conv_w[l], n_ctx)
        y_gla = _gla_mixer(pcv, p, gla_wg2[l], gla_bg[l], gla_norm[l], n_ctx)
        prm = dict(w0=rw_w0[l], w2=rw_w2[l], a0=rw_a0[l], a2=rw_a2[l], g2=rw_g2[l], kk=rw_kk[l], ka=rw_ka[l],
                   rk=rw_rk[l].reshape(-1), ln_g=rw_ln_g[l], ln_b=rw_ln_b[l])
        y_rw = _rwkv_mixer(pcv, p, prm, n_ctx)
        ops = _s5_operators(s5_lam_re[l], s5_lam_im[l], s5_log_dt[l], s5_b_re[l], s5_b_im[l],
                            s5_c_re[l], s5_c_im[l])
        y_s5 = _s5_mixer(p, ops, n_ctx)
        resid = _merge(y_gla, y_rw, y_s5, p, resid, mod, w_branch[l], w_out[l], s5_d[l], s5_glu_w[l],
                       s5_glu_b[l], n_ctx)
        h, *routing = _router(resid, norm_ffn[l], mod, router_w[l], router_b[l], n_ctx)
        y_moe = _moe_routed(h, *routing, exp_w_gate[l], exp_b_gate[l], exp_w_up[l], exp_b_up[l],
                            exp_w_down[l], exp_b_down[l])
        resid = _resid_gate(resid, y_moe, mod, 5 * D_MODEL, n_ctx)
    return _final_norm(resid, final_norm, n_ctx)
```

```python
import functools
import math

import jax
import jax.numpy as jnp
from jax import lax
from jax.experimental import pallas as pl
from jax.experimental.pallas import tpu as pltpu

F32 = jnp.float32
BF16 = jnp.bfloat16
HI = lax.Precision.HIGHEST

D_MODEL = 1024
GRID_W = 64

GLA_HEADS = 4
GLA_DK = 128
GLA_DV = 256
GLA_QK = GLA_HEADS * GLA_DK
GLA_V = GLA_HEADS * GLA_DV
GLA_GATE_RANK = 16
GLA_GATE_NORM = 16.0
CHUNK = 64
GLA_GROUP = 4

RWKV_HEAD = 64
RWKV_DIM = 1024
RWKV_W_RANK = 64
RWKV_A_RANK = 64
RWKV_G_RANK = 128
RWKV_W_SCALE = math.exp(-0.5)
RWKV_LN_EPS = 64e-5
RWKV_LANES = 256
RWKV_HPB = RWKV_LANES // RWKV_HEAD
RWKV_SEG = 256

S5_GROUP_CH = 16
S5_GROUPS = 64
S5_STATE = 64
S5_CHUNK = 16
S5_LANE_GROUPS = 8
MOD_ROWS = 8

N_EXPERTS = 32
TOP_K = 4
SWIGLU_LIMIT = 7.0
SWIGLU_ALPHA = 1.702
NORM_EPS = 1e-6

GLA_CONV = 2 * GLA_QK + GLA_V
CONV_CH = GLA_CONV + 3 * RWKV_DIM

OFF_PC = 0
OFF_OG = CONV_CH
OFF_U = OFF_OG + GLA_V
OFF_GATES = OFF_U + D_MODEL
OFF_GG = OFF_GATES + 3 * D_MODEL
OFF_RG = OFF_GG + 128
OFF_RW = OFF_RG + 128
OFF_RA = OFF_RW + 128
NP = OFF_RA + 128

VMEM_LIMIT = 56 * 1024 * 1024


def _cparams(sem):
    return pltpu.CompilerParams(dimension_semantics=sem, vmem_limit_bytes=VMEM_LIMIT)


def _row_tile(t, cap):
    best = 8
    for d in range(8, min(t, cap) + 1, 8):
        if t % d == 0:
            best = d
    return best


def _silu(x):
    return x * jax.nn.sigmoid(x)


def _log_sigmoid(z):
    return jnp.minimum(z, 0.0) - jnp.log(1.0 + jnp.exp(-jnp.abs(z)))


def _tri_matmul(keep, x):
    n = x.shape[1]
    h1 = x.astype(BF16)
    r1 = x - h1.astype(F32)
    h2 = r1.astype(BF16)
    h3 = (r1 - h2.astype(F32)).astype(BF16)
    y = jnp.dot(keep.astype(BF16), jnp.concatenate([h1, h2, h3], axis=1), preferred_element_type=F32)
    return y[:, 0:n] + y[:, n:2 * n] + y[:, 2 * n:3 * n]


def _row_to_col(row):
    return jnp.transpose(jnp.broadcast_to(row, (128, row.shape[1])))[:, 0:1]


def _mod_rows(mod_ref, b, nb, off, is_ctx):
    vb = mod_ref[pl.ds(b, 1), off:off + D_MODEL]
    vc = mod_ref[nb:nb + 1, off:off + D_MODEL]
    return jnp.where(is_ctx, vc, vb)


def _ada_kernel(c_ref, w_ref, b_ref, o_ref):
    a = _silu(c_ref[...])
    o_ref[0] = jnp.dot(a, w_ref[0], precision=HI, preferred_element_type=F32) + b_ref[0]


def _ada_table(cc, w_ada, b_ada):
    depth, d, n = w_ada.shape
    tn = 1536
    return pl.pallas_call(
        _ada_kernel,
        out_shape=jax.ShapeDtypeStruct((depth, cc.shape[0], n), F32),
        grid=(depth, n // tn),
        in_specs=[
            pl.BlockSpec((cc.shape[0], d), lambda l, j: (0, 0)),
            pl.BlockSpec((1, d, tn), lambda l, j: (l, 0, j)),
            pl.BlockSpec((1, 1, tn), lambda l, j: (l, 0, j)),
        ],
        out_specs=pl.BlockSpec((1, cc.shape[0], tn), lambda l, j: (l, 0, j)),
        compiler_params=_cparams(("arbitrary", "arbitrary")),
        name="ada_table",
    )(cc, w_ada, b_ada.reshape(depth, 1, n))


def _norm_mm_kernel(x_ref, nw_ref, mod_ref, w_ref, o_ref, h_ref, *, tm, n_ctx, nb):
    b = pl.program_id(0)
    i = pl.program_id(1)

    @pl.when(pl.program_id(2) == 0)
    def _():
        x = x_ref[0]
        xn = x * lax.rsqrt(jnp.mean(x * x, axis=-1, keepdims=True) + NORM_EPS) * nw_ref[...]
        row = i * tm + lax.broadcasted_iota(jnp.int32, (tm, 1), 0)
        is_ctx = row < n_ctx
        sh = _mod_rows(mod_ref, b, nb, 0, is_ctx)
        sc = _mod_rows(mod_ref, b, nb, D_MODEL, is_ctx)
        h_ref[...] = (xn * (1.0 + sc) + sh).astype(BF16)

    o_ref[0] = jnp.dot(h_ref[...], w_ref[...], preferred_element_type=F32)


def _norm_in_proj(resid, norm_w, mod, w_in_p, n_ctx):
    nb, t, d = resid.shape
    tm = _row_tile(t, 1088)
    tn = 1536
    kern = functools.partial(_norm_mm_kernel, tm=tm, n_ctx=n_ctx, nb=nb)
    return pl.pallas_call(
        kern,
        out_shape=jax.ShapeDtypeStruct((nb, t, NP), F32),
        grid=(nb, t // tm, NP // tn),
        in_specs=[
            pl.BlockSpec((1, tm, d), lambda b, i, j: (b, i, 0)),
            pl.BlockSpec((1, d), lambda b, i, j: (0, 0)),
            pl.BlockSpec(mod.shape, lambda b, i, j: (0, 0)),
            pl.BlockSpec((d, tn), lambda b, i, j: (0, j)),
        ],
        out_specs=pl.BlockSpec((1, tm, tn), lambda b, i, j: (b, i, j)),
        scratch_shapes=[pltpu.VMEM((tm, d), BF16)],
        compiler_params=_cparams(("arbitrary", "arbitrary", "arbitrary")),
        name="norm_in_proj",
    )(resid, norm_w.reshape(1, d), mod, w_in_p)


def _conv_kernel(x_ref, w_ref, o_ref, xp_ref, xl_ref, xr_ref, *, n_ctx, rows):
    tc = x_ref.shape[-1]
    gw = GRID_W

    def wrow(k):
        return w_ref[k:k + 1, :]

    xc = x_ref[0, 0:n_ctx, :]
    ci = lax.broadcasted_iota(jnp.int32, (n_ctx, tc), 0)
    left = jnp.where(ci > 0, pltpu.roll(xc, 1, 0), 0.0)
    right = jnp.where(ci < n_ctx - 1, pltpu.roll(xc, n_ctx - 1, 0), 0.0)
    o_ref[0, 0:n_ctx, :] = left * wrow(3) + xc * wrow(4) + right * wrow(5)

    zeros = jnp.zeros((gw, tc), F32)
    for ref in (xp_ref, xl_ref, xr_ref):
        ref[0:gw, :] = zeros
        ref[(rows + 1) * gw:(rows + 2) * gw, :] = zeros
    wi = lax.broadcasted_iota(jnp.int32, (gw, tc), 0)

    def fill(r, carry):
        base = pl.multiple_of(r * gw, gw)
        xrow = x_ref[0, pl.ds(n_ctx + base, gw), :]
        xp_ref[pl.ds(gw + base, gw), :] = xrow
        xl_ref[pl.ds(gw + base, gw), :] = jnp.where(wi > 0, pltpu.roll(xrow, 1, 0), 0.0)
        xr_ref[pl.ds(gw + base, gw), :] = jnp.where(wi < gw - 1, pltpu.roll(xrow, gw - 1, 0), 0.0)
        return carry

    lax.fori_loop(0, rows, fill, 0)

    def comp(r, carry):
        base = pl.multiple_of(r * gw, gw)
        acc = jnp.zeros((gw, tc), F32)
        for dr in range(3):
            sl = pl.ds(base + gw * dr, gw)
            acc = acc + xl_ref[sl, :] * wrow(3 * dr) + xp_ref[sl, :] * wrow(3 * dr + 1) \
                + xr_ref[sl, :] * wrow(3 * dr + 2)
        o_ref[0, pl.ds(n_ctx + base, gw), :] = acc
        return carry

    lax.fori_loop(0, rows, comp, 0)


def _short_conv(p, conv_w, n_ctx):
    nb, t, _ = p.shape
    rows = (t - n_ctx) // GRID_W
    tc = 256
    kern = functools.partial(_conv_kernel, n_ctx=n_ctx, rows=rows)
    pad_rows = (rows + 2) * GRID_W
    return pl.pallas_call(
        kern,
        out_shape=jax.ShapeDtypeStruct((nb, t, CONV_CH), F32),
        grid=(nb, CONV_CH // tc),
        in_specs=[
            pl.BlockSpec((1, t, tc), lambda b, j: (b, 0, j)),
            pl.BlockSpec((9, tc), lambda b, j: (0, j)),
        ],
        out_specs=pl.BlockSpec((1, t, tc), lambda b, j: (b, 0, j)),
        scratch_shapes=[pltpu.VMEM((pad_rows, tc), F32)] * 3,
        compiler_params=_cparams(("arbitrary", "arbitrary")),
        name="short_conv",
    )(p, conv_w.reshape(9, CONV_CH))


def _gla_kernel(q_ref, k_ref, v_ref, gg_ref, og_ref, wg_ref, bg_ref, ng_ref, o_ref, ob_ref, *, n_ctx, t):
    c = CHUNK
    nch = t // c
    nctx = n_ctx // c
    ii = lax.broadcasted_iota(jnp.int32, (c, c), 0)
    jj = lax.broadcasted_iota(jnp.int32, (c, c), 1)
    scale = GLA_DK ** -0.5

    nt = (((1,), (1,)), ((), ()))
    tn = (((0,), (0,)), ((), ()))
    dot = functools.partial(jnp.dot, preferred_element_type=F32)
    grp = GLA_GROUP
    assert nctx % grp == 0 and nch % grp == 0

    def group(g, carry):
        chains = []
        for d in range(2):
            for u in range(grp):
                i = g * grp + u
                ci = i if d == 0 else jnp.where(i < nctx, nctx - 1 - i, nch + nctx - 1 - i)
                chains.append((d, pl.ds(pl.multiple_of(ci * c, c), c)))
        keeps = [(ii >= jj) if d == 0 else (ii <= jj) for d, _ in chains]
        q = [_silu(q_ref[0, r, :]) * scale for _, r in chains]
        k = [_silu(k_ref[0, r, :]) for _, r in chains]
        vb = [_silu(v_ref[0, r, :]).astype(BF16) for _, r in chains]
        z = [dot(gg_ref[0, r, :][:, GLA_GATE_RANK * d:GLA_GATE_RANK * (d + 1)], wg_ref[d]) + bg_ref[d]
             for d, r in chains]
        la = [_log_sigmoid(x) * (1.0 / GLA_GATE_NORM) for x in z]
        bcum = [_tri_matmul(kp, x) for kp, x in zip(keeps, la)]
        blast = [b[c - 1:c, :] if d == 0 else b[0:1, :] for (d, _), b in zip(chains, bcum)]
        q_dec = [(x * jnp.exp(b)).astype(BF16) for x, b in zip(q, bcum)]
        k_inv = [(x * jnp.exp(-b)).astype(BF16) for x, b in zip(k, bcum)]
        k_end = [(x * jnp.exp(bl - b)).astype(BF16) for x, b, bl in zip(k, bcum, blast)]
        sc = [lax.dot_general(a, b, nt, preferred_element_type=F32) for a, b in zip(q_dec, k_inv)]
        sc = [jnp.where(kp, x, 0.0).astype(BF16) for kp, x in zip(keeps, sc)]
        intra = [dot(a, b) for a, b in zip(sc, vb)]
        kv = [lax.dot_general(a, b, tn, preferred_element_type=F32) for a, b in zip(k_end, vb)]
        dec = [_row_to_col(jnp.exp(bl)) for bl in blast]
        states = list(carry)
        for n, (d, r) in enumerate(chains):
            s = states[d]
            o = intra[n] + dot(q_dec[n], s.astype(BF16))
            if d == 0:
                o_ref[0, r, :] = o
            else:
                ob_ref[r, :] = o
            states[d] = dec[n] * s + kv[n]
        return tuple(states)

    s0 = jnp.zeros((GLA_DK, GLA_DV), F32)
    lax.fori_loop(0, nch // grp, group, (s0, s0))

    def finish(ci, carry):
        rows = pl.ds(pl.multiple_of(ci * c, c), c)
        o = o_ref[0, rows, :] + ob_ref[rows, :]
        o = o * lax.rsqrt(jnp.mean(o * o, axis=-1, keepdims=True) + 1e-5) * ng_ref[...]
        o_ref[0, rows, :] = o * _silu(og_ref[0, rows, :])
        return carry

    lax.fori_loop(0, nch, finish, 0, unroll=2)


def _gla_mixer(pcv, p, wg2, bg, norm_g, n_ctx):
    nb, t, _ = pcv.shape
    kern = functools.partial(_gla_kernel, n_ctx=n_ctx, t=t)
    return pl.pallas_call(
        kern,
        out_shape=jax.ShapeDtypeStruct((nb, t, GLA_V), F32),
        grid=(nb, GLA_HEADS),
        in_specs=[
            pl.BlockSpec((1, t, GLA_DK), lambda b, h: (b, 0, h)),
            pl.BlockSpec((1, t, GLA_DK), lambda b, h: (b, 0, GLA_HEADS + h)),
            pl.BlockSpec((1, t, GLA_DV), lambda b, h: (b, 0, GLA_HEADS + h)),
            pl.BlockSpec((1, t, 128), lambda b, h: (b, 0, OFF_GG // 128)),
            pl.BlockSpec((1, t, GLA_DV), lambda b, h: (b, 0, OFF_OG // GLA_DV + h)),
            pl.BlockSpec((2, GLA_GATE_RANK, GLA_DK), lambda b, h: (0, 0, h)),
            pl.BlockSpec((2, 1, GLA_DK), lambda b, h: (0, 0, h)),
            pl.BlockSpec((1, GLA_DV), lambda b, h: (0, 0)),
        ],
        out_specs=pl.BlockSpec((1, t, GLA_DV), lambda b, h: (b, 0, h)),
        scratch_shapes=[pltpu.VMEM((t, GLA_DV), F32)],
        compiler_params=_cparams(("arbitrary", "arbitrary")),
        name="gla_mixer",
    )(pcv, pcv, pcv, p, p, wg2, bg.reshape(2, 1, GLA_QK), norm_g.reshape(1, GLA_DV))


def _bd_mask():
    shift = RWKV_HEAD.bit_length() - 1
    r = lax.broadcasted_iota(jnp.int32, (RWKV_LANES, RWKV_LANES), 0) >> shift
    c = lax.broadcasted_iota(jnp.int32, (RWKV_LANES, RWKV_LANES), 1) >> shift
    return r == c


def _bd(x, mask):
    xb = x.astype(BF16)
    return jnp.where(mask, jnp.concatenate([xb] * RWKV_HPB, axis=0), jnp.zeros((), BF16))


def _head_sum(x, ones_bd):
    hi = x.astype(BF16)
    lo = (x - hi.astype(F32)).astype(BF16)
    return jnp.dot(hi, ones_bd, preferred_element_type=F32) + jnp.dot(lo, ones_bd, preferred_element_type=F32)


def _rwkv_group_kernel(rf_ref, kf_ref, vf_ref, rwf_ref, raf_ref, rb_ref, kb_ref, vb_ref, rwb_ref, rab_ref,
                       w0_ref, w2_ref, a0_ref, a2_ref, kkp_ref, kap_ref, yf_ref, yb_ref, stf_ref, stb_ref):
    c = CHUNK
    nchunk = RWKV_SEG // c
    mask_bd = _bd_mask()
    ones_bd = mask_bd.astype(BF16)
    dot = functools.partial(jnp.dot, preferred_element_type=F32)
    nt = (((1,), (1,)), ((), ()))
    tn = (((0,), (0,)), ((), ()))

    @pl.when(pl.program_id(2) == 0)
    def _():
        stf_ref[...] = jnp.zeros_like(stf_ref)
        stb_ref[...] = jnp.zeros_like(stb_ref)

    chains = [(0, slice(u * c, (u + 1) * c)) for u in range(nchunk)] \
        + [(1, slice(u * c, (u + 1) * c)) for u in range(nchunk - 1, -1, -1)]
    refs = ((rf_ref, kf_ref, vf_ref, rwf_ref, raf_ref), (rb_ref, kb_ref, vb_ref, rwb_ref, rab_ref))

    def each(f, *lists):
        return [f(*xs) for xs in zip(*lists)]

    ds = [d for d, _ in chains]
    r = [refs[d][0][0, rows, :] for d, rows in chains]
    k = [refs[d][1][0, rows, :] for d, rows in chains]
    v = [refs[d][2][0, rows, :] for d, rows in chains]
    pw = [refs[d][3][0, rows, :][:, RWKV_W_RANK * d:RWKV_W_RANK * (d + 1)] for d, rows in chains]
    pa = [refs[d][4][0, rows, :][:, RWKV_A_RANK * d:RWKV_A_RANK * (d + 1)] for d, rows in chains]
    lw = each(lambda d, x: w0_ref[d] + dot(jnp.tanh(x), w2_ref[d]), ds, pw)
    logw = [-RWKV_W_SCALE * jax.nn.sigmoid(x) for x in lw]
    la = each(lambda d, x: a0_ref[d] + dot(x, a2_ref[d]), ds, pa)
    a = [jax.nn.sigmoid(x) for x in la]
    kkv = [x * kkp_ref[...] for x in k]
    ss = [_head_sum(x * x, ones_bd) for x in kkv]
    kk = each(lambda x, s: x / jnp.maximum(jnp.sqrt(s), 1e-12), kkv, ss)
    kd = each(lambda x, y: x * (1.0 + (y - 1.0) * kap_ref[...]), k, a)
    kka = each(lambda x, y: x * y, kk, a)

    ii = lax.broadcasted_iota(jnp.int32, (c, c), 0)
    jj = lax.broadcasted_iota(jnp.int32, (c, c), 1)
    ti = lax.broadcasted_iota(jnp.int32, (c, RWKV_LANES), 0)
    si = lax.broadcasted_iota(jnp.int32, (c, RWKV_LANES), 1) & (c - 1)
    keep_sq = [(ii >= jj) if d == 0 else (ii <= jj) for d in ds]
    incl = [(si <= ti) if d == 0 else (si >= ti) for d in ds]
    strict = [(si < ti) if d == 0 else (si > ti) for d in ds]
    eye = (si == ti).astype(F32)

    cb = each(_tri_matmul, keep_sq, logw)
    cblast = each(lambda d, x: x[c - 1:c, :] if d == 0 else x[0:1, :], ds, cb)
    ad = each(lambda x, b, w: -x * jnp.exp(b - w), kk, cb, logw)
    rd = each(lambda x, b: x * jnp.exp(b), r, cb)
    einv = [jnp.exp(-b) for b in cb]
    bi = each(lambda x, e: x * e, kka, einv)
    ki = each(lambda x, e: x * e, kd, einv)
    eend = each(lambda bl, b: jnp.exp(bl - b), cblast, cb)
    bend = each(lambda x, e: x * e, kka, eend)
    kend = each(lambda x, e: x * e, kd, eend)

    def bd(x):
        return _bd(x, mask_bd)

    def mm(x, y_bd):
        return dot(x.astype(BF16), y_bd)

    lhs = each(lambda x, y: jnp.concatenate([x, y], axis=0).astype(BF16), ad, rd)
    pb = each(lambda x, y: lax.dot_general(x, bd(y), nt, preferred_element_type=F32), lhs, bi)
    pk = each(lambda x, y: lax.dot_general(x, bd(y), nt, preferred_element_type=F32), lhs, ki)
    a_ab = each(lambda m, x: jnp.where(m, x[0:c], 0.0), strict, pb)
    r_ab = each(lambda m, x: jnp.where(m, x[c:2 * c], 0.0), incl, pb)
    a_ak = each(lambda m, x: jnp.where(m, x[0:c], 0.0), strict, pk)
    r_ak = each(lambda m, x: jnp.where(m, x[c:2 * c], 0.0), incl, pk)

    tinv = [eye + x for x in a_ab]
    pwr = each(lambda x: mm(x, bd(x)), a_ab)
    for lvl in range(5):
        if lvl < 4:
            both = each(lambda t_, p_: mm(jnp.concatenate([t_, p_], axis=0), bd(p_)), tinv, pwr)
            tinv = each(lambda t_, b: t_ + b[0:c], tinv, both)
            pwr = [b[c:2 * c] for b in both]
        else:
            tinv = each(lambda t_, p_: t_ + mm(t_, bd(p_)), tinv, pwr)

    ln = RWKV_LANES
    akv = each(lambda x, y, v_: mm(jnp.concatenate([x, y], axis=0), bd(v_)), a_ak, r_ak, v)
    mu = each(lambda t_, x, av: mm(t_, jnp.concatenate([bd(x), bd(av[0:c])], axis=1)), tinv, ad, akv)
    m1 = [x[:, 0:ln] for x in mu]
    u0 = [x[:, ln:2 * ln] for x in mu]
    ry = each(lambda rab, m, u_: mm(rab, jnp.concatenate([bd(m), bd(u_)], axis=1)), r_ab, m1, u0)
    r1 = each(lambda x, z: (x + z[:, 0:ln]).astype(BF16), rd, ry)
    y0 = each(lambda z, av: z[:, ln:2 * ln] + av[c:2 * c], ry, akv)
    g1 = each(lambda x, m: jnp.where(mask_bd, lax.dot_general(x.astype(BF16), m.astype(BF16), tn,
                                                              preferred_element_type=F32), 0.0).astype(BF16),
              bend, m1)
    h = each(lambda x, y, u_, v_: jnp.where(mask_bd, lax.dot_general(
        jnp.concatenate([x, y], axis=0).astype(BF16), jnp.concatenate([u_, v_], axis=0).astype(BF16), tn,
        preferred_element_type=F32), 0.0), bend, kend, u0, v)
    dec = [_row_to_col(jnp.exp(x)) for x in cblast]

    states = [stf_ref[...], stb_ref[...]]
    outs = (yf_ref, yb_ref)
    for n, (d, rows) in enumerate(chains):
        st = states[d]
        stb = st.astype(BF16)
        outs[d][0, rows, :] = dot(r1[n], stb) + y0[n]
        states[d] = dec[n] * st + dot(g1[n], stb) + h[n]
    stf_ref[...] = states[0]
    stb_ref[...] = states[1]


def _rwkv_post_kernel(r_ref, k_ref, v_ref, ra_ref, rg_ref, yf_ref, yb_ref, a0_ref, a2_ref, g2_ref,
                      kap_ref, rkp_ref, lng_ref, lnb_ref, o_ref):
    ones_bd = _bd_mask().astype(BF16)
    r = r_ref[0]
    k = k_ref[0]
    ra = ra_ref[0]
    y = yf_ref[0] + yb_ref[0]
    inv_n = 1.0 / RWKV_HEAD
    mu = _head_sum(y, ones_bd) * inv_n
    yc = y - mu
    var = _head_sum(yc * yc, ones_bd) * inv_n
    yn = yc * lax.rsqrt(var + RWKV_LN_EPS) * lng_ref[...] + lnb_ref[...]
    kds = jnp.zeros_like(k)
    for d in range(2):
        la = a0_ref[d] + jnp.dot(ra[:, RWKV_A_RANK * d:RWKV_A_RANK * (d + 1)], a2_ref[d],
                                 preferred_element_type=F32)
        kds = kds + k * (1.0 + (jax.nn.sigmoid(la) - 1.0) * kap_ref[...])
    bonus = _head_sum(r * kds * rkp_ref[...], ones_bd) * v_ref[0]
    g = jnp.dot(jax.nn.sigmoid(rg_ref[0]), g2_ref[...], preferred_element_type=F32)
    o_ref[0] = (yn + bonus) * g


def _rwkv_mixer(pcv, p, prm, n_ctx):
    nb, t, _ = pcv.shape
    seg = RWKV_SEG
    nseg = t // seg
    nseg_ctx = n_ctx // seg
    ln = RWKV_LANES
    nhb = RWKV_DIM // ln
    cb0 = GLA_CONV // ln

    def fmap(s):
        return s

    def bmap(s):
        return jnp.where(s < nseg_ctx, nseg_ctx - 1 - s, nseg - 1 - (s - nseg_ctx))

    def tok(off_blocks, smap):
        return pl.BlockSpec((1, seg, ln), lambda b, h, s: (b, smap(s), off_blocks + h))

    def small(off, smap):
        return pl.BlockSpec((1, seg, 128), lambda b, h, s: (b, smap(s), off // 128))

    def dirspecs(smap):
        return [tok(cb0, smap), tok(cb0 + nhb, smap), tok(cb0 + 2 * nhb, smap),
                small(OFF_RW, smap), small(OFF_RA, smap)]

    def vec():
        return pl.BlockSpec((1, ln), lambda b, h, s: (0, h))

    w0 = prm["w0"].reshape(2, 1, RWKV_DIM)
    a0 = prm["a0"].reshape(2, 1, RWKV_DIM)
    kkp = prm["kk"].reshape(1, RWKV_DIM)
    kap = prm["ka"].reshape(1, RWKV_DIM)
    yf, yb = pl.pallas_call(
        _rwkv_group_kernel,
        out_shape=(jax.ShapeDtypeStruct((nb, t, RWKV_DIM), F32),) * 2,
        grid=(nb, nhb, nseg),
        in_specs=dirspecs(fmap) + dirspecs(bmap) + [
            pl.BlockSpec((2, 1, ln), lambda b, h, s: (0, 0, h)),
            pl.BlockSpec((2, RWKV_W_RANK, ln), lambda b, h, s: (0, 0, h)),
            pl.BlockSpec((2, 1, ln), lambda b, h, s: (0, 0, h)),
            pl.BlockSpec((2, RWKV_A_RANK, ln), lambda b, h, s: (0, 0, h)),
            vec(), vec()],
        out_specs=(pl.BlockSpec((1, seg, ln), lambda b, h, s: (b, fmap(s), h)),
                   pl.BlockSpec((1, seg, ln), lambda b, h, s: (b, bmap(s), h))),
        scratch_shapes=[pltpu.VMEM((ln, ln), F32)] * 2,
        compiler_params=_cparams(("arbitrary", "arbitrary", "arbitrary")),
        name="rwkv7_scan",
    )(pcv, pcv, pcv, p, p, pcv, pcv, pcv, p, p, w0, prm["w2"], a0, prm["a2"], kkp, kap)

    tm = _row_tile(t, 1088)

    def ptok(off_blocks):
        return pl.BlockSpec((1, tm, ln), lambda b, i, h: (b, i, off_blocks + h))

    def psmall(off):
        return pl.BlockSpec((1, tm, 128), lambda b, i, h: (b, i, off // 128))

    def pvec():
        return pl.BlockSpec((1, ln), lambda b, i, h: (0, h))

    return pl.pallas_call(
        _rwkv_post_kernel,
        out_shape=jax.ShapeDtypeStruct((nb, t, RWKV_DIM), F32),
        grid=(nb, t // tm, nhb),
        in_specs=[ptok(cb0), ptok(cb0 + nhb), ptok(cb0 + 2 * nhb), psmall(OFF_RA), psmall(OFF_RG),
                  ptok(0), ptok(0),
                  pl.BlockSpec((2, 1, ln), lambda b, i, h: (0, 0, h)),
                  pl.BlockSpec((2, RWKV_A_RANK, ln), lambda b, i, h: (0, 0, h)),
                  pl.BlockSpec((RWKV_G_RANK, ln), lambda b, i, h: (0, h)),
                  pvec(), pvec(), pvec(), pvec()],
        out_specs=ptok(0),
        compiler_params=_cparams(("arbitrary", "arbitrary", "arbitrary")),
        name="rwkv7_post",
    )(pcv, pcv, pcv, p, p, yf, yb, a0, prm["a2"], prm["g2"], kap, prm["rk"].reshape(1, RWKV_DIM),
      prm["ln_g"].reshape(1, RWKV_DIM), prm["ln_b"].reshape(1, RWKV_DIM))


def _s5_operators(lam_re, lam_im, log_dt, b_re, b_im, c_re, c_im):
    g, p_, lc = S5_GROUPS, S5_STATE, S5_CHUNK
    lr, li = lam_re.astype(F32), lam_im.astype(F32)
    br, bi = b_re.astype(F32), b_im.astype(F32)
    den = lr * lr + li * li
    n = jnp.arange(lc + 1, dtype=F32)[:, None, None]
    ein = functools.partial(jnp.einsum, precision=HI)
    out = []
    for d in range(2):
        dt = jnp.exp(log_dt[d].astype(F32))[:, None]
        mag = jnp.exp(n * (lr * dt)[None])
        ang = n * (li * dt)[None]
        pr, pi = mag * jnp.cos(ang), mag * jnp.sin(ang)
        m1 = jnp.exp(lr * dt)
        ar, ai = m1 * jnp.cos(li * dt), m1 * jnp.sin(li * dt)
        fr = ((ar - 1.0) * lr + ai * li) / den
        fi = (ai * lr - (ar - 1.0) * li) / den
        bb_r = fr[..., None] * br - fi[..., None] * bi
        bb_i = fr[..., None] * bi + fi[..., None] * br
        cr, ci = c_re[d].astype(F32), c_im[d].astype(F32)
        ca_r = cr[None] * pr[:, :, None, :] - ci[None] * pi[:, :, None, :]
        ca_i = cr[None] * pi[:, :, None, :] + ci[None] * pr[:, :, None, :]
        kt = ein('tgop,gpc->tgoc', ca_r[:lc], bb_r) - ein('tgop,gpc->tgoc', ca_i[:lc], bb_i)
        nlg, gl, ch = S5_LANE_GROUPS, S5_GROUPS // S5_LANE_GROUPS, S5_GROUP_CH
        ktc = kt.reshape(lc, nlg, gl, ch, ch).transpose(1, 0, 4, 2, 3).reshape(nlg, lc, ch, gl * ch)
        zer = jnp.zeros((nlg, lc - 1, ch, gl * ch), F32)
        lagc = jnp.concatenate([zer, ktc], axis=1) if d == 0 else jnp.concatenate([ktc[:, ::-1], zer], axis=1)

        e_s = (lc - 1 - jnp.arange(lc)) if d == 0 else jnp.arange(lc)
        ps_r, ps_i = pr[e_s], pi[e_s]
        bst_r = ps_r[..., None] * bb_r[None] - ps_i[..., None] * bb_i[None]
        bst_i = ps_r[..., None] * bb_i[None] + ps_i[..., None] * bb_r[None]

        def rows_op(x):
            return x.reshape(lc, nlg, gl, p_, ch).transpose(1, 0, 4, 2, 3).reshape(nlg, lc, ch, gl * p_)

        f_t = (jnp.arange(lc) + 1) if d == 0 else (lc - jnp.arange(lc))

        def cols_op(x):
            return x.reshape(lc, nlg, gl, ch, p_).transpose(1, 0, 4, 2, 3).reshape(nlg, lc, p_, gl * ch)

        out.append(dict(
            lag=lagc, bst_r=rows_op(bst_r), bst_i=rows_op(bst_i),
            co_r=cols_op(ca_r[f_t]), co_i=cols_op(-ca_i[f_t]),
            a_r=pr[lc].reshape(nlg, 1, gl * p_), a_i=pi[lc].reshape(nlg, 1, gl * p_)))
    return {k: jnp.stack([out[0][k], out[1][k]]) for k in out[0]}


def _s5_kernel(u_ref, lag_ref, br_ref, bi_ref, cr_ref, ci_ref, ar_ref, ai_ref, o_ref,
               ir_ref, ii_ref, xr_ref, xi_ref, op_ref, bsr_ref, bsi_ref, cor_ref, coi_ref, *, nchunk, nchunk_ctx):
    d = pl.program_id(1)
    lc = S5_CHUNK
    gl = S5_GROUPS // S5_LANE_GROUPS
    dot = functools.partial(jnp.dot, preferred_element_type=F32)

    def expand(x):
        w, n = x.shape
        m = n // gl
        rg = lax.broadcasted_iota(jnp.int32, (gl * w, n), 0) >> (w.bit_length() - 1)
        cg = lax.broadcasted_iota(jnp.int32, (gl * w, n), 1) >> (m.bit_length() - 1)
        return jnp.where(rg == cg, jnp.concatenate([x] * gl, axis=0), 0.0).astype(BF16)

    @pl.when(pl.program_id(2) == 0)
    def _():
        lags = [expand(lag_ref[0, 0, j]) for j in range(2 * lc - 1)]
        for t in range(lc):
            cor_ref[:, t * 128:(t + 1) * 128] = expand(cr_ref[0, 0, t])
            coi_ref[:, t * 128:(t + 1) * 128] = expand(ci_ref[0, 0, t])
            bsr_ref[t * 128:(t + 1) * 128, :] = expand(br_ref[0, 0, t])
            bsi_ref[t * 128:(t + 1) * 128, :] = expand(bi_ref[0, 0, t])
            for s in range(lc):
                op_ref[s * 128:(s + 1) * 128, t * 128:(t + 1) * 128] = lags[t - s + lc - 1]

    lhs = jnp.concatenate([u_ref[0, pl.ds(s, nchunk, stride=lc), :] for s in range(lc)], axis=1).astype(BF16)
    ir_ref[...] = dot(lhs, bsr_ref[...])
    ii_ref[...] = dot(lhs, bsi_ref[...])
    ar = ar_ref[0, 0]
    ai = ai_ref[0, 0]

    def step(i, carry):
        sr, si = carry
        rev = jnp.where(i < nchunk_ctx, nchunk_ctx - 1 - i, nchunk + nchunk_ctx - 1 - i)
        ci = jnp.where(d == 0, i, rev)
        row = pl.ds(ci, 1)
        xr_ref[row, :] = sr
        xi_ref[row, :] = si
        nr = ar * sr - ai * si + ir_ref[row, :]
        ni = ar * si + ai * sr + ii_ref[row, :]
        return nr, ni

    z = jnp.zeros((1, ir_ref.shape[1]), F32)
    lax.fori_loop(0, nchunk, step, (z, z), unroll=8)
    y = dot(lhs, op_ref[...]) + dot(xr_ref[...].astype(BF16), cor_ref[...]) \
        + dot(xi_ref[...].astype(BF16), coi_ref[...])
    for t in range(lc):
        o_ref[0, 0, pl.ds(t, nchunk, stride=lc), :] = y[:, t * 128:(t + 1) * 128]


def _s5_mixer(p, ops, n_ctx):
    nb, t, _ = p.shape
    lc = S5_CHUNK
    nchunk = t // lc
    nlg = S5_LANE_GROUPS
    ns = ops["a_r"].shape[-1]

    def wspec(x):
        nd = x.ndim - 2
        return pl.BlockSpec((1, 1) + x.shape[2:], lambda g, d, b: (d, g) + (0,) * nd)

    keys = ("lag", "bst_r", "bst_i", "co_r", "co_i", "a_r", "a_i")
    nop = lc * 128
    kern = functools.partial(_s5_kernel, nchunk=nchunk, nchunk_ctx=n_ctx // lc)
    return pl.pallas_call(
        kern,
        out_shape=jax.ShapeDtypeStruct((2, nb, t, D_MODEL), F32),
        grid=(nlg, 2, nb),
        in_specs=[pl.BlockSpec((1, t, 128), lambda g, d, b: (b, 0, OFF_U // 128 + g))]
        + [wspec(ops[k]) for k in keys],
        out_specs=pl.BlockSpec((1, 1, t, 128), lambda g, d, b: (d, b, 0, g)),
        scratch_shapes=[pltpu.VMEM((nchunk, ns), F32)] * 4
        + [pltpu.VMEM((nop, nop), BF16), pltpu.VMEM((nop, ns), BF16), pltpu.VMEM((nop, ns), BF16),
           pltpu.VMEM((ns, nop), BF16), pltpu.VMEM((ns, nop), BF16)],
        compiler_params=_cparams(("arbitrary", "arbitrary", "arbitrary")),
        name="s5_mixer",
    )(p, *[ops[k] for k in keys])


def _merge_kernel(ygla_ref, yrw_ref, ys5f_ref, ys5b_ref, u_ref, ga_ref, gb_ref, gc_ref, x_ref, mod_ref, wb_ref, wo_ref,
                  d_ref, gw_ref, gbias_ref, o_ref, *, tm, n_ctx, nb):
    b = pl.program_id(0)
    i = pl.program_id(1)
    dot = functools.partial(jnp.dot, preferred_element_type=F32)
    z = jax.nn.gelu(ys5f_ref[0, 0] + ys5b_ref[0, 0] + d_ref[...] * u_ref[0])
    zs = z * jax.nn.sigmoid(dot(z.astype(BF16), gw_ref[...]) + gbias_ref[...])
    m = jax.nn.sigmoid(ga_ref[0]) * dot(ygla_ref[0].astype(BF16), wb_ref[0]) \
        + jax.nn.sigmoid(gb_ref[0]) * dot(yrw_ref[0].astype(BF16), wb_ref[1]) \
        + jax.nn.sigmoid(gc_ref[0]) * dot(zs.astype(BF16), wb_ref[2])
    mix = dot(m.astype(BF16), wo_ref[...])
    row = i * tm + lax.broadcasted_iota(jnp.int32, (tm, 1), 0)
    gate = _mod_rows(mod_ref, b, nb, 2 * D_MODEL, row < n_ctx)
    o_ref[0] = x_ref[0] + gate * mix


def _merge(ygla, yrw, ys5, p, resid, mod, w_branch, w_out, s5_d, glu_w, glu_b, n_ctx):
    nb, t, d = resid.shape
    tm = _row_tile(t, 272)
    kern = functools.partial(_merge_kernel, tm=tm, n_ctx=n_ctx, nb=nb)

    def tok():
        return pl.BlockSpec((1, tm, d), lambda b, i: (b, i, 0))

    return pl.pallas_call(
        kern,
        out_shape=jax.ShapeDtypeStruct((nb, t, d), F32),
        grid=(nb, t // tm),
        in_specs=[
            tok(), tok(),
            pl.BlockSpec((1, 1, tm, d), lambda b, i: (0, b, i, 0)),
            pl.BlockSpec((1, 1, tm, d), lambda b, i: (1, b, i, 0)),
            pl.BlockSpec((1, tm, d), lambda b, i: (b, i, OFF_U // d)),
            pl.BlockSpec((1, tm, d), lambda b, i: (b, i, OFF_GATES // d)),
            pl.BlockSpec((1, tm, d), lambda b, i: (b, i, OFF_GATES // d + 1)),
            pl.BlockSpec((1, tm, d), lambda b, i: (b, i, OFF_GATES // d + 2)),
            tok(),
            pl.BlockSpec(mod.shape, lambda b, i: (0, 0)),
            pl.BlockSpec((3, d, d), lambda b, i: (0, 0, 0)),
            pl.BlockSpec((d, d), lambda b, i: (0, 0)),
            pl.BlockSpec((1, d), lambda b, i: (0, 0)),
            pl.BlockSpec((d, d), lambda b, i: (0, 0)),
            pl.BlockSpec((1, d), lambda b, i: (0, 0)),
        ],
        out_specs=tok(),
        compiler_params=_cparams(("arbitrary", "arbitrary")),
        name="merge",
    )(ygla, yrw, ys5, ys5, p, p, p, p, resid, mod, w_branch.astype(BF16), w_out.astype(BF16), s5_d.reshape(1, d),
      glu_w.astype(BF16), glu_b.reshape(1, d))


def _router_kernel(x_ref, nw_ref, mod_ref, rw_ref, rb_ref, h_ref, comb_ref, combt_ref, rank_ref, rankt_ref,
                   offs_ref, *, tm, n_ctx, nb):
    b = pl.program_id(0)
    i = pl.program_id(1)
    x = x_ref[0]
    xn = x * lax.rsqrt(jnp.mean(x * x, axis=-1, keepdims=True) + NORM_EPS) * nw_ref[...]
    row = i * tm + lax.broadcasted_iota(jnp.int32, (tm, 1), 0)
    is_ctx = row < n_ctx
    sh = _mod_rows(mod_ref, b, nb, 3 * D_MODEL, is_ctx)
    sc = _mod_rows(mod_ref, b, nb, 4 * D_MODEL, is_ctx)
    h = xn * (1.0 + sc) + sh
    h_ref[0] = h.astype(BF16)
    logits = jnp.dot(h, rw_ref[...], precision=HI, preferred_element_type=F32) + rb_ref[...]
    lane = lax.broadcasted_iota(jnp.int32, logits.shape, 1).astype(F32)
    neg = jnp.float32(-jnp.inf)
    cur = logits
    comb = jnp.zeros_like(logits)
    denom = jnp.zeros((tm, 1), F32)
    top = None
    for kk in range(TOP_K):
        m = jnp.max(cur, axis=-1, keepdims=True)
        idx = jnp.min(jnp.where(cur == m, lane, 128.0), axis=-1, keepdims=True)
        sel = lane == idx
        if kk == 0:
            top = m
        e = jnp.exp(m - top)
        comb = comb + jnp.where(sel, e, 0.0)
        denom = denom + e
        cur = jnp.where(sel, neg, cur)
    comb = comb / denom
    comb_ref[0] = comb

    chosen = jnp.where(comb > 0.0, 1.0, 0.0).astype(BF16)
    ri = lax.broadcasted_iota(jnp.int32, (tm, tm), 0)
    ci = lax.broadcasted_iota(jnp.int32, (tm, tm), 1)
    rank = jnp.dot((ci < ri).astype(BF16), chosen, preferred_element_type=F32)
    rank_ref[0] = rank
    rankt_ref[0] = lax.dot_general(chosen, (ri < ci).astype(BF16), (((0,), (0,)), ((), ())),
                                   preferred_element_type=F32)
    eye = (lax.broadcasted_iota(jnp.int32, (128, 128), 0)
           == lax.broadcasted_iota(jnp.int32, (128, 128), 1)).astype(BF16)
    h1 = comb.astype(BF16)
    r1 = comb - h1.astype(F32)
    h2 = r1.astype(BF16)
    h3 = (r1 - h2.astype(F32)).astype(BF16)
    nt = (((1,), (1,)), ((), ()))
    combt_ref[0] = lax.dot_general(eye, h1, nt, preferred_element_type=F32) \
        + lax.dot_general(eye, h2, nt, preferred_element_type=F32) \
        + lax.dot_general(eye, h3, nt, preferred_element_type=F32)
    tile = tm // MOE_TILES
    rows = [rank[j * tile:j * tile + 1, :] for j in range(MOE_TILES)]
    rows.append(rank[tm - 1:tm, :] + chosen[tm - 1:tm, :].astype(F32))
    rows.append(jnp.zeros((16 - len(rows), 128), F32))
    offs_ref[0] = jnp.concatenate(rows, axis=0)


def _router(resid, norm_w, mod, router_w, router_b, n_ctx):
    nb, t, d = resid.shape
    tm = _row_tile(t, 1088)
    nper = t // tm
    nq = nb * nper
    rw = jnp.pad(router_w, ((0, 0), (0, 128 - N_EXPERTS)))
    rb = jnp.pad(router_b, (0, 128 - N_EXPERTS), constant_values=-1e30).reshape(1, 128)
    kern = functools.partial(_router_kernel, tm=tm, n_ctx=n_ctx, nb=nb)

    def qspec(r, c):
        return pl.BlockSpec((1, r, c), lambda b, i: (b * nper + i, 0, 0))

    return pl.pallas_call(
        kern,
        out_shape=(jax.ShapeDtypeStruct((nb, t, d), BF16),
                   jax.ShapeDtypeStruct((nq, tm, 128), F32), jax.ShapeDtypeStruct((nq, 128, tm), F32),
                   jax.ShapeDtypeStruct((nq, tm, 128), F32), jax.ShapeDtypeStruct((nq, 128, tm), F32),
                   jax.ShapeDtypeStruct((nq, 16, 128), F32)),
        grid=(nb, nper),
        in_specs=[
            pl.BlockSpec((1, tm, d), lambda b, i: (b, i, 0)),
            pl.BlockSpec((1, d), lambda b, i: (0, 0)),
            pl.BlockSpec(mod.shape, lambda b, i: (0, 0)),
            pl.BlockSpec((d, 128), lambda b, i: (0, 0)),
            pl.BlockSpec((1, 128), lambda b, i: (0, 0)),
        ],
        out_specs=(pl.BlockSpec((1, tm, d), lambda b, i: (b, i, 0)),
                   qspec(tm, 128), qspec(128, tm), qspec(tm, 128), qspec(128, tm), qspec(16, 128)),
        compiler_params=_cparams(("arbitrary", "arbitrary")),
        name="router",
    )(resid, norm_w.reshape(1, d), mod, rw, rb)


MOE_TILES = 8
MOE_FFN_ROWS = 160


def _moe_routed_kernel(off_ref, h_ref, comb_ref, combt_ref, rank_ref, rankt_ref, wg_ref, bg_ref, wu_ref, bu_ref,
                       wd_ref, bd_ref, o_ref, g_ref, y_ref, *, qt):
    q = pl.program_id(0)
    e = pl.program_id(1)
    ne = pl.num_programs(1)
    tile = qt // MOE_TILES
    win = tile + 8
    fb = MOE_FFN_ROWS
    dot = functools.partial(jnp.dot, preferred_element_type=F32)

    @pl.when(e == 0)
    def _():
        o_ref[...] = jnp.zeros_like(o_ref)
        g_ref[...] = jnp.zeros_like(g_ref)
        y_ref[...] = jnp.zeros_like(y_ref)

    base = (q * ne + e) * (MOE_TILES + 1)
    wrow = combt_ref[0, pl.ds(e, 1), :]
    rrow = rankt_ref[0, pl.ds(e, 1), :]
    sub_i = lax.broadcasted_iota(jnp.int32, (win, 1), 0)
    sub_f = lax.broadcasted_iota(jnp.int32, (win, tile), 0).astype(F32)
    lane_f = lax.broadcasted_iota(jnp.int32, (tile, win), 1).astype(F32)
    lane_e = lax.broadcasted_iota(jnp.int32, (tile, 128), 1) == e

    for j in range(MOE_TILES):
        off = off_ref[base + j]
        nxt = off_ref[base + j + 1]
        ws = pl.multiple_of((off // 8) * 8, 8)
        rj = rrow[:, j * tile:(j + 1) * tile] - ws.astype(F32)
        wj = wrow[:, j * tile:(j + 1) * tile]
        onehot = jnp.where((sub_f == rj) & (wj > 0.0), 1.0, 0.0).astype(BF16)
        new = dot(onehot, h_ref[0, j * tile:(j + 1) * tile, :])
        slot = sub_i + ws
        hit = (slot >= off) & (slot < nxt)
        g_ref[pl.ds(ws, win), :] = jnp.where(hit, new, g_ref[pl.ds(ws, win), :])

    n = off_ref[base + MOE_TILES]

    def ffn(i, carry):
        rows = pl.ds(pl.multiple_of(i * fb, 8), fb)
        x = g_ref[rows, :].astype(BF16)
        gl = jnp.minimum(dot(x, wg_ref[0]) + bg_ref[0], SWIGLU_LIMIT)
        li = jnp.clip(dot(x, wu_ref[0]) + bu_ref[0], -SWIGLU_LIMIT, SWIGLU_LIMIT)
        act = gl * jax.nn.sigmoid(SWIGLU_ALPHA * gl) * (li + 1.0)
        y_ref[rows, :] = dot(act.astype(BF16), wd_ref[0]) + bd_ref[0]
        return carry

    lax.fori_loop(0, (n + fb - 1) // fb, ffn, 0)

    for j in range(MOE_TILES):
        off = off_ref[base + j]
        ws = pl.multiple_of((off // 8) * 8, 8)
        rows = slice(j * tile, (j + 1) * tile)
        wcol = jnp.sum(jnp.where(lane_e, comb_ref[0, rows, :], 0.0), axis=-1, keepdims=True)
        rcol = jnp.sum(jnp.where(lane_e, rank_ref[0, rows, :], 0.0), axis=-1, keepdims=True)
        onehot = jnp.where((lane_f == rcol - ws.astype(F32)) & (wcol > 0.0), 1.0, 0.0).astype(BF16)
        contrib = dot(onehot, y_ref[pl.ds(ws, win), :].astype(BF16))
        o_ref[0, rows, :] += wcol * contrib


def _moe_routed(h, combq, combt, rank, rankt, offs, wg, bg, wu, bu, wd, bd):
    nb, t, d = h.shape
    nq, qt, _ = combq.shape
    assert qt % (8 * MOE_TILES) == 0
    ne = N_EXPERTS
    off = offs[:, :MOE_TILES + 1, :ne].astype(jnp.int32).transpose(0, 2, 1).reshape(-1)
    gr = qt + MOE_FFN_ROWS + 16
    kern = functools.partial(_moe_routed_kernel, qt=qt)

    def wspec():
        return pl.BlockSpec((1, d, d), lambda q, e, off: (e, 0, 0))

    def bspec():
        return pl.BlockSpec((1, 1, d), lambda q, e, off: (e, 0, 0))

    def tokspec(w):
        return pl.BlockSpec((1, qt, w), lambda q, e, off: (q, 0, 0))

    def tspec():
        return pl.BlockSpec((1, 128, qt), lambda q, e, off: (q, 0, 0))

    out = pl.pallas_call(
        kern,
        out_shape=jax.ShapeDtypeStruct((nq, qt, d), F32),
        grid_spec=pltpu.PrefetchScalarGridSpec(
            num_scalar_prefetch=1,
            grid=(nq, ne),
            in_specs=[tokspec(d), tokspec(128), tspec(), tokspec(128), tspec(),
                      wspec(), bspec(), wspec(), bspec(), wspec(), bspec()],
            out_specs=tokspec(d),
            scratch_shapes=[pltpu.VMEM((gr, d), F32), pltpu.VMEM((gr, d), F32)]),
        compiler_params=_cparams(("arbitrary", "arbitrary")),
        name="moe_routed",
    )(off, h.reshape(nq, qt, d), combq, combt, rank, rankt, wg.astype(BF16), bg.reshape(ne, 1, d),
      wu.astype(BF16), bu.reshape(ne, 1, d), wd.astype(BF16), bd.reshape(ne, 1, d))
    return out.reshape(nb, t, d)


def _resid_gate_kernel(x_ref, y_ref, mod_ref, o_ref, *, tm, n_ctx, nb, off):
    b = pl.program_id(0)
    i = pl.program_id(1)
    row = i * tm + lax.broadcasted_iota(jnp.int32, (tm, 1), 0)
    gate = _mod_rows(mod_ref, b, nb, off, row < n_ctx)
    o_ref[0] = x_ref[0] + gate * y_ref[0]


def _resid_gate(resid, y, mod, off, n_ctx):
    nb, t, d = resid.shape
    tm = _row_tile(t, 1088)
    kern = functools.partial(_resid_gate_kernel, tm=tm, n_ctx=n_ctx, nb=nb, off=off)

    def tok():
        return pl.BlockSpec((1, tm, d), lambda b, i: (b, i, 0))

    return pl.pallas_call(
        kern,
        out_shape=jax.ShapeDtypeStruct((nb, t, d), F32),
        grid=(nb, t // tm),
        in_specs=[tok(), tok(), pl.BlockSpec(mod.shape, lambda b, i: (0, 0))],
        out_specs=tok(),
        compiler_params=_cparams(("arbitrary", "arbitrary")),
        name="resid_gate",
    )(resid, y, mod)


def _final_norm_kernel(x_ref, w_ref, o_ref):
    x = x_ref[0]
    o_ref[0] = x * lax.rsqrt(jnp.mean(x * x, axis=-1, keepdims=True) + NORM_EPS) * w_ref[...]


def _final_norm(resid, w, n_ctx):
    nb, t, d = resid.shape
    tm = n_ctx
    return pl.pallas_call(
        _final_norm_kernel,
        out_shape=jax.ShapeDtypeStruct((nb, t - n_ctx, d), F32),
        grid=(nb, (t - n_ctx) // tm),
        in_specs=[pl.BlockSpec((1, tm, d), lambda b, i: (b, i + 1, 0)),
                  pl.BlockSpec((1, d), lambda b, i: (0, 0))],
        out_specs=pl.BlockSpec((1, tm, d), lambda b, i: (b, i, 0)),
        compiler_params=_cparams(("arbitrary", "arbitrary")),
        name="final_norm",
    )(resid, w.reshape(1, d))


def _pack_w_in(w):
    o_gg = CONV_CH + GLA_V
    o_rg = o_gg + 2 * GLA_GATE_RANK
    o_rw = o_rg + RWKV_G_RANK
    o_ra = o_rw + 2 * RWKV_W_RANK
    o_u = o_ra + 2 * RWKV_A_RANK
    o_gt = o_u + D_MODEL
    pad = jnp.zeros((w.shape[0], 128 - 2 * GLA_GATE_RANK), w.dtype)
    return jnp.concatenate([w[:, :o_gg], w[:, o_u:o_gt], w[:, o_gt:], w[:, o_gg:o_rg], pad,
                            w[:, o_rg:o_rw], w[:, o_rw:o_ra], w[:, o_ra:o_u]], axis=1).astype(BF16)


def kernel(x, c, ctx, c_ctx, w_ada, b_ada, norm_mix, w_in, conv_w, gla_wg2, gla_bg, gla_norm, rw_w0, rw_w2, rw_a0, rw_a2, rw_g2, rw_kk, rw_ka, rw_rk, rw_ln_g, rw_ln_b, s5_lam_re, s5_lam_im, s5_log_dt, s5_b_re, s5_b_im, s5_c_re, s5_c_im, s5_d, s5_glu_w, s5_glu_b, w_branch, w_out, norm_ffn, router_w, router_b, exp_w_gate, exp_b_gate, exp_w_up, exp_b_up, exp_w_down, exp_b_down, final_norm):
    depth = w_ada.shape[0]
    nb, _, d = x.shape
    n_ctx = ctx.shape[1]
    assert d == D_MODEL and n_ctx % RWKV_SEG == 0 and x.shape[1] % RWKV_SEG == 0 and nb < MOD_ROWS

    resid = jnp.concatenate([ctx, x], axis=1).astype(F32)
    cc = jnp.concatenate([c, c_ctx[None], jnp.zeros((MOD_ROWS - nb - 1, d), F32)], axis=0)
    mods = _ada_table(cc, w_ada, b_ada)

    for l in range(depth):
        mod = mods[l]
        p = _norm_in_proj(resid, norm_mix[l], mod, _pack_w_in(w_in[l]), n_ctx)
        pcv = _short_conv(p, conv_w[l], n_ctx)
        y_gla = _gla_mixer(pcv, p, gla_wg2[l], gla_bg[l], gla_norm[l], n_ctx)
        prm = dict(w0=rw_w0[l], w2=rw_w2[l], a0=rw_a0[l], a2=rw_a2[l], g2=rw_g2[l], kk=rw_kk[l], ka=rw_ka[l],
                   rk=rw_rk[l].reshape(-1), ln_g=rw_ln_g[l], ln_b=rw_ln_b[l])
        y_rw = _rwkv_mixer(pcv, p, prm, n_ctx)
        ops = _s5_operators(s5_lam_re[l], s5_lam_im[l], s5_log_dt[l], s5_b_re[l], s5_b_im[l],
                            s5_c_re[l], s5_c_im[l])
        y_s5 = _s5_mixer(p, ops, n_ctx)
        resid = _merge(y_gla, y_rw, y_s5, p, resid, mod, w_branch[l], w_out[l], s5_d[l], s5_glu_w[l],
                       s5_glu_b[l], n_ctx)
        h, *routing = _router(resid, norm_ffn[l], mod, router_w[l], router_b[l], n_ctx)
        y_moe = _moe_routed(h, *routing, exp_w_gate[l], exp_b_gate[l], exp_w_up[l], exp_b_up[l],
                            exp_w_down[l], exp_b_down[l])
        resid = _resid_gate(resid, y_moe, mod, 5 * D_MODEL, n_ctx)
    return _final_norm(resid, final_norm, n_ctx)
```

```python
import functools
import math

import jax
import jax.numpy as jnp
from jax import lax
from jax.experimental import pallas as pl
from jax.experimental.pallas import tpu as pltpu

F32 = jnp.float32
BF16 = jnp.bfloat16
HI = lax.Precision.HIGHEST

D_MODEL = 1024
GRID_W = 64

GLA_HEADS = 4
GLA_DK = 128
GLA_DV = 256
GLA_QK = GLA_HEADS * GLA_DK
GLA_V = GLA_HEADS * GLA_DV
GLA_GATE_RANK = 16
GLA_GATE_NORM = 16.0
CHUNK = 64
GLA_GROUP = 4

RWKV_HEAD = 64
RWKV_DIM = 1024
RWKV_W_RANK = 64
RWKV_A_RANK = 64
RWKV_G_RANK = 128
RWKV_W_SCALE = math.exp(-0.5)
RWKV_LN_EPS = 64e-5
RWKV_LANES = 256
RWKV_HPB = RWKV_LANES // RWKV_HEAD
RWKV_SEG = 256

S5_GROUP_CH = 16
S5_GROUPS = 64
S5_STATE = 64
S5_CHUNK = 16
S5_LANE_GROUPS = 8
MOD_ROWS = 8

N_EXPERTS = 32
TOP_K = 4
SWIGLU_LIMIT = 7.0
SWIGLU_ALPHA = 1.702
NORM_EPS = 1e-6

GLA_CONV = 2 * GLA_QK + GLA_V
CONV_CH = GLA_CONV + 3 * RWKV_DIM

OFF_PC = 0
OFF_OG = CONV_CH
OFF_U = OFF_OG + GLA_V
OFF_GATES = OFF_U + D_MODEL
OFF_GG = OFF_GATES + 3 * D_MODEL
OFF_RG = OFF_GG + 128
OFF_RW = OFF_RG + 128
OFF_RA = OFF_RW + 128
NP = OFF_RA + 128

VMEM_LIMIT = 56 * 1024 * 1024


def _cparams(sem):
    return pltpu.CompilerParams(dimension_semantics=sem, vmem_limit_bytes=VMEM_LIMIT)


def _row_tile(t, cap):
    best = 8
    for d in range(8, min(t, cap) + 1, 8):
        if t % d == 0:
            best = d
    return best


def _silu(x):
    return x * jax.nn.sigmoid(x)


def _log_sigmoid(z):
    return jnp.minimum(z, 0.0) - jnp.log(1.0 + jnp.exp(-jnp.abs(z)))


def _tri_matmul(keep, x):
    n = x.shape[1]
    h1 = x.astype(BF16)
    r1 = x - h1.astype(F32)
    h2 = r1.astype(BF16)
    h3 = (r1 - h2.astype(F32)).astype(BF16)
    y = jnp.dot(keep.astype(BF16), jnp.concatenate([h1, h2, h3], axis=1), preferred_element_type=F32)
    return y[:, 0:n] + y[:, n:2 * n] + y[:, 2 * n:3 * n]


def _row_to_col(row):
    return jnp.transpose(jnp.broadcast_to(row, (128, row.shape[1])))[:, 0:1]


def _mod_rows(mod_ref, b, nb, off, is_ctx):
    vb = mod_ref[pl.ds(b, 1), off:off + D_MODEL]
    vc = mod_ref[nb:nb + 1, off:off + D_MODEL]
    return jnp.where(is_ctx, vc, vb)


def _ada_kernel(c_ref, w_ref, b_ref, o_ref):
    a = _silu(c_ref[...])
    o_ref[0] = jnp.dot(a, w_ref[0], precision=HI, preferred_element_type=F32) + b_ref[0]


def _ada_table(cc, w_ada, b_ada):
    depth, d, n = w_ada.shape
    tn = 1536
    return pl.pallas_call(
        _ada_kernel,
        out_shape=jax.ShapeDtypeStruct((depth, cc.shape[0], n), F32),
        grid=(depth, n // tn),
        in_specs=[
            pl.BlockSpec((cc.shape[0], d), lambda l, j: (0, 0)),
            pl.BlockSpec((1, d, tn), lambda l, j: (l, 0, j)),
            pl.BlockSpec((1, 1, tn), lambda l, j: (l, 0, j)),
        ],
        out_specs=pl.BlockSpec((1, cc.shape[0], tn), lambda l, j: (l, 0, j)),
        compiler_params=_cparams(("arbitrary", "arbitrary")),
        name="ada_table",
    )(cc, w_ada, b_ada.reshape(depth, 1, n))


def _norm_mm_kernel(x_ref, nw_ref, mod_ref, w_ref, o_ref, h_ref, *, tm, n_ctx, nb):
    b = pl.program_id(0)
    i = pl.program_id(1)

    @pl.when(pl.program_id(2) == 0)
    def _():
        x = x_ref[0]
        xn = x * lax.rsqrt(jnp.mean(x * x, axis=-1, keepdims=True) + NORM_EPS) * nw_ref[...]
        row = i * tm + lax.broadcasted_iota(jnp.int32, (tm, 1), 0)
        is_ctx = row < n_ctx
        sh = _mod_rows(mod_ref, b, nb, 0, is_ctx)
        sc = _mod_rows(mod_ref, b, nb, D_MODEL, is_ctx)
        h_ref[...] = (xn * (1.0 + sc) + sh).astype(BF16)

    o_ref[0] = jnp.dot(h_ref[...], w_ref[...], preferred_element_type=F32)


def _norm_in_proj(resid, norm_w, mod, w_in_p, n_ctx):
    nb, t, d = resid.shape
    tm = _row_tile(t, 1088)
    tn = 1536
    kern = functools.partial(_norm_mm_kernel, tm=tm, n_ctx=n_ctx, nb=nb)
    return pl.pallas_call(
        kern,
        out_shape=jax.ShapeDtypeStruct((nb, t, NP), F32),
        grid=(nb, t // tm, NP // tn),
        in_specs=[
            pl.BlockSpec((1, tm, d), lambda b, i, j: (b, i, 0)),
            pl.BlockSpec((1, d), lambda b, i, j: (0, 0)),
            pl.BlockSpec(mod.shape, lambda b, i, j: (0, 0)),
            pl.BlockSpec((d, tn), lambda b, i, j: (0, j)),
        ],
        out_specs=pl.BlockSpec((1, tm, tn), lambda b, i, j: (b, i, j)),
        scratch_shapes=[pltpu.VMEM((tm, d), BF16)],
        compiler_params=_cparams(("arbitrary", "arbitrary", "arbitrary")),
        name="norm_in_proj",
    )(resid, norm_w.reshape(1, d), mod, w_in_p)


def _conv_kernel(x_ref, w_ref, o_ref, xp_ref, xl_ref, xr_ref, *, n_ctx, rows):
    tc = x_ref.shape[-1]
    gw = GRID_W

    def wrow(k):
        return w_ref[k:k + 1, :]

    xc = x_ref[0, 0:n_ctx, :]
    ci = lax.broadcasted_iota(jnp.int32, (n_ctx, tc), 0)
    left = jnp.where(ci > 0, pltpu.roll(xc, 1, 0), 0.0)
    right = jnp.where(ci < n_ctx - 1, pltpu.roll(xc, n_ctx - 1, 0), 0.0)
    o_ref[0, 0:n_ctx, :] = left * wrow(3) + xc * wrow(4) + right * wrow(5)

    zeros = jnp.zeros((gw, tc), F32)
    for ref in (xp_ref, xl_ref, xr_ref):
        ref[0:gw, :] = zeros
        ref[(rows + 1) * gw:(rows + 2) * gw, :] = zeros
    wi = lax.broadcasted_iota(jnp.int32, (gw, tc), 0)

    def fill(r, carry):
        base = pl.multiple_of(r * gw, gw)
        xrow = x_ref[0, pl.ds(n_ctx + base, gw), :]
        xp_ref[pl.ds(gw + base, gw), :] = xrow
        xl_ref[pl.ds(gw + base, gw), :] = jnp.where(wi > 0, pltpu.roll(xrow, 1, 0), 0.0)
        xr_ref[pl.ds(gw + base, gw), :] = jnp.where(wi < gw - 1, pltpu.roll(xrow, gw - 1, 0), 0.0)
        return carry

    lax.fori_loop(0, rows, fill, 0)

    def comp(r, carry):
        base = pl.multiple_of(r * gw, gw)
        acc = jnp.zeros((gw, tc), F32)
        for dr in range(3):
            sl = pl.ds(base + gw * dr, gw)
            acc = acc + xl_ref[sl, :] * wrow(3 * dr) + xp_ref[sl, :] * wrow(3 * dr + 1) \
                + xr_ref[sl, :] * wrow(3 * dr + 2)
        o_ref[0, pl.ds(n_ctx + base, gw), :] = acc
        return carry

    lax.fori_loop(0, rows, comp, 0)


def _short_conv(p, conv_w, n_ctx):
    nb, t, _ = p.shape
    rows = (t - n_ctx) // GRID_W
    tc = 256
    kern = functools.partial(_conv_kernel, n_ctx=n_ctx, rows=rows)
    pad_rows = (rows + 2) * GRID_W
    return pl.pallas_call(
        kern,
        out_shape=jax.ShapeDtypeStruct((nb, t, CONV_CH), F32),
        grid=(nb, CONV_CH // tc),
        in_specs=[
            pl.BlockSpec((1, t, tc), lambda b, j: (b, 0, j)),
            pl.BlockSpec((9, tc), lambda b, j: (0, j)),
        ],
        out_specs=pl.BlockSpec((1, t, tc), lambda b, j: (b, 0, j)),
        scratch_shapes=[pltpu.VMEM((pad_rows, tc), F32)] * 3,
        compiler_params=_cparams(("arbitrary", "arbitrary")),
        name="short_conv",
    )(p, conv_w.reshape(9, CONV_CH))


def _gla_kernel(q_ref, k_ref, v_ref, gg_ref, og_ref, wg_ref, bg_ref, ng_ref, o_ref, ob_ref, *, n_ctx, t):
    c = CHUNK
    nch = t // c
    nctx = n_ctx // c
    ii = lax.broadcasted_iota(jnp.int32, (c, c), 0)
    jj = lax.broadcasted_iota(jnp.int32, (c, c), 1)
    scale = GLA_DK ** -0.5

    nt = (((1,), (1,)), ((), ()))
    tn = (((0,), (0,)), ((), ()))
    dot = functools.partial(jnp.dot, preferred_element_type=F32)
    grp = GLA_GROUP
    assert nctx % grp == 0 and nch % grp == 0

    def group(g, carry):
        chains = []
        for d in range(2):
            for u in range(grp):
                i = g * grp + u
                ci = i if d == 0 else jnp.where(i < nctx, nctx - 1 - i, nch + nctx - 1 - i)
                chains.append((d, pl.ds(pl.multiple_of(ci * c, c), c)))
        keeps = [(ii >= jj) if d == 0 else (ii <= jj) for d, _ in chains]
        q = [_silu(q_ref[0, r, :]) * scale for _, r in chains]
        k = [_silu(k_ref[0, r, :]) for _, r in chains]
        vb = [_silu(v_ref[0, r, :]).astype(BF16) for _, r in chains]
        z = [dot(gg_ref[0, r, :][:, GLA_GATE_RANK * d:GLA_GATE_RANK * (d + 1)], wg_ref[d]) + bg_ref[d]
             for d, r in chains]
        la = [_log_sigmoid(x) * (1.0 / GLA_GATE_NORM) for x in z]
        bcum = [_tri_matmul(kp, x) for kp, x in zip(keeps, la)]
        blast = [b[c - 1:c, :] if d == 0 else b[0:1, :] for (d, _), b in zip(chains, bcum)]
        q_dec = [(x * jnp.exp(b)).astype(BF16) for x, b in zip(q, bcum)]
        k_inv = [(x * jnp.exp(-b)).astype(BF16) for x, b in zip(k, bcum)]
        k_end = [(x * jnp.exp(bl - b)).astype(BF16) for x, b, bl in zip(k, bcum, blast)]
        sc = [lax.dot_general(a, b, nt, preferred_element_type=F32) for a, b in zip(q_dec, k_inv)]
        sc = [jnp.where(kp, x, 0.0).astype(BF16) for kp, x in zip(keeps, sc)]
        intra = [dot(a, b) for a, b in zip(sc, vb)]
        kv = [lax.dot_general(a, b, tn, preferred_element_type=F32) for a, b in zip(k_end, vb)]
        dec = [_row_to_col(jnp.exp(bl)) for bl in blast]
        states = list(carry)
        for n, (d, r) in enumerate(chains):
            s = states[d]
            o = intra[n] + dot(q_dec[n], s.astype(BF16))
            if d == 0:
                o_ref[0, r, :] = o
            else:
                ob_ref[r, :] = o
            states[d] = dec[n] * s + kv[n]
        return tuple(states)

    s0 = jnp.zeros((GLA_DK, GLA_DV), F32)
    lax.fori_loop(0, nch // grp, group, (s0, s0))

    def finish(ci, carry):
        rows = pl.ds(pl.multiple_of(ci * c, c), c)
        o = o_ref[0, rows, :] + ob_ref[rows, :]
        o = o * lax.rsqrt(jnp.mean(o * o, axis=-1, keepdims=True) + 1e-5) * ng_ref[...]
        o_ref[0, rows, :] = o * _silu(og_ref[0, rows, :])
        return carry

    lax.fori_loop(0, nch, finish, 0, unroll=2)


def _gla_mixer(pcv, p, wg2, bg, norm_g, n_ctx):
    nb, t, _ = pcv.shape
    kern = functools.partial(_gla_kernel, n_ctx=n_ctx, t=t)
    return pl.pallas_call(
        kern,
        out_shape=jax.ShapeDtypeStruct((nb, t, GLA_V), F32),
        grid=(nb, GLA_HEADS),
        in_specs=[
            pl.BlockSpec((1, t, GLA_DK), lambda b, h: (b, 0, h)),
            pl.BlockSpec((1, t, GLA_DK), lambda b, h: (b, 0, GLA_HEADS + h)),
            pl.BlockSpec((1, t, GLA_DV), lambda b, h: (b, 0, GLA_HEADS + h)),
            pl.BlockSpec((1, t, 128), lambda b, h: (b, 0, OFF_GG // 128)),
            pl.BlockSpec((1, t, GLA_DV), lambda b, h: (b, 0, OFF_OG // GLA_DV + h)),
            pl.BlockSpec((2, GLA_GATE_RANK, GLA_DK), lambda b, h: (0, 0, h)),
            pl.BlockSpec((2, 1, GLA_DK), lambda b, h: (0, 0, h)),
            pl.BlockSpec((1, GLA_DV), lambda b, h: (0, 0)),
        ],
        out_specs=pl.BlockSpec((1, t, GLA_DV), lambda b, h: (b, 0, h)),
        scratch_shapes=[pltpu.VMEM((t, GLA_DV), F32)],
        compiler_params=_cparams(("arbitrary", "arbitrary")),
        name="gla_mixer",
    )(pcv, pcv, pcv, p, p, wg2, bg.reshape(2, 1, GLA_QK), norm_g.reshape(1, GLA_DV))


def _bd_mask():
    shift = RWKV_HEAD.bit_length() - 1
    r = lax.broadcasted_iota(jnp.int32, (RWKV_LANES, RWKV_LANES), 0) >> shift
    c = lax.broadcasted_iota(jnp.int32, (RWKV_LANES, RWKV_LANES), 1) >> shift
    return r == c


def _bd(x, mask):
    xb = x.astype(BF16)
    return jnp.where(mask, jnp.concatenate([xb] * RWKV_HPB, axis=0), jnp.zeros((), BF16))


def _head_sum(x, ones_bd):
    hi = x.astype(BF16)
    lo = (x - hi.astype(F32)).astype(BF16)
    return jnp.dot(hi, ones_bd, preferred_element_type=F32) + jnp.dot(lo, ones_bd, preferred_element_type=F32)


def _rwkv_group_kernel(rf_ref, kf_ref, vf_ref, rwf_ref, raf_ref, rb_ref, kb_ref, vb_ref, rwb_ref, rab_ref,
                       w0_ref, w2_ref, a0_ref, a2_ref, kkp_ref, kap_ref, yf_ref, yb_ref, stf_ref, stb_ref):
    c = CHUNK
    nchunk = RWKV_SEG // c
    mask_bd = _bd_mask()
    ones_bd = mask_bd.astype(BF16)
    dot = functools.partial(jnp.dot, preferred_element_type=F32)
    nt = (((1,), (1,)), ((), ()))
    tn = (((0,), (0,)), ((), ()))

    @pl.when(pl.program_id(2) == 0)
    def _():
        stf_ref[...] = jnp.zeros_like(stf_ref)
        stb_ref[...] = jnp.zeros_like(stb_ref)

    chains = [(0, slice(u * c, (u + 1) * c)) for u in range(nchunk)] \
        + [(1, slice(u * c, (u + 1) * c)) for u in range(nchunk - 1, -1, -1)]
    refs = ((rf_ref, kf_ref, vf_ref, rwf_ref, raf_ref), (rb_ref, kb_ref, vb_ref, rwb_ref, rab_ref))

    def each(f, *lists):
        return [f(*xs) for xs in zip(*lists)]

    ds = [d for d, _ in chains]
    r = [refs[d][0][0, rows, :] for d, rows in chains]
    k = [refs[d][1][0, rows, :] for d, rows in chains]
    v = [refs[d][2][0, rows, :] for d, rows in chains]
    pw = [refs[d][3][0, rows, :][:, RWKV_W_RANK * d:RWKV_W_RANK * (d + 1)] for d, rows in chains]
    pa = [refs[d][4][0, rows, :][:, RWKV_A_RANK * d:RWKV_A_RANK * (d + 1)] for d, rows in chains]
    lw = each(lambda d, x: w0_ref[d] + dot(jnp.tanh(x), w2_ref[d]), ds, pw)
    logw = [-RWKV_W_SCALE * jax.nn.sigmoid(x) for x in lw]
    la = each(lambda d, x: a0_ref[d] + dot(x, a2_ref[d]), ds, pa)
    a = [jax.nn.sigmoid(x) for x in la]
    kkv = [x * kkp_ref[...] for x in k]
    ss = [_head_sum(x * x, ones_bd) for x in kkv]
    kk = each(lambda x, s: x / jnp.maximum(jnp.sqrt(s), 1e-12), kkv, ss)
    kd = each(lambda x, y: x * (1.0 + (y - 1.0) * kap_ref[...]), k, a)
    kka = each(lambda x, y: x * y, kk, a)

    ii = lax.broadcasted_iota(jnp.int32, (c, c), 0)
    jj = lax.broadcasted_iota(jnp.int32, (c, c), 1)
    ti = lax.broadcasted_iota(jnp.int32, (c, RWKV_LANES), 0)
    si = lax.broadcasted_iota(jnp.int32, (c, RWKV_LANES), 1) & (c - 1)
    keep_sq = [(ii >= jj) if d == 0 else (ii <= jj) for d in ds]
    incl = [(si <= ti) if d == 0 else (si >= ti) for d in ds]
    strict = [(si < ti) if d == 0 else (si > ti) for d in ds]
    eye = (si == ti).astype(F32)

    cb = each(_tri_matmul, keep_sq, logw)
    cblast = each(lambda d, x: x[c - 1:c, :] if d == 0 else x[0:1, :], ds, cb)
    ad = each(lambda x, b, w: -x * jnp.exp(b - w), kk, cb, logw)
    rd = each(lambda x, b: x * jnp.exp(b), r, cb)
    einv = [jnp.exp(-b) for b in cb]
    bi = each(lambda x, e: x * e, kka, einv)
    ki = each(lambda x, e: x * e, kd, einv)
    eend = each(lambda bl, b: jnp.exp(bl - b), cblast, cb)
    bend = each(lambda x, e: x * e, kka, eend)
    kend = each(lambda x, e: x * e, kd, eend)

    def bd(x):
        return _bd(x, mask_bd)

    def mm(x, y_bd):
        return dot(x.astype(BF16), y_bd)

    lhs = each(lambda x, y: jnp.concatenate([x, y], axis=0).astype(BF16), ad, rd)
    pb = each(lambda x, y: lax.dot_general(x, bd(y), nt, preferred_element_type=F32), lhs, bi)
    pk = each(lambda x, y: lax.dot_general(x, bd(y), nt, preferred_element_type=F32), lhs, ki)
    a_ab = each(lambda m, x: jnp.where(m, x[0:c], 0.0), strict, pb)
    r_ab = each(lambda m, x: jnp.where(m, x[c:2 * c], 0.0), incl, pb)
    a_ak = each(lambda m, x: jnp.where(m, x[0:c], 0.0), strict, pk)
    r_ak = each(lambda m, x: jnp.where(m, x[c:2 * c], 0.0), incl, pk)

    tinv = [eye + x for x in a_ab]
    pwr = each(lambda x: mm(x, bd(x)), a_ab)
    for lvl in range(5):
        if lvl < 4:
            both = each(lambda t_, p_: mm(jnp.concatenate([t_, p_], axis=0), bd(p_)), tinv, pwr)
            tinv = each(lambda t_, b: t_ + b[0:c], tinv, both)
            pwr = [b[c:2 * c] for b in both]
        else:
            tinv = each(lambda t_, p_: t_ + mm(t_, bd(p_)), tinv, pwr)

    ln = RWKV_LANES
    akv = each(lambda x, y, v_: mm(jnp.concatenate([x, y], axis=0), bd(v_)), a_ak, r_ak, v)
    mu = each(lambda t_, x, av: mm(t_, jnp.concatenate([bd(x), bd(av[0:c])], axis=1)), tinv, ad, akv)
    m1 = [x[:, 0:ln] for x in mu]
    u0 = [x[:, ln:2 * ln] for x in mu]
    ry = each(lambda rab, m, u_: mm(rab, jnp.concatenate([bd(m), bd(u_)], axis=1)), r_ab, m1, u0)
    r1 = each(lambda x, z: (x + z[:, 0:ln]).astype(BF16), rd, ry)
    y0 = each(lambda z, av: z[:, ln:2 * ln] + av[c:2 * c], ry, akv)
    g1 = each(lambda x, m: jnp.where(mask_bd, lax.dot_general(x.astype(BF16), m.astype(BF16), tn,
                                                              preferred_element_type=F32), 0.0).astype(BF16),
              bend, m1)
    h = each(lambda x, y, u_, v_: jnp.where(mask_bd, lax.dot_general(
        jnp.concatenate([x, y], axis=0).astype(BF16), jnp.concatenate([u_, v_], axis=0).astype(BF16), tn,
        preferred_element_type=F32), 0.0), bend, kend, u0, v)
    dec = [_row_to_col(jnp.exp(x)) for x in cblast]

    states = [stf_ref[...], stb_ref[...]]
    outs = (yf_ref, yb_ref)
    for n, (d, rows) in enumerate(chains):
        st = states[d]
        stb = st.astype(BF16)
        outs[d][0, rows, :] = dot(r1[n], stb) + y0[n]
        states[d] = dec[n] * st + dot(g1[n], stb) + h[n]
    stf_ref[...] = states[0]
    stb_ref[...] = states[1]


def _rwkv_post_kernel(r_ref, k_ref, v_ref, ra_ref, rg_ref, yf_ref, yb_ref, a0_ref, a2_ref, g2_ref,
                      kap_ref, rkp_ref, lng_ref, lnb_ref, o_ref):
    ones_bd = _bd_mask().astype(BF16)
    r = r_ref[0]
    k = k_ref[0]
    ra = ra_ref[0]
    y = yf_ref[0] + yb_ref[0]
    inv_n = 1.0 / RWKV_HEAD
    mu = _head_sum(y, ones_bd) * inv_n
    yc = y - mu
    var = _head_sum(yc * yc, ones_bd) * inv_n
    yn = yc * lax.rsqrt(var + RWKV_LN_EPS) * lng_ref[...] + lnb_ref[...]
    kds = jnp.zeros_like(k)
    for d in range(2):
        la = a0_ref[d] + jnp.dot(ra[:, RWKV_A_RANK * d:RWKV_A_RANK * (d + 1)], a2_ref[d],
                                 preferred_element_type=F32)
        kds = kds + k * (1.0 + (jax.nn.sigmoid(la) - 1.0) * kap_ref[...])
    bonus = _head_sum(r * kds * rkp_ref[...], ones_bd) * v_ref[0]
    g = jnp.dot(jax.nn.sigmoid(rg_ref[0]), g2_ref[...], preferred_element_type=F32)
    o_ref[0] = (yn + bonus) * g


def _rwkv_mixer(pcv, p, prm, n_ctx):
    nb, t, _ = pcv.shape
    seg = RWKV_SEG
    nseg = t // seg
    nseg_ctx = n_ctx // seg
    ln = RWKV_LANES
    nhb = RWKV_DIM // ln
    cb0 = GLA_CONV // ln

    def fmap(s):
        return s

    def bmap(s):
        return jnp.where(s < nseg_ctx, nseg_ctx - 1 - s, nseg - 1 - (s - nseg_ctx))

    def tok(off_blocks, smap):
        return pl.BlockSpec((1, seg, ln), lambda b, h, s: (b, smap(s), off_blocks + h))

    def small(off, smap):
        return pl.BlockSpec((1, seg, 128), lambda b, h, s: (b, smap(s), off // 128))

    def dirspecs(smap):
        return [tok(cb0, smap), tok(cb0 + nhb, smap), tok(cb0 + 2 * nhb, smap),
                small(OFF_RW, smap), small(OFF_RA, smap)]

    def vec():
        return pl.BlockSpec((1, ln), lambda b, h, s: (0, h))

    w0 = prm["w0"].reshape(2, 1, RWKV_DIM)
    a0 = prm["a0"].reshape(2, 1, RWKV_DIM)
    kkp = prm["kk"].reshape(1, RWKV_DIM)
    kap = prm["ka"].reshape(1, RWKV_DIM)
    yf, yb = pl.pallas_call(
        _rwkv_group_kernel,
        out_shape=(jax.ShapeDtypeStruct((nb, t, RWKV_DIM), F32),) * 2,
        grid=(nb, nhb, nseg),
        in_specs=dirspecs(fmap) + dirspecs(bmap) + [
            pl.BlockSpec((2, 1, ln), lambda b, h, s: (0, 0, h)),
            pl.BlockSpec((2, RWKV_W_RANK, ln), lambda b, h, s: (0, 0, h)),
            pl.BlockSpec((2, 1, ln), lambda b, h, s: (0, 0, h)),
            pl.BlockSpec((2, RWKV_A_RANK, ln), lambda b, h, s: (0, 0, h)),
            vec(), vec()],
        out_specs=(pl.BlockSpec((1, seg, ln), lambda b, h, s: (b, fmap(s), h)),
                   pl.BlockSpec((1, seg, ln), lambda b, h, s: (b, bmap(s), h))),
        scratch_shapes=[pltpu.VMEM((ln, ln), F32)] * 2,
        compiler_params=_cparams(("arbitrary", "arbitrary", "arbitrary")),
        name="rwkv7_scan",
    )(pcv, pcv, pcv, p, p, pcv, pcv, pcv, p, p, w0, prm["w2"], a0, prm["a2"], kkp, kap)

    tm = _row_tile(t, 1088)

    def ptok(off_blocks):
        return pl.BlockSpec((1, tm, ln), lambda b, i, h: (b, i, off_blocks + h))

    def psmall(off):
        return pl.BlockSpec((1, tm, 128), lambda b, i, h: (b, i, off // 128))

    def pvec():
        return pl.BlockSpec((1, ln), lambda b, i, h: (0, h))

    return pl.pallas_call(
        _rwkv_post_kernel,
        out_shape=jax.ShapeDtypeStruct((nb, t, RWKV_DIM), F32),
        grid=(nb, t // tm, nhb),
        in_specs=[ptok(cb0), ptok(cb0 + nhb), ptok(cb0 + 2 * nhb), psmall(OFF_RA), psmall(OFF_RG),
                  ptok(0), ptok(0),
                  pl.BlockSpec((2, 1, ln), lambda b, i, h: (0, 0, h)),
                  pl.BlockSpec((2, RWKV_A_RANK, ln), lambda b, i, h: (0, 0, h)),
                  pl.BlockSpec((RWKV_G_RANK, ln), lambda b, i, h: (0, h)),
                  pvec(), pvec(), pvec(), pvec()],
        out_specs=ptok(0),
        compiler_params=_cparams(("arbitrary", "arbitrary", "arbitrary")),
        name="rwkv7_post",
    )(pcv, pcv, pcv, p, p, yf, yb, a0, prm["a2"], prm["g2"], kap, prm["rk"].reshape(1, RWKV_DIM),
      prm["ln_g"].reshape(1, RWKV_DIM), prm["ln_b"].reshape(1, RWKV_DIM))


def _s5_operators(lam_re, lam_im, log_dt, b_re, b_im, c_re, c_im):
    g, p_, lc = S5_GROUPS, S5_STATE, S5_CHUNK
    lr, li = lam_re.astype(F32), lam_im.astype(F32)
    br, bi = b_re.astype(F32), b_im.astype(F32)
    den = lr * lr + li * li
    n = jnp.arange(lc + 1, dtype=F32)[:, None, None]
    ein = functools.partial(jnp.einsum, precision=HI)
    out = []
    for d in range(2):
        dt = jnp.exp(log_dt[d].astype(F32))[:, None]
        mag = jnp.exp(n * (lr * dt)[None])
        ang = n * (li * dt)[None]
        pr, pi = mag * jnp.cos(ang), mag * jnp.sin(ang)
        m1 = jnp.exp(lr * dt)
        ar, ai = m1 * jnp.cos(li * dt), m1 * jnp.sin(li * dt)
        fr = ((ar - 1.0) * lr + ai * li) / den
        fi = (ai * lr - (ar - 1.0) * li) / den
        bb_r = fr[..., None] * br - fi[..., None] * bi
        bb_i = fr[..., None] * bi + fi[..., None] * br
        cr, ci = c_re[d].astype(F32), c_im[d].astype(F32)
        ca_r = cr[None] * pr[:, :, None, :] - ci[None] * pi[:, :, None, :]
        ca_i = cr[None] * pi[:, :, None, :] + ci[None] * pr[:, :, None, :]
        kt = ein('tgop,gpc->tgoc', ca_r[:lc], bb_r) - ein('tgop,gpc->tgoc', ca_i[:lc], bb_i)
        nlg, gl, ch = S5_LANE_GROUPS, S5_GROUPS // S5_LANE_GROUPS, S5_GROUP_CH
        ktc = kt.reshape(lc, nlg, gl, ch, ch).transpose(1, 0, 4, 2, 3).reshape(nlg, lc, ch, gl * ch)
        zer = jnp.zeros((nlg, lc - 1, ch, gl * ch), F32)
        lagc = jnp.concatenate([zer, ktc], axis=1) if d == 0 else jnp.concatenate([ktc[:, ::-1], zer], axis=1)

        e_s = (lc - 1 - jnp.arange(lc)) if d == 0 else jnp.arange(lc)
        ps_r, ps_i = pr[e_s], pi[e_s]
        bst_r = ps_r[..., None] * bb_r[None] - ps_i[..., None] * bb_i[None]
        bst_i = ps_r[..., None] * bb_i[None] + ps_i[..., None] * bb_r[None]

        def rows_op(x):
            return x.reshape(lc, nlg, gl, p_, ch).transpose(1, 0, 4, 2, 3).reshape(nlg, lc, ch, gl * p_)

        f_t = (jnp.arange(lc) + 1) if d == 0 else (lc - jnp.arange(lc))

        def cols_op(x):
            return x.reshape(lc, nlg, gl, ch, p_).transpose(1, 0, 4, 2, 3).reshape(nlg, lc, p_, gl * ch)

        out.append(dict(
            lag=lagc, bst_r=rows_op(bst_r), bst_i=rows_op(bst_i),
            co_r=cols_op(ca_r[f_t]), co_i=cols_op(-ca_i[f_t]),
            a_r=pr[lc].reshape(nlg, 1, gl * p_), a_i=pi[lc].reshape(nlg, 1, gl * p_)))
    return {k: jnp.stack([out[0][k], out[1][k]]) for k in out[0]}


def _s5_kernel(u_ref, lag_ref, br_ref, bi_ref, cr_ref, ci_ref, ar_ref, ai_ref, o_ref,
               ir_ref, ii_ref, xr_ref, xi_ref, op_ref, bsr_ref, bsi_ref, cor_ref, coi_ref, *, nchunk, nchunk_ctx):
    d = pl.program_id(1)
    lc = S5_CHUNK
    gl = S5_GROUPS // S5_LANE_GROUPS
    dot = functools.partial(jnp.dot, preferred_element_type=F32)

    def expand(x):
        w, n = x.shape
        m = n // gl
        rg = lax.broadcasted_iota(jnp.int32, (gl * w, n), 0) >> (w.bit_length() - 1)
        cg = lax.broadcasted_iota(jnp.int32, (gl * w, n), 1) >> (m.bit_length() - 1)
        return jnp.where(rg == cg, jnp.concatenate([x] * gl, axis=0), 0.0).astype(BF16)

    @pl.when(pl.program_id(2) == 0)
    def _():
        lags = [expand(lag_ref[0, 0, j]) for j in range(2 * lc - 1)]
        for t in range(lc):
            cor_ref[:, t * 128:(t + 1) * 128] = expand(cr_ref[0, 0, t])
            coi_ref[:, t * 128:(t + 1) * 128] = expand(ci_ref[0, 0, t])
            bsr_ref[t * 128:(t + 1) * 128, :] = expand(br_ref[0, 0, t])
            bsi_ref[t * 128:(t + 1) * 128, :] = expand(bi_ref[0, 0, t])
            for s in range(lc):
                op_ref[s * 128:(s + 1) * 128, t * 128:(t + 1) * 128] = lags[t - s + lc - 1]

    nbb = u_ref.shape[0]
    lhs = jnp.concatenate(
        [jnp.concatenate([u_ref[bb, pl.ds(s, nchunk, stride=lc), :] for s in range(lc)], axis=1)
         for bb in range(nbb)], axis=0).astype(BF16)
    ir_ref[...] = dot(lhs, bsr_ref[...])
    ii_ref[...] = dot(lhs, bsi_ref[...])
    ar = ar_ref[0, 0]
    ai = ai_ref[0, 0]

    def step(i, carry):
        rev = jnp.where(i < nchunk_ctx, nchunk_ctx - 1 - i, nchunk + nchunk_ctx - 1 - i)
        ci = jnp.where(d == 0, i, rev)
        out = []
        for bb in range(nbb):
            sr, si = carry[2 * bb], carry[2 * bb + 1]
            row = pl.ds(bb * nchunk + ci, 1)
            xr_ref[row, :] = sr
            xi_ref[row, :] = si
            out.append(ar * sr - ai * si + ir_ref[row, :])
            out.append(ar * si + ai * sr + ii_ref[row, :])
        return tuple(out)

    z = jnp.zeros((1, ir_ref.shape[1]), F32)
    lax.fori_loop(0, nchunk, step, (z,) * (2 * nbb), unroll=8)
    y = dot(lhs, op_ref[...]) + dot(xr_ref[...].astype(BF16), cor_ref[...]) \
        + dot(xi_ref[...].astype(BF16), coi_ref[...])
    for bb in range(nbb):
        for t in range(lc):
            o_ref[0, bb, pl.ds(t, nchunk, stride=lc), :] = y[bb * nchunk:(bb + 1) * nchunk, t * 128:(t + 1) * 128]


def _s5_mixer(p, ops, n_ctx):
    nb, t, _ = p.shape
    lc = S5_CHUNK
    nchunk = t // lc
    nlg = S5_LANE_GROUPS
    ns = ops["a_r"].shape[-1]

    def wspec(x):
        nd = x.ndim - 2
        return pl.BlockSpec((1, 1) + x.shape[2:], lambda g, d, b: (d, g) + (0,) * nd)

    keys = ("lag", "bst_r", "bst_i", "co_r", "co_i", "a_r", "a_i")
    nop = lc * 128
    kern = functools.partial(_s5_kernel, nchunk=nchunk, nchunk_ctx=n_ctx // lc)
    nbb = 2 if nb % 2 == 0 else 1
    return pl.pallas_call(
        kern,
        out_shape=jax.ShapeDtypeStruct((2, nb, t, D_MODEL), F32),
        grid=(nlg, 2, nb // nbb),
        in_specs=[pl.BlockSpec((nbb, t, 128), lambda g, d, b: (b, 0, OFF_U // 128 + g))]
        + [wspec(ops[k]) for k in keys],
        out_specs=pl.BlockSpec((1, nbb, t, 128), lambda g, d, b: (d, b, 0, g)),
        scratch_shapes=[pltpu.VMEM((nbb * nchunk, ns), F32)] * 4
        + [pltpu.VMEM((nop, nop), BF16), pltpu.VMEM((nop, ns), BF16), pltpu.VMEM((nop, ns), BF16),
           pltpu.VMEM((ns, nop), BF16), pltpu.VMEM((ns, nop), BF16)],
        compiler_params=_cparams(("arbitrary", "arbitrary", "arbitrary")),
        name="s5_mixer",
    )(p, *[ops[k] for k in keys])


def _merge_kernel(ygla_ref, yrw_ref, ys5f_ref, ys5b_ref, u_ref, ga_ref, gb_ref, gc_ref, x_ref, mod_ref, wb_ref, wo_ref,
                  d_ref, gw_ref, gbias_ref, o_ref, *, tm, n_ctx, nb):
    b = pl.program_id(0)
    i = pl.program_id(1)
    dot = functools.partial(jnp.dot, preferred_element_type=F32)
    z = jax.nn.gelu(ys5f_ref[0, 0] + ys5b_ref[0, 0] + d_ref[...] * u_ref[0])
    zs = z * jax.nn.sigmoid(dot(z.astype(BF16), gw_ref[...]) + gbias_ref[...])
    m = jax.nn.sigmoid(ga_ref[0]) * dot(ygla_ref[0].astype(BF16), wb_ref[0]) \
        + jax.nn.sigmoid(gb_ref[0]) * dot(yrw_ref[0].astype(BF16), wb_ref[1]) \
        + jax.nn.sigmoid(gc_ref[0]) * dot(zs.astype(BF16), wb_ref[2])
    mix = dot(m.astype(BF16), wo_ref[...])
    row = i * tm + lax.broadcasted_iota(jnp.int32, (tm, 1), 0)
    gate = _mod_rows(mod_ref, b, nb, 2 * D_MODEL, row < n_ctx)
    o_ref[0] = x_ref[0] + gate * mix


def _merge(ygla, yrw, ys5, p, resid, mod, w_branch, w_out, s5_d, glu_w, glu_b, n_ctx):
    nb, t, d = resid.shape
    tm = _row_tile(t, 272)
    kern = functools.partial(_merge_kernel, tm=tm, n_ctx=n_ctx, nb=nb)

    def tok():
        return pl.BlockSpec((1, tm, d), lambda b, i: (b, i, 0))

    return pl.pallas_call(
        kern,
        out_shape=jax.ShapeDtypeStruct((nb, t, d), F32),
        grid=(nb, t // tm),
        in_specs=[
            tok(), tok(),
            pl.BlockSpec((1, 1, tm, d), lambda b, i: (0, b, i, 0)),
            pl.BlockSpec((1, 1, tm, d), lambda b, i: (1, b, i, 0)),
            pl.BlockSpec((1, tm, d), lambda b, i: (b, i, OFF_U // d)),
            pl.BlockSpec((1, tm, d), lambda b, i: (b, i, OFF_GATES // d)),
            pl.BlockSpec((1, tm, d), lambda b, i: (b, i, OFF_GATES // d + 1)),
            pl.BlockSpec((1, tm, d), lambda b, i: (b, i, OFF_GATES // d + 2)),
            tok(),
            pl.BlockSpec(mod.shape, lambda b, i: (0, 0)),
            pl.BlockSpec((3, d, d), lambda b, i: (0, 0, 0)),
            pl.BlockSpec((d, d), lambda b, i: (0, 0)),
            pl.BlockSpec((1, d), lambda b, i: (0, 0)),
            pl.BlockSpec((d, d), lambda b, i: (0, 0)),
            pl.BlockSpec((1, d), lambda b, i: (0, 0)),
        ],
        out_specs=tok(),
        compiler_params=_cparams(("arbitrary", "arbitrary")),
        name="merge",
    )(ygla, yrw, ys5, ys5, p, p, p, p, resid, mod, w_branch.astype(BF16), w_out.astype(BF16), s5_d.reshape(1, d),
      glu_w.astype(BF16), glu_b.reshape(1, d))


def _router_kernel(x_ref, nw_ref, mod_ref, rw_ref, rb_ref, h_ref, comb_ref, combt_ref, rank_ref, rankt_ref,
                   offs_ref, *, tm, n_ctx, nb):
    b = pl.program_id(0)
    i = pl.program_id(1)
    x = x_ref[0]
    xn = x * lax.rsqrt(jnp.mean(x * x, axis=-1, keepdims=True) + NORM_EPS) * nw_ref[...]
    row = i * tm + lax.broadcasted_iota(jnp.int32, (tm, 1), 0)
    is_ctx = row < n_ctx
    sh = _mod_rows(mod_ref, b, nb, 3 * D_MODEL, is_ctx)
    sc = _mod_rows(mod_ref, b, nb, 4 * D_MODEL, is_ctx)
    h = xn * (1.0 + sc) + sh
    h_ref[0] = h.astype(BF16)
    logits = jnp.dot(h, rw_ref[...], precision=HI, preferred_element_type=F32) + rb_ref[...]
    lane = lax.broadcasted_iota(jnp.int32, logits.shape, 1).astype(F32)
    neg = jnp.float32(-jnp.inf)
    cur = logits
    comb = jnp.zeros_like(logits)
    denom = jnp.zeros((tm, 1), F32)
    top = None
    for kk in range(TOP_K):
        m = jnp.max(cur, axis=-1, keepdims=True)
        idx = jnp.min(jnp.where(cur == m, lane, 128.0), axis=-1, keepdims=True)
        sel = lane == idx
        if kk == 0:
            top = m
        e = jnp.exp(m - top)
        comb = comb + jnp.where(sel, e, 0.0)
        denom = denom + e
        cur = jnp.where(sel, neg, cur)
    comb = comb / denom
    comb_ref[0] = comb

    chosen = jnp.where(comb > 0.0, 1.0, 0.0).astype(BF16)
    ri = lax.broadcasted_iota(jnp.int32, (tm, tm), 0)
    ci = lax.broadcasted_iota(jnp.int32, (tm, tm), 1)
    rank = jnp.dot((ci < ri).astype(BF16), chosen, preferred_element_type=F32)
    rank_ref[0] = rank
    rankt_ref[0] = lax.dot_general(chosen, (ri < ci).astype(BF16), (((0,), (0,)), ((), ())),
                                   preferred_element_type=F32)
    eye = (lax.broadcasted_iota(jnp.int32, (128, 128), 0)
           == lax.broadcasted_iota(jnp.int32, (128, 128), 1)).astype(BF16)
    h1 = comb.astype(BF16)
    r1 = comb - h1.astype(F32)
    h2 = r1.astype(BF16)
    h3 = (r1 - h2.astype(F32)).astype(BF16)
    nt = (((1,), (1,)), ((), ()))
    combt_ref[0] = lax.dot_general(eye, h1, nt, preferred_element_type=F32) \
        + lax.dot_general(eye, h2, nt, preferred_element_type=F32) \
        + lax.dot_general(eye, h3, nt, preferred_element_type=F32)
    tile = tm // MOE_TILES
    rows = [rank[j * tile:j * tile + 1, :] for j in range(MOE_TILES)]
    rows.append(rank[tm - 1:tm, :] + chosen[tm - 1:tm, :].astype(F32))
    rows.append(jnp.zeros((16 - len(rows), 128), F32))
    offs_ref[0] = jnp.concatenate(rows, axis=0)


def _router(resid, norm_w, mod, router_w, router_b, n_ctx):
    nb, t, d = resid.shape
    tm = _row_tile(t, 1088)
    nper = t // tm
    nq = nb * nper
    rw = jnp.pad(router_w, ((0, 0), (0, 128 - N_EXPERTS)))
    rb = jnp.pad(router_b, (0, 128 - N_EXPERTS), constant_values=-1e30).reshape(1, 128)
    kern = functools.partial(_router_kernel, tm=tm, n_ctx=n_ctx, nb=nb)

    def qspec(r, c):
        return pl.BlockSpec((1, r, c), lambda b, i: (b * nper + i, 0, 0))

    return pl.pallas_call(
        kern,
        out_shape=(jax.ShapeDtypeStruct((nb, t, d), BF16),
                   jax.ShapeDtypeStruct((nq, tm, 128), F32), jax.ShapeDtypeStruct((nq, 128, tm), F32),
                   jax.ShapeDtypeStruct((nq, tm, 128), F32), jax.ShapeDtypeStruct((nq, 128, tm), F32),
                   jax.ShapeDtypeStruct((nq, 16, 128), F32)),
        grid=(nb, nper),
        in_specs=[
            pl.BlockSpec((1, tm, d), lambda b, i: (b, i, 0)),
            pl.BlockSpec((1, d), lambda b, i: (0, 0)),
            pl.BlockSpec(mod.shape, lambda b, i: (0, 0)),
            pl.BlockSpec((d, 128), lambda b, i: (0, 0)),
            pl.BlockSpec((1, 128), lambda b, i: (0, 0)),
        ],
        out_specs=(pl.BlockSpec((1, tm, d), lambda b, i: (b, i, 0)),
                   qspec(tm, 128), qspec(128, tm), qspec(tm, 128), qspec(128, tm), qspec(16, 128)),
        compiler_params=_cparams(("arbitrary", "arbitrary")),
        name="router",
    )(resid, norm_w.reshape(1, d), mod, rw, rb)


MOE_TILES = 8
MOE_FFN_ROWS = 160


def _moe_routed_kernel(off_ref, h_ref, comb_ref, combt_ref, rank_ref, rankt_ref, x_ref, mod_ref, wg_ref, bg_ref,
                       wu_ref, bu_ref, wd_ref, bd_ref, o_ref, g_ref, y_ref, *, qt, nper, nb, n_ctx):
    q = pl.program_id(0)
    e = pl.program_id(1)
    ne = pl.num_programs(1)
    tile = qt // MOE_TILES
    win = tile + 8
    fb = MOE_FFN_ROWS
    dot = functools.partial(jnp.dot, preferred_element_type=F32)

    @pl.when(e == 0)
    def _():
        o_ref[...] = jnp.zeros_like(o_ref)
        g_ref[...] = jnp.zeros_like(g_ref)
        y_ref[...] = jnp.zeros_like(y_ref)

    base = (q * ne + e) * (MOE_TILES + 1)
    wrow = combt_ref[0, pl.ds(e, 1), :]
    rrow = rankt_ref[0, pl.ds(e, 1), :]
    sub_i = lax.broadcasted_iota(jnp.int32, (win, 1), 0)
    sub_f = lax.broadcasted_iota(jnp.int32, (win, tile), 0).astype(F32)
    lane_f = lax.broadcasted_iota(jnp.int32, (tile, win), 1).astype(F32)
    lane_e = lax.broadcasted_iota(jnp.int32, (tile, 128), 1) == e

    for j in range(MOE_TILES):
        off = off_ref[base + j]
        nxt = off_ref[base + j + 1]
        ws = pl.multiple_of((off // 8) * 8, 8)
        rj = rrow[:, j * tile:(j + 1) * tile] - ws.astype(F32)
        wj = wrow[:, j * tile:(j + 1) * tile]
        onehot = jnp.where((sub_f == rj) & (wj > 0.0), 1.0, 0.0).astype(BF16)
        new = dot(onehot, h_ref[0, j * tile:(j + 1) * tile, :])
        slot = sub_i + ws
        hit = (slot >= off) & (slot < nxt)
        g_ref[pl.ds(ws, win), :] = jnp.where(hit, new, g_ref[pl.ds(ws, win), :])

    n = off_ref[base + MOE_TILES]

    def ffn(i, carry):
        rows = pl.ds(pl.multiple_of(i * fb, 8), fb)
        x = g_ref[rows, :].astype(BF16)
        gl = jnp.minimum(dot(x, wg_ref[0]) + bg_ref[0], SWIGLU_LIMIT)
        li = jnp.clip(dot(x, wu_ref[0]) + bu_ref[0], -SWIGLU_LIMIT, SWIGLU_LIMIT)
        act = gl * jax.nn.sigmoid(SWIGLU_ALPHA * gl) * (li + 1.0)
        y_ref[rows, :] = dot(act.astype(BF16), wd_ref[0]) + bd_ref[0]
        return carry

    lax.fori_loop(0, (n + fb - 1) // fb, ffn, 0)

    for j in range(MOE_TILES):
        off = off_ref[base + j]
        ws = pl.multiple_of((off // 8) * 8, 8)
        rows = slice(j * tile, (j + 1) * tile)
        wcol = jnp.sum(jnp.where(lane_e, comb_ref[0, rows, :], 0.0), axis=-1, keepdims=True)
        rcol = jnp.sum(jnp.where(lane_e, rank_ref[0, rows, :], 0.0), axis=-1, keepdims=True)
        onehot = jnp.where((lane_f == rcol - ws.astype(F32)) & (wcol > 0.0), 1.0, 0.0).astype(BF16)
        contrib = dot(onehot, y_ref[pl.ds(ws, win), :].astype(BF16))
        o_ref[0, rows, :] += wcol * contrib

    @pl.when(e == ne - 1)
    def _():
        b = q // nper
        row = (q - b * nper) * qt + lax.broadcasted_iota(jnp.int32, (qt, 1), 0)
        gate = _mod_rows(mod_ref, b, nb, 5 * D_MODEL, row < n_ctx)
        o_ref[0] = x_ref[0] + gate * o_ref[0]


def _moe_routed(h, combq, combt, rank, rankt, offs, resid, mod, wg, bg, wu, bu, wd, bd, n_ctx):
    nb, t, d = h.shape
    nq, qt, _ = combq.shape
    assert qt % (8 * MOE_TILES) == 0
    ne = N_EXPERTS
    off = offs[:, :MOE_TILES + 1, :ne].astype(jnp.int32).transpose(0, 2, 1).reshape(-1)
    gr = qt + MOE_FFN_ROWS + 16
    kern = functools.partial(_moe_routed_kernel, qt=qt, nper=t // qt, nb=nb, n_ctx=n_ctx)

    def wspec():
        return pl.BlockSpec((1, d, d), lambda q, e, off: (e, 0, 0))

    def bspec():
        return pl.BlockSpec((1, 1, d), lambda q, e, off: (e, 0, 0))

    def tokspec(w):
        return pl.BlockSpec((1, qt, w), lambda q, e, off: (q, 0, 0))

    def tspec():
        return pl.BlockSpec((1, 128, qt), lambda q, e, off: (q, 0, 0))

    out = pl.pallas_call(
        kern,
        out_shape=jax.ShapeDtypeStruct((nq, qt, d), F32),
        grid_spec=pltpu.PrefetchScalarGridSpec(
            num_scalar_prefetch=1,
            grid=(nq, ne),
            in_specs=[tokspec(d), tokspec(128), tspec(), tokspec(128), tspec(), tokspec(d),
                      pl.BlockSpec(mod.shape, lambda q, e, off: (0, 0)),
                      wspec(), bspec(), wspec(), bspec(), wspec(), bspec()],
            out_specs=tokspec(d),
            scratch_shapes=[pltpu.VMEM((gr, d), F32), pltpu.VMEM((gr, d), F32)]),
        compiler_params=_cparams(("arbitrary", "arbitrary")),
        name="moe_routed",
    )(off, h.reshape(nq, qt, d), combq, combt, rank, rankt, resid.reshape(nq, qt, d), mod,
      wg.astype(BF16), bg.reshape(ne, 1, d),
      wu.astype(BF16), bu.reshape(ne, 1, d), wd.astype(BF16), bd.reshape(ne, 1, d))
    return out.reshape(nb, t, d)


def _final_norm_kernel(x_ref, w_ref, o_ref):
    x = x_ref[0]
    o_ref[0] = x * lax.rsqrt(jnp.mean(x * x, axis=-1, keepdims=True) + NORM_EPS) * w_ref[...]


def _final_norm(resid, w, n_ctx):
    nb, t, d = resid.shape
    tm = n_ctx
    return pl.pallas_call(
        _final_norm_kernel,
        out_shape=jax.ShapeDtypeStruct((nb, t - n_ctx, d), F32),
        grid=(nb, (t - n_ctx) // tm),
        in_specs=[pl.BlockSpec((1, tm, d), lambda b, i: (b, i + 1, 0)),
                  pl.BlockSpec((1, d), lambda b, i: (0, 0))],
        out_specs=pl.BlockSpec((1, tm, d), lambda b, i: (b, i, 0)),
        compiler_params=_cparams(("arbitrary", "arbitrary")),
        name="final_norm",
    )(resid, w.reshape(1, d))


def _pack_w_in(w):
    o_gg = CONV_CH + GLA_V
    o_rg = o_gg + 2 * GLA_GATE_RANK
    o_rw = o_rg + RWKV_G_RANK
    o_ra = o_rw + 2 * RWKV_W_RANK
    o_u = o_ra + 2 * RWKV_A_RANK
    o_gt = o_u + D_MODEL
    pad = jnp.zeros((w.shape[0], 128 - 2 * GLA_GATE_RANK), w.dtype)
    return jnp.concatenate([w[:, :o_gg], w[:, o_u:o_gt], w[:, o_gt:], w[:, o_gg:o_rg], pad,
                            w[:, o_rg:o_rw], w[:, o_rw:o_ra], w[:, o_ra:o_u]], axis=1).astype(BF16)


def kernel(x, c, ctx, c_ctx, w_ada, b_ada, norm_mix, w_in, conv_w, gla_wg2, gla_bg, gla_norm, rw_w0, rw_w2, rw_a0, rw_a2, rw_g2, rw_kk, rw_ka, rw_rk, rw_ln_g, rw_ln_b, s5_lam_re, s5_lam_im, s5_log_dt, s5_b_re, s5_b_im, s5_c_re, s5_c_im, s5_d, s5_glu_w, s5_glu_b, w_branch, w_out, norm_ffn, router_w, router_b, exp_w_gate, exp_b_gate, exp_w_up, exp_b_up, exp_w_down, exp_b_down, final_norm):
    depth = w_ada.shape[0]
    nb, _, d = x.shape
    n_ctx = ctx.shape[1]
    assert d == D_MODEL and n_ctx % RWKV_SEG == 0 and x.shape[1] % RWKV_SEG == 0 and nb < MOD_ROWS

    resid = jnp.concatenate([ctx, x], axis=1).astype(F32)
    cc = jnp.concatenate([c, c_ctx[None], jnp.zeros((MOD_ROWS - nb - 1, d), F32)], axis=0)
    mods = _ada_table(cc, w_ada, b_ada)

    for l in range(depth):
        mod = mods[l]
        p = _norm_in_proj(resid, norm_mix[l], mod, _pack_w_in(w_in[l]), n_ctx)
        pcv = _short_conv(p, conv_w[l], n_ctx)
        y_gla = _gla_mixer(pcv, p, gla_wg2[l], gla_bg[l], gla_norm[l], n_ctx)
        prm = dict(w0=rw_w0[l], w2=rw_w2[l], a0=rw_a0[l], a2=rw_a2[l], g2=rw_g2[l], kk=rw_kk[l], ka=rw_ka[l],
                   rk=rw_rk[l].reshape(-1), ln_g=rw_ln_g[l], ln_b=rw_ln_b[l])
        y_rw = _rwkv_mixer(pcv, p, prm, n_ctx)
        ops = _s5_operators(s5_lam_re[l], s5_lam_im[l], s5_log_dt[l], s5_b_re[l], s5_b_im[l],
                            s5_c_re[l], s5_c_im[l])
        y_s5 = _s5_mixer(p, ops, n_ctx)
        resid = _merge(y_gla, y_rw, y_s5, p, resid, mod, w_branch[l], w_out[l], s5_d[l], s5_glu_w[l],
                       s5_glu_b[l], n_ctx)
        h, *routing = _router(resid, norm_ffn[l], mod, router_w[l], router_b[l], n_ctx)
        resid = _moe_routed(h, *routing, resid, mod, exp_w_gate[l], exp_b_gate[l], exp_w_up[l], exp_b_up[l],
                            exp_w_down[l], exp_b_down[l], n_ctx)
    return _final_norm(resid, final_norm, n_ctx)
```

```python
import functools
import math

import jax
import jax.numpy as jnp
from jax import lax
from jax.experimental import pallas as pl
from jax.experimental.pallas import tpu as pltpu

F32 = jnp.float32
BF16 = jnp.bfloat16
HI = lax.Precision.HIGHEST

D_MODEL = 1024
GRID_W = 64

GLA_HEADS = 4
GLA_DK = 128
GLA_DV = 256
GLA_QK = GLA_HEADS * GLA_DK
GLA_V = GLA_HEADS * GLA_DV
GLA_GATE_RANK = 16
GLA_GATE_NORM = 16.0
CHUNK = 64
GLA_GROUP = 4

RWKV_HEAD = 64
RWKV_DIM = 1024
RWKV_W_RANK = 64
RWKV_A_RANK = 64
RWKV_G_RANK = 128
RWKV_W_SCALE = math.exp(-0.5)
RWKV_LN_EPS = 64e-5
RWKV_LANES = 256
RWKV_HPB = RWKV_LANES // RWKV_HEAD
RWKV_SEG = 256

S5_GROUP_CH = 16
S5_GROUPS = 64
S5_STATE = 64
S5_CHUNK = 16
S5_LANE_GROUPS = 8
MOD_ROWS = 8

N_EXPERTS = 32
TOP_K = 4
SWIGLU_LIMIT = 7.0
SWIGLU_ALPHA = 1.702
NORM_EPS = 1e-6

GLA_CONV = 2 * GLA_QK + GLA_V
CONV_CH = GLA_CONV + 3 * RWKV_DIM

OFF_PC = 0
OFF_OG = CONV_CH
OFF_U = OFF_OG + GLA_V
OFF_GATES = OFF_U + D_MODEL
OFF_GG = OFF_GATES + 3 * D_MODEL
OFF_RG = OFF_GG + 128
OFF_RW = OFF_RG + 128
OFF_RA = OFF_RW + 128
NP = OFF_RA + 128

VMEM_LIMIT = 56 * 1024 * 1024


def _cparams(sem):
    return pltpu.CompilerParams(dimension_semantics=sem, vmem_limit_bytes=VMEM_LIMIT)


def _row_tile(t, cap):
    best = 8
    for d in range(8, min(t, cap) + 1, 8):
        if t % d == 0:
            best = d
    return best


def _silu(x):
    return x * jax.nn.sigmoid(x)


def _log_sigmoid(z):
    return jnp.minimum(z, 0.0) - jnp.log(1.0 + jnp.exp(-jnp.abs(z)))


def _tri_matmul(keep, x):
    n = x.shape[1]
    h1 = x.astype(BF16)
    r1 = x - h1.astype(F32)
    h2 = r1.astype(BF16)
    h3 = (r1 - h2.astype(F32)).astype(BF16)
    y = jnp.dot(keep.astype(BF16), jnp.concatenate([h1, h2, h3], axis=1), preferred_element_type=F32)
    return y[:, 0:n] + y[:, n:2 * n] + y[:, 2 * n:3 * n]


def _row_to_col(row):
    return jnp.transpose(jnp.broadcast_to(row, (128, row.shape[1])))[:, 0:1]


def _mod_rows(mod_ref, b, nb, off, is_ctx):
    vb = mod_ref[pl.ds(b, 1), off:off + D_MODEL]
    vc = mod_ref[nb:nb + 1, off:off + D_MODEL]
    return jnp.where(is_ctx, vc, vb)


def _ada_kernel(c_ref, w_ref, b_ref, o_ref):
    a = _silu(c_ref[...])
    o_ref[0] = jnp.dot(a, w_ref[0], precision=HI, preferred_element_type=F32) + b_ref[0]


def _ada_table(cc, w_ada, b_ada):
    depth, d, n = w_ada.shape
    tn = 1536
    return pl.pallas_call(
        _ada_kernel,
        out_shape=jax.ShapeDtypeStruct((depth, cc.shape[0], n), F32),
        grid=(depth, n // tn),
        in_specs=[
            pl.BlockSpec((cc.shape[0], d), lambda l, j: (0, 0)),
            pl.BlockSpec((1, d, tn), lambda l, j: (l, 0, j)),
            pl.BlockSpec((1, 1, tn), lambda l, j: (l, 0, j)),
        ],
        out_specs=pl.BlockSpec((1, cc.shape[0], tn), lambda l, j: (l, 0, j)),
        compiler_params=_cparams(("arbitrary", "arbitrary")),
        name="ada_table",
    )(cc, w_ada, b_ada.reshape(depth, 1, n))


def _norm_mm_kernel(x_ref, nw_ref, mod_ref, w_ref, o_ref, h_ref, *, tm, n_ctx, nb):
    b = pl.program_id(0)
    i = pl.program_id(1)

    @pl.when(pl.program_id(2) == 0)
    def _():
        x = x_ref[0]
        xn = x * lax.rsqrt(jnp.mean(x * x, axis=-1, keepdims=True) + NORM_EPS) * nw_ref[...]
        row = i * tm + lax.broadcasted_iota(jnp.int32, (tm, 1), 0)
        is_ctx = row < n_ctx
        sh = _mod_rows(mod_ref, b, nb, 0, is_ctx)
        sc = _mod_rows(mod_ref, b, nb, D_MODEL, is_ctx)
        h_ref[...] = (xn * (1.0 + sc) + sh).astype(BF16)

    o_ref[0] = jnp.dot(h_ref[...], w_ref[...], preferred_element_type=F32)


def _norm_in_proj(resid, norm_w, mod, w_in_p, n_ctx):
    nb, t, d = resid.shape
    tm = _row_tile(t, 1088)
    tn = 1536
    kern = functools.partial(_norm_mm_kernel, tm=tm, n_ctx=n_ctx, nb=nb)
    return pl.pallas_call(
        kern,
        out_shape=jax.ShapeDtypeStruct((nb, t, NP), F32),
        grid=(nb, t // tm, NP // tn),
        in_specs=[
            pl.BlockSpec((1, tm, d), lambda b, i, j: (b, i, 0)),
            pl.BlockSpec((1, d), lambda b, i, j: (0, 0)),
            pl.BlockSpec(mod.shape, lambda b, i, j: (0, 0)),
            pl.BlockSpec((d, tn), lambda b, i, j: (0, j)),
        ],
        out_specs=pl.BlockSpec((1, tm, tn), lambda b, i, j: (b, i, j)),
        scratch_shapes=[pltpu.VMEM((tm, d), BF16)],
        compiler_params=_cparams(("arbitrary", "arbitrary", "arbitrary")),
        name="norm_in_proj",
    )(resid, norm_w.reshape(1, d), mod, w_in_p)


def _conv_kernel(x_ref, w_ref, o_ref, xp_ref, xl_ref, xr_ref, *, n_ctx, rows):
    tc = x_ref.shape[-1]
    gw = GRID_W

    def wrow(k):
        return w_ref[k:k + 1, :]

    xc = x_ref[0, 0:n_ctx, :]
    ci = lax.broadcasted_iota(jnp.int32, (n_ctx, tc), 0)
    left = jnp.where(ci > 0, pltpu.roll(xc, 1, 0), 0.0)
    right = jnp.where(ci < n_ctx - 1, pltpu.roll(xc, n_ctx - 1, 0), 0.0)
    o_ref[0, 0:n_ctx, :] = left * wrow(3) + xc * wrow(4) + right * wrow(5)

    zeros = jnp.zeros((gw, tc), F32)
    for ref in (xp_ref, xl_ref, xr_ref):
        ref[0:gw, :] = zeros
        ref[(rows + 1) * gw:(rows + 2) * gw, :] = zeros
    wi = lax.broadcasted_iota(jnp.int32, (gw, tc), 0)

    def fill(r, carry):
        base = pl.multiple_of(r * gw, gw)
        xrow = x_ref[0, pl.ds(n_ctx + base, gw), :]
        xp_ref[pl.ds(gw + base, gw), :] = xrow
        xl_ref[pl.ds(gw + base, gw), :] = jnp.where(wi > 0, pltpu.roll(xrow, 1, 0), 0.0)
        xr_ref[pl.ds(gw + base, gw), :] = jnp.where(wi < gw - 1, pltpu.roll(xrow, gw - 1, 0), 0.0)
        return carry

    lax.fori_loop(0, rows, fill, 0)

    def comp(r, carry):
        base = pl.multiple_of(r * gw, gw)
        acc = jnp.zeros((gw, tc), F32)
        for dr in range(3):
            sl = pl.ds(base + gw * dr, gw)
            acc = acc + xl_ref[sl, :] * wrow(3 * dr) + xp_ref[sl, :] * wrow(3 * dr + 1) \
                + xr_ref[sl, :] * wrow(3 * dr + 2)
        o_ref[0, pl.ds(n_ctx + base, gw), :] = acc
        return carry

    lax.fori_loop(0, rows, comp, 0)


def _short_conv(p, conv_w, n_ctx):
    nb, t, _ = p.shape
    rows = (t - n_ctx) // GRID_W
    tc = 256
    kern = functools.partial(_conv_kernel, n_ctx=n_ctx, rows=rows)
    pad_rows = (rows + 2) * GRID_W
    return pl.pallas_call(
        kern,
        out_shape=jax.ShapeDtypeStruct((nb, t, CONV_CH), F32),
        grid=(nb, CONV_CH // tc),
        in_specs=[
            pl.BlockSpec((1, t, tc), lambda b, j: (b, 0, j)),
            pl.BlockSpec((9, tc), lambda b, j: (0, j)),
        ],
        out_specs=pl.BlockSpec((1, t, tc), lambda b, j: (b, 0, j)),
        scratch_shapes=[pltpu.VMEM((pad_rows, tc), F32)] * 3,
        compiler_params=_cparams(("arbitrary", "arbitrary")),
        name="short_conv",
    )(p, conv_w.reshape(9, CONV_CH))


def _gla_kernel(q_ref, k_ref, v_ref, gg_ref, og_ref, wg_ref, bg_ref, ng_ref, o_ref, ob_ref, *, n_ctx, t):
    c = CHUNK
    nch = t // c
    nctx = n_ctx // c
    ii = lax.broadcasted_iota(jnp.int32, (c, c), 0)
    jj = lax.broadcasted_iota(jnp.int32, (c, c), 1)
    scale = GLA_DK ** -0.5

    nt = (((1,), (1,)), ((), ()))
    tn = (((0,), (0,)), ((), ()))
    dot = functools.partial(jnp.dot, preferred_element_type=F32)
    grp = GLA_GROUP
    assert nctx % grp == 0 and nch % grp == 0

    def group(g, carry):
        chains = []
        for d in range(2):
            for u in range(grp):
                i = g * grp + u
                ci = i if d == 0 else jnp.where(i < nctx, nctx - 1 - i, nch + nctx - 1 - i)
                chains.append((d, pl.ds(pl.multiple_of(ci * c, c), c)))
        keeps = [(ii >= jj) if d == 0 else (ii <= jj) for d, _ in chains]
        q = [_silu(q_ref[0, r, :]) * scale for _, r in chains]
        k = [_silu(k_ref[0, r, :]) for _, r in chains]
        vb = [_silu(v_ref[0, r, :]).astype(BF16) for _, r in chains]
        z = [dot(gg_ref[0, r, :][:, GLA_GATE_RANK * d:GLA_GATE_RANK * (d + 1)], wg_ref[d]) + bg_ref[d]
             for d, r in chains]
        la = [_log_sigmoid(x) * (1.0 / GLA_GATE_NORM) for x in z]
        bcum = [_tri_matmul(kp, x) for kp, x in zip(keeps, la)]
        blast = [b[c - 1:c, :] if d == 0 else b[0:1, :] for (d, _), b in zip(chains, bcum)]
        q_dec = [(x * jnp.exp(b)).astype(BF16) for x, b in zip(q, bcum)]
        k_inv = [(x * jnp.exp(-b)).astype(BF16) for x, b in zip(k, bcum)]
        k_end = [(x * jnp.exp(bl - b)).astype(BF16) for x, b, bl in zip(k, bcum, blast)]
        sc = [lax.dot_general(a, b, nt, preferred_element_type=F32) for a, b in zip(q_dec, k_inv)]
        sc = [jnp.where(kp, x, 0.0).astype(BF16) for kp, x in zip(keeps, sc)]
        intra = [dot(a, b) for a, b in zip(sc, vb)]
        kv = [lax.dot_general(a, b, tn, preferred_element_type=F32) for a, b in zip(k_end, vb)]
        dec = [_row_to_col(jnp.exp(bl)) for bl in blast]
        states = list(carry)
        for n, (d, r) in enumerate(chains):
            s = states[d]
            o = intra[n] + dot(q_dec[n], s.astype(BF16))
            if d == 0:
                o_ref[0, r, :] = o
            else:
                ob_ref[r, :] = o
            states[d] = dec[n] * s + kv[n]
        return tuple(states)

    s0 = jnp.zeros((GLA_DK, GLA_DV), F32)
    lax.fori_loop(0, nch // grp, group, (s0, s0))

    def finish(ci, carry):
        rows = pl.ds(pl.multiple_of(ci * c, c), c)
        o = o_ref[0, rows, :] + ob_ref[rows, :]
        o = o * lax.rsqrt(jnp.mean(o * o, axis=-1, keepdims=True) + 1e-5) * ng_ref[...]
        o_ref[0, rows, :] = o * _silu(og_ref[0, rows, :])
        return carry

    lax.fori_loop(0, nch, finish, 0, unroll=2)


def _gla_mixer(pcv, p, wg2, bg, norm_g, n_ctx):
    nb, t, _ = pcv.shape
    kern = functools.partial(_gla_kernel, n_ctx=n_ctx, t=t)
    return pl.pallas_call(
        kern,
        out_shape=jax.ShapeDtypeStruct((nb, t, GLA_V), F32),
        grid=(nb, GLA_HEADS),
        in_specs=[
            pl.BlockSpec((1, t, GLA_DK), lambda b, h: (b, 0, h)),
            pl.BlockSpec((1, t, GLA_DK), lambda b, h: (b, 0, GLA_HEADS + h)),
            pl.BlockSpec((1, t, GLA_DV), lambda b, h: (b, 0, GLA_HEADS + h)),
            pl.BlockSpec((1, t, 128), lambda b, h: (b, 0, OFF_GG // 128)),
            pl.BlockSpec((1, t, GLA_DV), lambda b, h: (b, 0, OFF_OG // GLA_DV + h)),
            pl.BlockSpec((2, GLA_GATE_RANK, GLA_DK), lambda b, h: (0, 0, h)),
            pl.BlockSpec((2, 1, GLA_DK), lambda b, h: (0, 0, h)),
            pl.BlockSpec((1, GLA_DV), lambda b, h: (0, 0)),
        ],
        out_specs=pl.BlockSpec((1, t, GLA_DV), lambda b, h: (b, 0, h)),
        scratch_shapes=[pltpu.VMEM((t, GLA_DV), F32)],
        compiler_params=_cparams(("arbitrary", "arbitrary")),
        name="gla_mixer",
    )(pcv, pcv, pcv, p, p, wg2, bg.reshape(2, 1, GLA_QK), norm_g.reshape(1, GLA_DV))


def _bd_mask():
    shift = RWKV_HEAD.bit_length() - 1
    r = lax.broadcasted_iota(jnp.int32, (RWKV_LANES, RWKV_LANES), 0) >> shift
    c = lax.broadcasted_iota(jnp.int32, (RWKV_LANES, RWKV_LANES), 1) >> shift
    return r == c


def _bd(x, mask):
    xb = x.astype(BF16)
    return jnp.where(mask, jnp.concatenate([xb] * RWKV_HPB, axis=0), jnp.zeros((), BF16))


def _head_sum(x, ones_bd):
    hi = x.astype(BF16)
    lo = (x - hi.astype(F32)).astype(BF16)
    return jnp.dot(hi, ones_bd, preferred_element_type=F32) + jnp.dot(lo, ones_bd, preferred_element_type=F32)


def _rwkv_group_kernel(rf_ref, kf_ref, vf_ref, rwf_ref, raf_ref, rb_ref, kb_ref, vb_ref, rwb_ref, rab_ref,
                       w0_ref, w2_ref, a0_ref, a2_ref, kkp_ref, kap_ref, yf_ref, yb_ref, stf_ref, stb_ref):
    c = CHUNK
    nchunk = RWKV_SEG // c
    mask_bd = _bd_mask()
    ones_bd = mask_bd.astype(BF16)
    dot = functools.partial(jnp.dot, preferred_element_type=F32)
    nt = (((1,), (1,)), ((), ()))
    tn = (((0,), (0,)), ((), ()))

    @pl.when(pl.program_id(2) == 0)
    def _():
        stf_ref[...] = jnp.zeros_like(stf_ref)
        stb_ref[...] = jnp.zeros_like(stb_ref)

    chains = [(0, slice(u * c, (u + 1) * c)) for u in range(nchunk)] \
        + [(1, slice(u * c, (u + 1) * c)) for u in range(nchunk - 1, -1, -1)]
    refs = ((rf_ref, kf_ref, vf_ref, rwf_ref, raf_ref), (rb_ref, kb_ref, vb_ref, rwb_ref, rab_ref))

    def each(f, *lists):
        return [f(*xs) for xs in zip(*lists)]

    ds = [d for d, _ in chains]
    r = [refs[d][0][0, rows, :] for d, rows in chains]
    k = [refs[d][1][0, rows, :] for d, rows in chains]
    v = [refs[d][2][0, rows, :] for d, rows in chains]
    pw = [refs[d][3][0, rows, :][:, RWKV_W_RANK * d:RWKV_W_RANK * (d + 1)] for d, rows in chains]
    pa = [refs[d][4][0, rows, :][:, RWKV_A_RANK * d:RWKV_A_RANK * (d + 1)] for d, rows in chains]
    lw = each(lambda d, x: w0_ref[d] + dot(jnp.tanh(x), w2_ref[d]), ds, pw)
    logw = [-RWKV_W_SCALE * jax.nn.sigmoid(x) for x in lw]
    la = each(lambda d, x: a0_ref[d] + dot(x, a2_ref[d]), ds, pa)
    a = [jax.nn.sigmoid(x) for x in la]
    kkv = [x * kkp_ref[...] for x in k]
    ss = [dot((x * x).astype(BF16), ones_bd) for x in kkv]
    kk = each(lambda x, s: x / jnp.maximum(jnp.sqrt(s), 1e-12), kkv, ss)
    kd = each(lambda x, y: x * (1.0 + (y - 1.0) * kap_ref[...]), k, a)
    kka = each(lambda x, y: x * y, kk, a)

    ii = lax.broadcasted_iota(jnp.int32, (c, c), 0)
    jj = lax.broadcasted_iota(jnp.int32, (c, c), 1)
    ti = lax.broadcasted_iota(jnp.int32, (c, RWKV_LANES), 0)
    si = lax.broadcasted_iota(jnp.int32, (c, RWKV_LANES), 1) & (c - 1)
    keep_sq = [(ii >= jj) if d == 0 else (ii <= jj) for d in ds]
    incl = [(si <= ti) if d == 0 else (si >= ti) for d in ds]
    strict = [(si < ti) if d == 0 else (si > ti) for d in ds]
    eye = (si == ti).astype(F32)

    cb = each(_tri_matmul, keep_sq, logw)
    cblast = each(lambda d, x: x[c - 1:c, :] if d == 0 else x[0:1, :], ds, cb)
    ad = each(lambda x, b, w: -x * jnp.exp(b - w), kk, cb, logw)
    rd = each(lambda x, b: x * jnp.exp(b), r, cb)
    einv = [jnp.exp(-b) for b in cb]
    bi = each(lambda x, e: x * e, kka, einv)
    ki = each(lambda x, e: x * e, kd, einv)
    eend = each(lambda bl, b: jnp.exp(bl - b), cblast, cb)
    bend = each(lambda x, e: x * e, kka, eend)
    kend = each(lambda x, e: x * e, kd, eend)

    def bd(x):
        return _bd(x, mask_bd)

    def mm(x, y_bd):
        return dot(x.astype(BF16), y_bd)

    lhs = each(lambda x, y: jnp.concatenate([x, y], axis=0).astype(BF16), ad, rd)
    pb = each(lambda x, y: lax.dot_general(x, bd(y), nt, preferred_element_type=F32), lhs, bi)
    pk = each(lambda x, y: lax.dot_general(x, bd(y), nt, preferred_element_type=F32), lhs, ki)
    a_ab = each(lambda m, x: jnp.where(m, x[0:c], 0.0), strict, pb)
    r_ab = each(lambda m, x: jnp.where(m, x[c:2 * c], 0.0), incl, pb)
    a_ak = each(lambda m, x: jnp.where(m, x[0:c], 0.0), strict, pk)
    r_ak = each(lambda m, x: jnp.where(m, x[c:2 * c], 0.0), incl, pk)

    tinv = [eye + x for x in a_ab]
    pwr = each(lambda x: mm(x, bd(x)), a_ab)
    for lvl in range(5):
        if lvl < 4:
            both = each(lambda t_, p_: mm(jnp.concatenate([t_, p_], axis=0), bd(p_)), tinv, pwr)
            tinv = each(lambda t_, b: t_ + b[0:c], tinv, both)
            pwr = [b[c:2 * c] for b in both]
        else:
            tinv = each(lambda t_, p_: t_ + mm(t_, bd(p_)), tinv, pwr)

    ln = RWKV_LANES
    akv = each(lambda x, y, v_: mm(jnp.concatenate([x, y], axis=0), bd(v_)), a_ak, r_ak, v)
    mu = each(lambda t_, x, av: mm(t_, jnp.concatenate([bd(x), bd(av[0:c])], axis=1)), tinv, ad, akv)
    m1 = [x[:, 0:ln] for x in mu]
    u0 = [x[:, ln:2 * ln] for x in mu]
    ry = each(lambda rab, m, u_: mm(rab, jnp.concatenate([bd(m), bd(u_)], axis=1)), r_ab, m1, u0)
    r1 = each(lambda x, z: (x + z[:, 0:ln]).astype(BF16), rd, ry)
    y0 = each(lambda z, av: z[:, ln:2 * ln] + av[c:2 * c], ry, akv)
    g1 = each(lambda x, m: jnp.where(mask_bd, lax.dot_general(x.astype(BF16), m.astype(BF16), tn,
                                                              preferred_element_type=F32), 0.0).astype(BF16),
              bend, m1)
    h = each(lambda x, y, u_, v_: jnp.where(mask_bd, lax.dot_general(
        jnp.concatenate([x, y], axis=0).astype(BF16), jnp.concatenate([u_, v_], axis=0).astype(BF16), tn,
        preferred_element_type=F32), 0.0), bend, kend, u0, v)
    dec = [_row_to_col(jnp.exp(x)) for x in cblast]

    states = [stf_ref[...], stb_ref[...]]
    outs = (yf_ref, yb_ref)
    for n, (d, rows) in enumerate(chains):
        st = states[d]
        stb = st.astype(BF16)
        outs[d][0, rows, :] = dot(r1[n], stb) + y0[n]
        states[d] = dec[n] * st + dot(g1[n], stb) + h[n]
    stf_ref[...] = states[0]
    stb_ref[...] = states[1]


def _rwkv_post_kernel(r_ref, k_ref, v_ref, ra_ref, rg_ref, yf_ref, yb_ref, a0_ref, a2_ref, g2_ref,
                      kap_ref, rkp_ref, lng_ref, lnb_ref, o_ref):
    ones_bd = _bd_mask().astype(BF16)
    r = r_ref[0]
    k = k_ref[0]
    ra = ra_ref[0]
    y = yf_ref[0] + yb_ref[0]
    inv_n = 1.0 / RWKV_HEAD
    mu = _head_sum(y, ones_bd) * inv_n
    yc = y - mu
    var = _head_sum(yc * yc, ones_bd) * inv_n
    yn = yc * lax.rsqrt(var + RWKV_LN_EPS) * lng_ref[...] + lnb_ref[...]
    kds = jnp.zeros_like(k)
    for d in range(2):
        la = a0_ref[d] + jnp.dot(ra[:, RWKV_A_RANK * d:RWKV_A_RANK * (d + 1)], a2_ref[d],
                                 preferred_element_type=F32)
        kds = kds + k * (1.0 + (jax.nn.sigmoid(la) - 1.0) * kap_ref[...])
    bonus = _head_sum(r * kds * rkp_ref[...], ones_bd) * v_ref[0]
    g = jnp.dot(jax.nn.sigmoid(rg_ref[0]), g2_ref[...], preferred_element_type=F32)
    o_ref[0] = (yn + bonus) * g


def _rwkv_mixer(pcv, p, prm, n_ctx):
    nb, t, _ = pcv.shape
    seg = RWKV_SEG
    nseg = t // seg
    nseg_ctx = n_ctx // seg
    ln = RWKV_LANES
    nhb = RWKV_DIM // ln
    cb0 = GLA_CONV // ln

    def fmap(s):
        return s

    def bmap(s):
        return jnp.where(s < nseg_ctx, nseg_ctx - 1 - s, nseg - 1 - (s - nseg_ctx))

    def tok(off_blocks, smap):
        return pl.BlockSpec((1, seg, ln), lambda b, h, s: (b, smap(s), off_blocks + h))

    def small(off, smap):
        return pl.BlockSpec((1, seg, 128), lambda b, h, s: (b, smap(s), off // 128))

    def dirspecs(smap):
        return [tok(cb0, smap), tok(cb0 + nhb, smap), tok(cb0 + 2 * nhb, smap),
                small(OFF_RW, smap), small(OFF_RA, smap)]

    def vec():
        return pl.BlockSpec((1, ln), lambda b, h, s: (0, h))

    w0 = prm["w0"].reshape(2, 1, RWKV_DIM)
    a0 = prm["a0"].reshape(2, 1, RWKV_DIM)
    kkp = prm["kk"].reshape(1, RWKV_DIM)
    kap = prm["ka"].reshape(1, RWKV_DIM)
    yf, yb = pl.pallas_call(
        _rwkv_group_kernel,
        out_shape=(jax.ShapeDtypeStruct((nb, t, RWKV_DIM), F32),) * 2,
        grid=(nb, nhb, nseg),
        in_specs=dirspecs(fmap) + dirspecs(bmap) + [
            pl.BlockSpec((2, 1, ln), lambda b, h, s: (0, 0, h)),
            pl.BlockSpec((2, RWKV_W_RANK, ln), lambda b, h, s: (0, 0, h)),
            pl.BlockSpec((2, 1, ln), lambda b, h, s: (0, 0, h)),
            pl.BlockSpec((2, RWKV_A_RANK, ln), lambda b, h, s: (0, 0, h)),
            vec(), vec()],
        out_specs=(pl.BlockSpec((1, seg, ln), lambda b, h, s: (b, fmap(s), h)),
                   pl.BlockSpec((1, seg, ln), lambda b, h, s: (b, bmap(s), h))),
        scratch_shapes=[pltpu.VMEM((ln, ln), F32)] * 2,
        compiler_params=_cparams(("arbitrary", "arbitrary", "arbitrary")),
        name="rwkv7_scan",
    )(pcv, pcv, pcv, p, p, pcv, pcv, pcv, p, p, w0, prm["w2"], a0, prm["a2"], kkp, kap)

    tm = _row_tile(t, 1088)

    def ptok(off_blocks):
        return pl.BlockSpec((1, tm, ln), lambda b, i, h: (b, i, off_blocks + h))

    def psmall(off):
        return pl.BlockSpec((1, tm, 128), lambda b, i, h: (b, i, off // 128))

    def pvec():
        return pl.BlockSpec((1, ln), lambda b, i, h: (0, h))

    return pl.pallas_call(
        _rwkv_post_kernel,
        out_shape=jax.ShapeDtypeStruct((nb, t, RWKV_DIM), F32),
        grid=(nb, t // tm, nhb),
        in_specs=[ptok(cb0), ptok(cb0 + nhb), ptok(cb0 + 2 * nhb), psmall(OFF_RA), psmall(OFF_RG),
                  ptok(0), ptok(0),
                  pl.BlockSpec((2, 1, ln), lambda b, i, h: (0, 0, h)),
                  pl.BlockSpec((2, RWKV_A_RANK, ln), lambda b, i, h: (0, 0, h)),
                  pl.BlockSpec((RWKV_G_RANK, ln), lambda b, i, h: (0, h)),
                  pvec(), pvec(), pvec(), pvec()],
        out_specs=ptok(0),
        compiler_params=_cparams(("arbitrary", "arbitrary", "arbitrary")),
        name="rwkv7_post",
    )(pcv, pcv, pcv, p, p, yf, yb, a0, prm["a2"], prm["g2"], kap, prm["rk"].reshape(1, RWKV_DIM),
      prm["ln_g"].reshape(1, RWKV_DIM), prm["ln_b"].reshape(1, RWKV_DIM))


def _s5_operators(lam_re, lam_im, log_dt, b_re, b_im, c_re, c_im):
    g, p_, lc = S5_GROUPS, S5_STATE, S5_CHUNK
    lr, li = lam_re.astype(F32), lam_im.astype(F32)
    br, bi = b_re.astype(F32), b_im.astype(F32)
    den = lr * lr + li * li
    n = jnp.arange(lc + 1, dtype=F32)[:, None, None]
    out = []
    for d in range(2):
        dt = jnp.exp(log_dt[d].astype(F32))[:, None]
        mag = jnp.exp(n * (lr * dt)[None])
        ang = n * (li * dt)[None]
        pr, pi = mag * jnp.cos(ang), mag * jnp.sin(ang)
        m1 = jnp.exp(lr * dt)
        ar, ai = m1 * jnp.cos(li * dt), m1 * jnp.sin(li * dt)
        fr = ((ar - 1.0) * lr + ai * li) / den
        fi = (ai * lr - (ar - 1.0) * li) / den
        bb_r = fr[..., None] * br - fi[..., None] * bi
        bb_i = fr[..., None] * bi + fi[..., None] * br
        cr, ci = c_re[d].astype(F32), c_im[d].astype(F32)
        ca_r = cr[None] * pr[:, :, None, :] - ci[None] * pi[:, :, None, :]
        ca_i = cr[None] * pi[:, :, None, :] + ci[None] * pr[:, :, None, :]
        nlg, gl, ch = S5_LANE_GROUPS, S5_GROUPS // S5_LANE_GROUPS, S5_GROUP_CH

        def bbt(x):
            return x.reshape(nlg, gl, p_, ch).transpose(0, 3, 1, 2).reshape(nlg, ch, gl * p_)

        e_s = (lc - 1 - jnp.arange(lc)) if d == 0 else jnp.arange(lc)
        ps_r, ps_i = pr[e_s], pi[e_s]
        bst_r = ps_r[..., None] * bb_r[None] - ps_i[..., None] * bb_i[None]
        bst_i = ps_r[..., None] * bb_i[None] + ps_i[..., None] * bb_r[None]

        def rows_op(x):
            return x.reshape(lc, nlg, gl, p_, ch).transpose(1, 0, 4, 2, 3).reshape(nlg, lc, ch, gl * p_)

        f_t = jnp.concatenate([(jnp.arange(lc) + 1) if d == 0 else (lc - jnp.arange(lc)),
                               jnp.zeros((1,), jnp.int32)])

        def cols_op(x):
            nn = x.shape[0]
            return x.reshape(nn, nlg, gl, ch, p_).transpose(1, 0, 4, 2, 3).reshape(nlg, nn, p_, gl * ch)

        out.append(dict(
            bbt_r=bbt(bb_r), bbt_i=bbt(bb_i), bst_r=rows_op(bst_r), bst_i=rows_op(bst_i),
            co_r=cols_op(ca_r[f_t]), co_i=cols_op(-ca_i[f_t]),
            a_r=pr[lc].reshape(nlg, 1, gl * p_), a_i=pi[lc].reshape(nlg, 1, gl * p_)))
    return {k: jnp.stack([out[0][k], out[1][k]]) for k in out[0]}


def _s5_kernel(u_ref, bbr_ref, bbi_ref, br_ref, bi_ref, cr_ref, ci_ref, ar_ref, ai_ref, o_ref,
               ir_ref, ii_ref, xr_ref, xi_ref, op_ref, bsr_ref, bsi_ref, cor_ref, coi_ref, *, nchunk, nchunk_ctx):
    d = pl.program_id(1)
    lc = S5_CHUNK
    gl = S5_GROUPS // S5_LANE_GROUPS
    dot = functools.partial(jnp.dot, preferred_element_type=F32)

    def expand(x):
        w, n = x.shape
        m = n // gl
        rg = lax.broadcasted_iota(jnp.int32, (gl * w, n), 0) >> (w.bit_length() - 1)
        cg = lax.broadcasted_iota(jnp.int32, (gl * w, n), 1) >> (m.bit_length() - 1)
        return jnp.where(rg == cg, jnp.concatenate([x] * gl, axis=0), 0.0).astype(BF16)

    def build(direction):
        for t in range(lc + 1):
            cor_ref[:, t * 128:(t + 1) * 128] = expand(cr_ref[0, 0, t])
            coi_ref[:, t * 128:(t + 1) * 128] = expand(ci_ref[0, 0, t])
        for t in range(lc):
            bsr_ref[t * 128:(t + 1) * 128, :] = expand(br_ref[0, 0, t])
            bsi_ref[t * 128:(t + 1) * 128, :] = expand(bi_ref[0, 0, t])
        kall = dot(bbr_ref[0, 0].astype(BF16), cor_ref[...]) + dot(bbi_ref[0, 0].astype(BF16), coi_ref[...])

        def lag_block(tau):
            blk = lc if tau == 0 else (tau - 1 if direction == 0 else lc - tau)
            return expand(kall[:, blk * 128:(blk + 1) * 128])

        lags = [lag_block(tau) for tau in range(lc)]
        zero = jnp.zeros((128, 128), BF16)
        for s in range(lc):
            for t in range(lc):
                tau = (t - s) if direction == 0 else (s - t)
                op_ref[s * 128:(s + 1) * 128, t * 128:(t + 1) * 128] = lags[tau] if tau >= 0 else zero

    first = pl.program_id(2) == 0
    pl.when(first & (d == 0))(functools.partial(build, 0))
    pl.when(first & (d == 1))(functools.partial(build, 1))

    nbb = u_ref.shape[0]
    lhs = jnp.concatenate(
        [jnp.concatenate([u_ref[bb, pl.ds(s, nchunk, stride=lc), :] for s in range(lc)], axis=1)
         for bb in range(nbb)], axis=0).astype(BF16)
    ir_ref[...] = dot(lhs, bsr_ref[...])
    ii_ref[...] = dot(lhs, bsi_ref[...])
    ar = ar_ref[0, 0]
    ai = ai_ref[0, 0]

    def step(i, carry):
        rev = jnp.where(i < nchunk_ctx, nchunk_ctx - 1 - i, nchunk + nchunk_ctx - 1 - i)
        ci = jnp.where(d == 0, i, rev)
        out = []
        for bb in range(nbb):
            sr, si = carry[2 * bb], carry[2 * bb + 1]
            row = pl.ds(bb * nchunk + ci, 1)
            xr_ref[row, :] = sr
            xi_ref[row, :] = si
            out.append(ar * sr - ai * si + ir_ref[row, :])
            out.append(ar * si + ai * sr + ii_ref[row, :])
        return tuple(out)

    z = jnp.zeros((1, ir_ref.shape[1]), F32)
    lax.fori_loop(0, nchunk, step, (z,) * (2 * nbb), unroll=8)
    y = dot(lhs, op_ref[...]) + dot(xr_ref[...].astype(BF16), cor_ref[:, 0:lc * 128]) \
        + dot(xi_ref[...].astype(BF16), coi_ref[:, 0:lc * 128])
    for bb in range(nbb):
        for t in range(lc):
            o_ref[0, bb, pl.ds(t, nchunk, stride=lc), :] = y[bb * nchunk:(bb + 1) * nchunk, t * 128:(t + 1) * 128]


def _s5_mixer(p, ops, n_ctx):
    nb, t, _ = p.shape
    lc = S5_CHUNK
    nchunk = t // lc
    nlg = S5_LANE_GROUPS
    ns = ops["a_r"].shape[-1]

    def wspec(x):
        nd = x.ndim - 2
        return pl.BlockSpec((1, 1) + x.shape[2:], lambda g, d, b: (d, g) + (0,) * nd)

    keys = ("bbt_r", "bbt_i", "bst_r", "bst_i", "co_r", "co_i", "a_r", "a_i")
    nop = lc * 128
    kern = functools.partial(_s5_kernel, nchunk=nchunk, nchunk_ctx=n_ctx // lc)
    nbb = 2 if nb % 2 == 0 else 1
    return pl.pallas_call(
        kern,
        out_shape=jax.ShapeDtypeStruct((2, nb, t, D_MODEL), F32),
        grid=(nlg, 2, nb // nbb),
        in_specs=[pl.BlockSpec((nbb, t, 128), lambda g, d, b: (b, 0, OFF_U // 128 + g))]
        + [wspec(ops[k]) for k in keys],
        out_specs=pl.BlockSpec((1, nbb, t, 128), lambda g, d, b: (d, b, 0, g)),
        scratch_shapes=[pltpu.VMEM((nbb * nchunk, ns), F32)] * 4
        + [pltpu.VMEM((nop, nop), BF16), pltpu.VMEM((nop, ns), BF16), pltpu.VMEM((nop, ns), BF16),
           pltpu.VMEM((ns, nop + 128), BF16), pltpu.VMEM((ns, nop + 128), BF16)],
        compiler_params=_cparams(("arbitrary", "arbitrary", "arbitrary")),
        name="s5_mixer",
    )(p, *[ops[k] for k in keys])


def _merge_kernel(ygla_ref, yrw_ref, ys5f_ref, ys5b_ref, u_ref, ga_ref, gb_ref, gc_ref, x_ref, mod_ref, wb_ref, wo_ref,
                  d_ref, gw_ref, gbias_ref, o_ref, *, tm, n_ctx, nb):
    b = pl.program_id(0)
    i = pl.program_id(1)
    dot = functools.partial(jnp.dot, preferred_element_type=F32)
    z = jax.nn.gelu(ys5f_ref[0, 0] + ys5b_ref[0, 0] + d_ref[...] * u_ref[0])
    zs = z * jax.nn.sigmoid(dot(z.astype(BF16), gw_ref[...]) + gbias_ref[...])
    m = jax.nn.sigmoid(ga_ref[0]) * dot(ygla_ref[0].astype(BF16), wb_ref[0]) \
        + jax.nn.sigmoid(gb_ref[0]) * dot(yrw_ref[0].astype(BF16), wb_ref[1]) \
        + jax.nn.sigmoid(gc_ref[0]) * dot(zs.astype(BF16), wb_ref[2])
    mix = dot(m.astype(BF16), wo_ref[...])
    row = i * tm + lax.broadcasted_iota(jnp.int32, (tm, 1), 0)
    gate = _mod_rows(mod_ref, b, nb, 2 * D_MODEL, row < n_ctx)
    o_ref[0] = x_ref[0] + gate * mix


def _merge(ygla, yrw, ys5, p, resid, mod, w_branch, w_out, s5_d, glu_w, glu_b, n_ctx):
    nb, t, d = resid.shape
    tm = _row_tile(t, 272)
    kern = functools.partial(_merge_kernel, tm=tm, n_ctx=n_ctx, nb=nb)

    def tok():
        return pl.BlockSpec((1, tm, d), lambda b, i: (b, i, 0))

    return pl.pallas_call(
        kern,
        out_shape=jax.ShapeDtypeStruct((nb, t, d), F32),
        grid=(nb, t // tm),
        in_specs=[
            tok(), tok(),
            pl.BlockSpec((1, 1, tm, d), lambda b, i: (0, b, i, 0)),
            pl.BlockSpec((1, 1, tm, d), lambda b, i: (1, b, i, 0)),
            pl.BlockSpec((1, tm, d), lambda b, i: (b, i, OFF_U // d)),
            pl.BlockSpec((1, tm, d), lambda b, i: (b, i, OFF_GATES // d)),
            pl.BlockSpec((1, tm, d), lambda b, i: (b, i, OFF_GATES // d + 1)),
            pl.BlockSpec((1, tm, d), lambda b, i: (b, i, OFF_GATES // d + 2)),
            tok(),
            pl.BlockSpec(mod.shape, lambda b, i: (0, 0)),
            pl.BlockSpec((3, d, d), lambda b, i: (0, 0, 0)),
            pl.BlockSpec((d, d), lambda b, i: (0, 0)),
            pl.BlockSpec((1, d), lambda b, i: (0, 0)),
            pl.BlockSpec((d, d), lambda b, i: (0, 0)),
            pl.BlockSpec((1, d), lambda b, i: (0, 0)),
        ],
        out_specs=tok(),
        compiler_params=_cparams(("arbitrary", "arbitrary")),
        name="merge",
    )(ygla, yrw, ys5, ys5, p, p, p, p, resid, mod, w_branch.astype(BF16), w_out.astype(BF16), s5_d.reshape(1, d),
      glu_w.astype(BF16), glu_b.reshape(1, d))


def _router_kernel(x_ref, nw_ref, mod_ref, rw_ref, rb_ref, h_ref, comb_ref, combt_ref, rank_ref, rankt_ref,
                   offs_ref, *, tm, n_ctx, nb):
    b = pl.program_id(0)
    i = pl.program_id(1)
    x = x_ref[0]
    xn = x * lax.rsqrt(jnp.mean(x * x, axis=-1, keepdims=True) + NORM_EPS) * nw_ref[...]
    row = i * tm + lax.broadcasted_iota(jnp.int32, (tm, 1), 0)
    is_ctx = row < n_ctx
    sh = _mod_rows(mod_ref, b, nb, 3 * D_MODEL, is_ctx)
    sc = _mod_rows(mod_ref, b, nb, 4 * D_MODEL, is_ctx)
    h = xn * (1.0 + sc) + sh
    h_ref[0] = h.astype(BF16)
    logits = jnp.dot(h, rw_ref[...], precision=HI, preferred_element_type=F32) + rb_ref[...]
    lane = lax.broadcasted_iota(jnp.int32, logits.shape, 1).astype(F32)
    neg = jnp.float32(-jnp.inf)
    cur = logits
    comb = jnp.zeros_like(logits)
    denom = jnp.zeros((tm, 1), F32)
    top = None
    for kk in range(TOP_K):
        m = jnp.max(cur, axis=-1, keepdims=True)
        idx = jnp.min(jnp.where(cur == m, lane, 128.0), axis=-1, keepdims=True)
        sel = lane == idx
        if kk == 0:
            top = m
        e = jnp.exp(m - top)
        comb = comb + jnp.where(sel, e, 0.0)
        denom = denom + e
        cur = jnp.where(sel, neg, cur)
    comb = comb / denom
    comb_ref[0] = comb

    chosen = jnp.where(comb > 0.0, 1.0, 0.0).astype(BF16)
    ri = lax.broadcasted_iota(jnp.int32, (tm, tm), 0)
    ci = lax.broadcasted_iota(jnp.int32, (tm, tm), 1)
    rank = jnp.dot((ci < ri).astype(BF16), chosen, preferred_element_type=F32)
    rank_ref[0] = rank
    rankt_ref[0] = lax.dot_general(chosen, (ri < ci).astype(BF16), (((0,), (0,)), ((), ())),
                                   preferred_element_type=F32)
    eye = (lax.broadcasted_iota(jnp.int32, (128, 128), 0)
           == lax.broadcasted_iota(jnp.int32, (128, 128), 1)).astype(BF16)
    h1 = comb.astype(BF16)
    r1 = comb - h1.astype(F32)
    h2 = r1.astype(BF16)
    h3 = (r1 - h2.astype(F32)).astype(BF16)
    nt = (((1,), (1,)), ((), ()))
    combt_ref[0] = lax.dot_general(eye, h1, nt, preferred_element_type=F32) \
        + lax.dot_general(eye, h2, nt, preferred_element_type=F32) \
        + lax.dot_general(eye, h3, nt, preferred_element_type=F32)
    tile = tm // MOE_TILES
    rows = [rank[j * tile:j * tile + 1, :] for j in range(MOE_TILES)]
    rows.append(rank[tm - 1:tm, :] + chosen[tm - 1:tm, :].astype(F32))
    rows.append(jnp.zeros((16 - len(rows), 128), F32))
    offs_ref[0] = jnp.concatenate(rows, axis=0)


def _router(resid, norm_w, mod, router_w, router_b, n_ctx):
    nb, t, d = resid.shape
    tm = _row_tile(t, 1088)
    nper = t // tm
    nq = nb * nper
    rw = jnp.pad(router_w, ((0, 0), (0, 128 - N_EXPERTS)))
    rb = jnp.pad(router_b, (0, 128 - N_EXPERTS), constant_values=-1e30).reshape(1, 128)
    kern = functools.partial(_router_kernel, tm=tm, n_ctx=n_ctx, nb=nb)

    def qspec(r, c):
        return pl.BlockSpec((1, r, c), lambda b, i: (b * nper + i, 0, 0))

    return pl.pallas_call(
        kern,
        out_shape=(jax.ShapeDtypeStruct((nb, t, d), BF16),
                   jax.ShapeDtypeStruct((nq, tm, 128), F32), jax.ShapeDtypeStruct((nq, 128, tm), F32),
                   jax.ShapeDtypeStruct((nq, tm, 128), F32), jax.ShapeDtypeStruct((nq, 128, tm), F32),
                   jax.ShapeDtypeStruct((nq, 16, 128), F32)),
        grid=(nb, nper),
        in_specs=[
            pl.BlockSpec((1, tm, d), lambda b, i: (b, i, 0)),
            pl.BlockSpec((1, d), lambda b, i: (0, 0)),
            pl.BlockSpec(mod.shape, lambda b, i: (0, 0)),
            pl.BlockSpec((d, 128), lambda b, i: (0, 0)),
            pl.BlockSpec((1, 128), lambda b, i: (0, 0)),
        ],
        out_specs=(pl.BlockSpec((1, tm, d), lambda b, i: (b, i, 0)),
                   qspec(tm, 128), qspec(128, tm), qspec(tm, 128), qspec(128, tm), qspec(16, 128)),
        compiler_params=_cparams(("arbitrary", "arbitrary")),
        name="router",
    )(resid, norm_w.reshape(1, d), mod, rw, rb)


MOE_TILES = 8
MOE_FFN_ROWS = 160


def _moe_routed_kernel(off_ref, h_ref, comb_ref, combt_ref, rank_ref, rankt_ref, x_ref, mod_ref, wg_ref, bg_ref,
                       wu_ref, bu_ref, wd_ref, bd_ref, o_ref, g_ref, y_ref, *, qt, nper, nb, n_ctx):
    q = pl.program_id(0)
    e = pl.program_id(1)
    ne = pl.num_programs(1)
    tile = qt // MOE_TILES
    win = tile + 8
    fb = MOE_FFN_ROWS
    dot = functools.partial(jnp.dot, preferred_element_type=F32)

    @pl.when(e == 0)
    def _():
        o_ref[...] = jnp.zeros_like(o_ref)
        g_ref[...] = jnp.zeros_like(g_ref)
        y_ref[...] = jnp.zeros_like(y_ref)

    base = (q * ne + e) * (MOE_TILES + 1)
    wrow = combt_ref[0, pl.ds(e, 1), :]
    rrow = rankt_ref[0, pl.ds(e, 1), :]
    sub_i = lax.broadcasted_iota(jnp.int32, (win, 1), 0)
    sub_f = lax.broadcasted_iota(jnp.int32, (win, tile), 0).astype(F32)
    lane_f = lax.broadcasted_iota(jnp.int32, (tile, win), 1).astype(F32)
    lane_e = lax.broadcasted_iota(jnp.int32, (tile, 128), 1) == e

    for j in range(MOE_TILES):
        off = off_ref[base + j]
        nxt = off_ref[base + j + 1]
        ws = pl.multiple_of((off // 8) * 8, 8)
        rj = rrow[:, j * tile:(j + 1) * tile] - ws.astype(F32)
        wj = wrow[:, j * tile:(j + 1) * tile]
        onehot = jnp.where((sub_f == rj) & (wj > 0.0), 1.0, 0.0).astype(BF16)
        new = dot(onehot, h_ref[0, j * tile:(j + 1) * tile, :])
        slot = sub_i + ws
        hit = (slot >= off) & (slot < nxt)
        g_ref[pl.ds(ws, win), :] = jnp.where(hit, new, g_ref[pl.ds(ws, win), :])

    n = off_ref[base + MOE_TILES]

    def ffn(i, carry):
        rows = pl.ds(pl.multiple_of(i * fb, 8), fb)
        x = g_ref[rows, :].astype(BF16)
        gl = jnp.minimum(dot(x, wg_ref[0]) + bg_ref[0], SWIGLU_LIMIT)
        li = jnp.clip(dot(x, wu_ref[0]) + bu_ref[0], -SWIGLU_LIMIT, SWIGLU_LIMIT)
        act = gl * jax.nn.sigmoid(SWIGLU_ALPHA * gl) * (li + 1.0)
        y_ref[rows, :] = dot(act.astype(BF16), wd_ref[0]) + bd_ref[0]
        return carry

    lax.fori_loop(0, (n + fb - 1) // fb, ffn, 0)

    for j in range(MOE_TILES):
        off = off_ref[base + j]
        ws = pl.multiple_of((off // 8) * 8, 8)
        rows = slice(j * tile, (j + 1) * tile)
        wcol = jnp.sum(jnp.where(lane_e, comb_ref[0, rows, :], 0.0), axis=-1, keepdims=True)
        rcol = jnp.sum(jnp.where(lane_e, rank_ref[0, rows, :], 0.0), axis=-1, keepdims=True)
        onehot = jnp.where((lane_f == rcol - ws.astype(F32)) & (wcol > 0.0), 1.0, 0.0).astype(BF16)
        contrib = dot(onehot, y_ref[pl.ds(ws, win), :].astype(BF16))
        o_ref[0, rows, :] += wcol * contrib

    @pl.when(e == ne - 1)
    def _():
        b = q // nper
        row = (q - b * nper) * qt + lax.broadcasted_iota(jnp.int32, (qt, 1), 0)
        gate = _mod_rows(mod_ref, b, nb, 5 * D_MODEL, row < n_ctx)
        o_ref[0] = x_ref[0] + gate * o_ref[0]


def _moe_routed(h, combq, combt, rank, rankt, offs, resid, mod, wg, bg, wu, bu, wd, bd, n_ctx):
    nb, t, d = h.shape
    nq, qt, _ = combq.shape
    assert qt % (8 * MOE_TILES) == 0
    ne = N_EXPERTS
    off = offs[:, :MOE_TILES + 1, :ne].astype(jnp.int32).transpose(0, 2, 1).reshape(-1)
    gr = qt + MOE_FFN_ROWS + 16
    kern = functools.partial(_moe_routed_kernel, qt=qt, nper=t // qt, nb=nb, n_ctx=n_ctx)

    def wspec():
        return pl.BlockSpec((1, d, d), lambda q, e, off: (e, 0, 0))

    def bspec():
        return pl.BlockSpec((1, 1, d), lambda q, e, off: (e, 0, 0))

    def tokspec(w):
        return pl.BlockSpec((1, qt, w), lambda q, e, off: (q, 0, 0))

    def tspec():
        return pl.BlockSpec((1, 128, qt), lambda q, e, off: (q, 0, 0))

    out = pl.pallas_call(
        kern,
        out_shape=jax.ShapeDtypeStruct((nq, qt, d), F32),
        grid_spec=pltpu.PrefetchScalarGridSpec(
            num_scalar_prefetch=1,
            grid=(nq, ne),
            in_specs=[tokspec(d), tokspec(128), tspec(), tokspec(128), tspec(), tokspec(d),
                      pl.BlockSpec(mod.shape, lambda q, e, off: (0, 0)),
                      wspec(), bspec(), wspec(), bspec(), wspec(), bspec()],
            out_specs=tokspec(d),
            scratch_shapes=[pltpu.VMEM((gr, d), F32), pltpu.VMEM((gr, d), F32)]),
        compiler_params=_cparams(("arbitrary", "arbitrary")),
        name="moe_routed",
    )(off, h.reshape(nq, qt, d), combq, combt, rank, rankt, resid.reshape(nq, qt, d), mod,
      wg.astype(BF16), bg.reshape(ne, 1, d),
      wu.astype(BF16), bu.reshape(ne, 1, d), wd.astype(BF16), bd.reshape(ne, 1, d))
    return out.reshape(nb, t, d)


def _final_norm_kernel(x_ref, w_ref, o_ref):
    x = x_ref[0]
    o_ref[0] = x * lax.rsqrt(jnp.mean(x * x, axis=-1, keepdims=True) + NORM_EPS) * w_ref[...]


def _final_norm(resid, w, n_ctx):
    nb, t, d = resid.shape
    tm = n_ctx
    return pl.pallas_call(
        _final_norm_kernel,
        out_shape=jax.ShapeDtypeStruct((nb, t - n_ctx, d), F32),
        grid=(nb, (t - n_ctx) // tm),
        in_specs=[pl.BlockSpec((1, tm, d), lambda b, i: (b, i + 1, 0)),
                  pl.BlockSpec((1, d), lambda b, i: (0, 0))],
        out_specs=pl.BlockSpec((1, tm, d), lambda b, i: (b, i, 0)),
        compiler_params=_cparams(("arbitrary", "arbitrary")),
        name="final_norm",
    )(resid, w.reshape(1, d))


def _pack_w_in(w):
    o_gg = CONV_CH + GLA_V
    o_rg = o_gg + 2 * GLA_GATE_RANK
    o_rw = o_rg + RWKV_G_RANK
    o_ra = o_rw + 2 * RWKV_W_RANK
    o_u = o_ra + 2 * RWKV_A_RANK
    o_gt = o_u + D_MODEL
    pad = jnp.zeros((w.shape[0], 128 - 2 * GLA_GATE_RANK), w.dtype)
    return jnp.concatenate([w[:, :o_gg], w[:, o_u:o_gt], w[:, o_gt:], w[:, o_gg:o_rg], pad,
                            w[:, o_rg:o_rw], w[:, o_rw:o_ra], w[:, o_ra:o_u]], axis=1).astype(BF16)


def kernel(x, c, ctx, c_ctx, w_ada, b_ada, norm_mix, w_in, conv_w, gla_wg2, gla_bg, gla_norm, rw_w0, rw_w2, rw_a0, rw_a2, rw_g2, rw_kk, rw_ka, rw_rk, rw_ln_g, rw_ln_b, s5_lam_re, s5_lam_im, s5_log_dt, s5_b_re, s5_b_im, s5_c_re, s5_c_im, s5_d, s5_glu_w, s5_glu_b, w_branch, w_out, norm_ffn, router_w, router_b, exp_w_gate, exp_b_gate, exp_w_up, exp_b_up, exp_w_down, exp_b_down, final_norm):
    depth = w_ada.shape[0]
    nb, _, d = x.shape
    n_ctx = ctx.shape[1]
    assert d == D_MODEL and n_ctx % RWKV_SEG == 0 and x.shape[1] % RWKV_SEG == 0 and nb < MOD_ROWS

    resid = jnp.concatenate([ctx, x], axis=1).astype(F32)
    cc = jnp.concatenate([c, c_ctx[None], jnp.zeros((MOD_ROWS - nb - 1, d), F32)], axis=0)
    mods = _ada_table(cc, w_ada, b_ada)

    for l in range(depth):
        mod = mods[l]
        p = _norm_in_proj(resid, norm_mix[l], mod, _pack_w_in(w_in[l]), n_ctx)
        pcv = _short_conv(p, conv_w[l], n_ctx)
        y_gla = _gla_mixer(pcv, p, gla_wg2[l], gla_bg[l], gla_norm[l], n_ctx)
        prm = dict(w0=rw_w0[l], w2=rw_w2[l], a0=rw_a0[l], a2=rw_a2[l], g2=rw_g2[l], kk=rw_kk[l], ka=rw_ka[l],
                   rk=rw_rk[l].reshape(-1), ln_g=rw_ln_g[l], ln_b=rw_ln_b[l])
        y_rw = _rwkv_mixer(pcv, p, prm, n_ctx)
        ops = _s5_operators(s5_lam_re[l], s5_lam_im[l], s5_log_dt[l], s5_b_re[l], s5_b_im[l],
                            s5_c_re[l], s5_c_im[l])
        y_s5 = _s5_mixer(p, ops, n_ctx)
        resid = _merge(y_gla, y_rw, y_s5, p, resid, mod, w_branch[l], w_out[l], s5_d[l], s5_glu_w[l],
                       s5_glu_b[l], n_ctx)
        h, *routing = _router(resid, norm_ffn[l], mod, router_w[l], router_b[l], n_ctx)
        resid = _moe_routed(h, *routing, resid, mod, exp_w_gate[l], exp_b_gate[l], exp_w_up[l], exp_b_up[l],
                            exp_w_down[l], exp_b_down[l], n_ctx)
    return _final_norm(resid, final_norm, n_ctx)
```

```python
import functools
import math

import jax
import jax.numpy as jnp
from jax import lax
from jax.experimental import pallas as pl
from jax.experimental.pallas import tpu as pltpu

F32 = jnp.float32
BF16 = jnp.bfloat16
HI = lax.Precision.HIGHEST

D_MODEL = 1024
GRID_W = 64

GLA_HEADS = 4
GLA_DK = 128
GLA_DV = 256
GLA_QK = GLA_HEADS * GLA_DK
GLA_V = GLA_HEADS * GLA_DV
GLA_GATE_RANK = 16
GLA_GATE_NORM = 16.0
CHUNK = 64
GLA_GROUP = 4

RWKV_HEAD = 64
RWKV_DIM = 1024
RWKV_W_RANK = 64
RWKV_A_RANK = 64
RWKV_G_RANK = 128
RWKV_W_SCALE = math.exp(-0.5)
RWKV_LN_EPS = 64e-5
RWKV_LANES = 256
RWKV_HPB = RWKV_LANES // RWKV_HEAD
RWKV_SEG = 256

S5_GROUP_CH = 16
S5_GROUPS = 64
S5_STATE = 64
S5_CHUNK = 16
S5_LANE_GROUPS = 8
MOD_ROWS = 8

N_EXPERTS = 32
TOP_K = 4
SWIGLU_LIMIT = 7.0
SWIGLU_ALPHA = 1.702
NORM_EPS = 1e-6

GLA_CONV = 2 * GLA_QK + GLA_V
CONV_CH = GLA_CONV + 3 * RWKV_DIM

OFF_PC = 0
OFF_OG = CONV_CH
OFF_U = OFF_OG + GLA_V
OFF_GATES = OFF_U + D_MODEL
OFF_GG = OFF_GATES + 3 * D_MODEL
OFF_RG = OFF_GG + 128
OFF_RW = OFF_RG + 128
OFF_RA = OFF_RW + 128
NP = OFF_RA + 128

VMEM_LIMIT = 56 * 1024 * 1024


def _cparams(sem):
    return pltpu.CompilerParams(dimension_semantics=sem, vmem_limit_bytes=VMEM_LIMIT)


def _row_tile(t, cap):
    best = 8
    for d in range(8, min(t, cap) + 1, 8):
        if t % d == 0:
            best = d
    return best


def _silu(x):
    return x * jax.nn.sigmoid(x)


def _log_sigmoid(z):
    return jnp.minimum(z, 0.0) - jnp.log(1.0 + jnp.exp(-jnp.abs(z)))


def _tri_matmul(keep, x):
    n = x.shape[1]
    h1 = x.astype(BF16)
    r1 = x - h1.astype(F32)
    h2 = r1.astype(BF16)
    h3 = (r1 - h2.astype(F32)).astype(BF16)
    y = jnp.dot(keep.astype(BF16), jnp.concatenate([h1, h2, h3], axis=1), preferred_element_type=F32)
    return y[:, 0:n] + y[:, n:2 * n] + y[:, 2 * n:3 * n]


def _row_to_col(row):
    return jnp.transpose(jnp.broadcast_to(row, (128, row.shape[1])))[:, 0:1]


def _mod_rows(mod_ref, b, nb, off, is_ctx):
    vb = mod_ref[pl.ds(b, 1), off:off + D_MODEL]
    vc = mod_ref[nb:nb + 1, off:off + D_MODEL]
    return jnp.where(is_ctx, vc, vb)


def _ada_kernel(c_ref, w_ref, b_ref, o_ref):
    a = _silu(c_ref[...])
    o_ref[0] = jnp.dot(a, w_ref[0], precision=HI, preferred_element_type=F32) + b_ref[0]


def _ada_table(cc, w_ada, b_ada):
    depth, d, n = w_ada.shape
    tn = 1536
    return pl.pallas_call(
        _ada_kernel,
        out_shape=jax.ShapeDtypeStruct((depth, cc.shape[0], n), F32),
        grid=(depth, n // tn),
        in_specs=[
            pl.BlockSpec((cc.shape[0], d), lambda l, j: (0, 0)),
            pl.BlockSpec((1, d, tn), lambda l, j: (l, 0, j)),
            pl.BlockSpec((1, 1, tn), lambda l, j: (l, 0, j)),
        ],
        out_specs=pl.BlockSpec((1, cc.shape[0], tn), lambda l, j: (l, 0, j)),
        compiler_params=_cparams(("arbitrary", "arbitrary")),
        name="ada_table",
    )(cc, w_ada, b_ada.reshape(depth, 1, n))


def _norm_mm_kernel(x_ref, nw_ref, mod_ref, w_ref, o_ref, h_ref, *, tm, n_ctx, nb):
    b = pl.program_id(0)
    i = pl.program_id(1)

    @pl.when(pl.program_id(2) == 0)
    def _():
        x = x_ref[0]
        xn = x * lax.rsqrt(jnp.mean(x * x, axis=-1, keepdims=True) + NORM_EPS) * nw_ref[...]
        row = i * tm + lax.broadcasted_iota(jnp.int32, (tm, 1), 0)
        is_ctx = row < n_ctx
        sh = _mod_rows(mod_ref, b, nb, 0, is_ctx)
        sc = _mod_rows(mod_ref, b, nb, D_MODEL, is_ctx)
        h_ref[...] = (xn * (1.0 + sc) + sh).astype(BF16)

    o_ref[0] = jnp.dot(h_ref[...], w_ref[...], preferred_element_type=F32)


def _norm_in_proj(resid, norm_w, mod, w_in_p, n_ctx):
    nb, t, d = resid.shape
    tm = _row_tile(t, 1088)
    tn = 1536
    kern = functools.partial(_norm_mm_kernel, tm=tm, n_ctx=n_ctx, nb=nb)
    return pl.pallas_call(
        kern,
        out_shape=jax.ShapeDtypeStruct((nb, t, NP), F32),
        grid=(nb, t // tm, NP // tn),
        in_specs=[
            pl.BlockSpec((1, tm, d), lambda b, i, j: (b, i, 0)),
            pl.BlockSpec((1, d), lambda b, i, j: (0, 0)),
            pl.BlockSpec(mod.shape, lambda b, i, j: (0, 0)),
            pl.BlockSpec((d, tn), lambda b, i, j: (0, j)),
        ],
        out_specs=pl.BlockSpec((1, tm, tn), lambda b, i, j: (b, i, j)),
        scratch_shapes=[pltpu.VMEM((tm, d), BF16)],
        compiler_params=_cparams(("arbitrary", "arbitrary", "arbitrary")),
        name="norm_in_proj",
    )(resid, norm_w.reshape(1, d), mod, w_in_p)


def _conv_kernel(x_ref, w_ref, o_ref, xp_ref, xl_ref, xr_ref, *, n_ctx, rows):
    tc = x_ref.shape[-1]
    gw = GRID_W

    def wrow(k):
        return w_ref[k:k + 1, :]

    xc = x_ref[0, 0:n_ctx, :]
    ci = lax.broadcasted_iota(jnp.int32, (n_ctx, tc), 0)
    left = jnp.where(ci > 0, pltpu.roll(xc, 1, 0), 0.0)
    right = jnp.where(ci < n_ctx - 1, pltpu.roll(xc, n_ctx - 1, 0), 0.0)
    o_ref[0, 0:n_ctx, :] = left * wrow(3) + xc * wrow(4) + right * wrow(5)

    zeros = jnp.zeros((gw, tc), F32)
    for ref in (xp_ref, xl_ref, xr_ref):
        ref[0:gw, :] = zeros
        ref[(rows + 1) * gw:(rows + 2) * gw, :] = zeros
    wi = lax.broadcasted_iota(jnp.int32, (gw, tc), 0)

    def fill(r, carry):
        base = pl.multiple_of(r * gw, gw)
        xrow = x_ref[0, pl.ds(n_ctx + base, gw), :]
        xp_ref[pl.ds(gw + base, gw), :] = xrow
        xl_ref[pl.ds(gw + base, gw), :] = jnp.where(wi > 0, pltpu.roll(xrow, 1, 0), 0.0)
        xr_ref[pl.ds(gw + base, gw), :] = jnp.where(wi < gw - 1, pltpu.roll(xrow, gw - 1, 0), 0.0)
        return carry

    lax.fori_loop(0, rows, fill, 0)

    def comp(r, carry):
        base = pl.multiple_of(r * gw, gw)
        acc = jnp.zeros((gw, tc), F32)
        for dr in range(3):
            sl = pl.ds(base + gw * dr, gw)
            acc = acc + xl_ref[sl, :] * wrow(3 * dr) + xp_ref[sl, :] * wrow(3 * dr + 1) \
                + xr_ref[sl, :] * wrow(3 * dr + 2)
        o_ref[0, pl.ds(n_ctx + base, gw), :] = acc
        return carry

    lax.fori_loop(0, rows, comp, 0)


def _short_conv(p, conv_w, n_ctx):
    nb, t, _ = p.shape
    rows = (t - n_ctx) // GRID_W
    tc = 256
    kern = functools.partial(_conv_kernel, n_ctx=n_ctx, rows=rows)
    pad_rows = (rows + 2) * GRID_W
    return pl.pallas_call(
        kern,
        out_shape=jax.ShapeDtypeStruct((nb, t, CONV_CH), F32),
        grid=(nb, CONV_CH // tc),
        in_specs=[
            pl.BlockSpec((1, t, tc), lambda b, j: (b, 0, j)),
            pl.BlockSpec((9, tc), lambda b, j: (0, j)),
        ],
        out_specs=pl.BlockSpec((1, t, tc), lambda b, j: (b, 0, j)),
        scratch_shapes=[pltpu.VMEM((pad_rows, tc), F32)] * 3,
        compiler_params=_cparams(("arbitrary", "arbitrary")),
        name="short_conv",
    )(p, conv_w.reshape(9, CONV_CH))


def _gla_kernel(q_ref, k_ref, v_ref, gg_ref, og_ref, wg_ref, bg_ref, ng_ref, o_ref, of_ref, ob_ref, *, n_ctx, t):
    c = CHUNK
    nch = t // c
    nctx = n_ctx // c
    ii = lax.broadcasted_iota(jnp.int32, (c, c), 0)
    jj = lax.broadcasted_iota(jnp.int32, (c, c), 1)
    scale = GLA_DK ** -0.5

    nt = (((1,), (1,)), ((), ()))
    tn = (((0,), (0,)), ((), ()))
    dot = functools.partial(jnp.dot, preferred_element_type=F32)
    grp = GLA_GROUP
    assert nctx % grp == 0 and nch % grp == 0

    def group(g, carry):
        chains = []
        for d in range(2):
            for u in range(grp):
                i = g * grp + u
                ci = i if d == 0 else jnp.where(i < nctx, nctx - 1 - i, nch + nctx - 1 - i)
                chains.append((d, pl.ds(pl.multiple_of(ci * c, c), c)))
        keeps = [(ii >= jj) if d == 0 else (ii <= jj) for d, _ in chains]
        q = [_silu(q_ref[0, r, :]) * scale for _, r in chains]
        k = [_silu(k_ref[0, r, :]) for _, r in chains]
        vb = [_silu(v_ref[0, r, :]).astype(BF16) for _, r in chains]
        z = [dot(gg_ref[0, r, :][:, GLA_GATE_RANK * d:GLA_GATE_RANK * (d + 1)], wg_ref[d]) + bg_ref[d]
             for d, r in chains]
        la = [_log_sigmoid(x) * (1.0 / GLA_GATE_NORM) for x in z]
        bcum = [_tri_matmul(kp, x) for kp, x in zip(keeps, la)]
        blast = [b[c - 1:c, :] if d == 0 else b[0:1, :] for (d, _), b in zip(chains, bcum)]
        q_dec = [(x * jnp.exp(b)).astype(BF16) for x, b in zip(q, bcum)]
        k_inv = [(x * jnp.exp(-b)).astype(BF16) for x, b in zip(k, bcum)]
        k_end = [(x * jnp.exp(bl - b)).astype(BF16) for x, b, bl in zip(k, bcum, blast)]
        sc = [lax.dot_general(a, b, nt, preferred_element_type=F32) for a, b in zip(q_dec, k_inv)]
        sc = [jnp.where(kp, x, 0.0).astype(BF16) for kp, x in zip(keeps, sc)]
        intra = [dot(a, b) for a, b in zip(sc, vb)]
        kv = [lax.dot_general(a, b, tn, preferred_element_type=F32) for a, b in zip(k_end, vb)]
        dec = [_row_to_col(jnp.exp(bl)) for bl in blast]
        states = list(carry)
        for n, (d, r) in enumerate(chains):
            s = states[d]
            o = intra[n] + dot(q_dec[n], s.astype(BF16))
            if d == 0:
                of_ref[r, :] = o
            else:
                ob_ref[r, :] = o
            states[d] = dec[n] * s + kv[n]
        return tuple(states)

    s0 = jnp.zeros((GLA_DK, GLA_DV), F32)
    lax.fori_loop(0, nch // grp, group, (s0, s0))

    def finish(ci, carry):
        rows = pl.ds(pl.multiple_of(ci * c, c), c)
        o = of_ref[rows, :] + ob_ref[rows, :]
        o = o * lax.rsqrt(jnp.mean(o * o, axis=-1, keepdims=True) + 1e-5) * ng_ref[...]
        o_ref[0, rows, :] = (o * _silu(og_ref[0, rows, :])).astype(BF16)
        return carry

    lax.fori_loop(0, nch, finish, 0, unroll=2)


def _gla_mixer(pcv, p, wg2, bg, norm_g, n_ctx):
    nb, t, _ = pcv.shape
    kern = functools.partial(_gla_kernel, n_ctx=n_ctx, t=t)
    return pl.pallas_call(
        kern,
        out_shape=jax.ShapeDtypeStruct((nb, t, GLA_V), BF16),
        grid=(nb, GLA_HEADS),
        in_specs=[
            pl.BlockSpec((1, t, GLA_DK), lambda b, h: (b, 0, h)),
            pl.BlockSpec((1, t, GLA_DK), lambda b, h: (b, 0, GLA_HEADS + h)),
            pl.BlockSpec((1, t, GLA_DV), lambda b, h: (b, 0, GLA_HEADS + h)),
            pl.BlockSpec((1, t, 128), lambda b, h: (b, 0, OFF_GG // 128)),
            pl.BlockSpec((1, t, GLA_DV), lambda b, h: (b, 0, OFF_OG // GLA_DV + h)),
            pl.BlockSpec((2, GLA_GATE_RANK, GLA_DK), lambda b, h: (0, 0, h)),
            pl.BlockSpec((2, 1, GLA_DK), lambda b, h: (0, 0, h)),
            pl.BlockSpec((1, GLA_DV), lambda b, h: (0, 0)),
        ],
        out_specs=pl.BlockSpec((1, t, GLA_DV), lambda b, h: (b, 0, h)),
        scratch_shapes=[pltpu.VMEM((t, GLA_DV), F32)] * 2,
        compiler_params=_cparams(("arbitrary", "arbitrary")),
        name="gla_mixer",
    )(pcv, pcv, pcv, p, p, wg2, bg.reshape(2, 1, GLA_QK), norm_g.reshape(1, GLA_DV))


def _bd_mask():
    shift = RWKV_HEAD.bit_length() - 1
    r = lax.broadcasted_iota(jnp.int32, (RWKV_LANES, RWKV_LANES), 0) >> shift
    c = lax.broadcasted_iota(jnp.int32, (RWKV_LANES, RWKV_LANES), 1) >> shift
    return r == c


def _bd(x, mask):
    xb = x.astype(BF16)
    return jnp.where(mask, jnp.concatenate([xb] * RWKV_HPB, axis=0), jnp.zeros((), BF16))


def _head_sum(x, ones_bd):
    hi = x.astype(BF16)
    lo = (x - hi.astype(F32)).astype(BF16)
    return jnp.dot(hi, ones_bd, preferred_element_type=F32) + jnp.dot(lo, ones_bd, preferred_element_type=F32)


def _rwkv_group_kernel(rf_ref, kf_ref, vf_ref, rwf_ref, raf_ref, rb_ref, kb_ref, vb_ref, rwb_ref, rab_ref,
                       w0_ref, w2_ref, a0_ref, a2_ref, kkp_ref, kap_ref, yf_ref, yb_ref, stf_ref, stb_ref):
    c = CHUNK
    nchunk = RWKV_SEG // c
    mask_bd = _bd_mask()
    ones_bd = mask_bd.astype(BF16)
    dot = functools.partial(jnp.dot, preferred_element_type=F32)
    nt = (((1,), (1,)), ((), ()))
    tn = (((0,), (0,)), ((), ()))

    @pl.when(pl.program_id(2) == 0)
    def _():
        stf_ref[...] = jnp.zeros_like(stf_ref)
        stb_ref[...] = jnp.zeros_like(stb_ref)

    chains = [(0, slice(u * c, (u + 1) * c)) for u in range(nchunk)] \
        + [(1, slice(u * c, (u + 1) * c)) for u in range(nchunk - 1, -1, -1)]
    refs = ((rf_ref, kf_ref, vf_ref, rwf_ref, raf_ref), (rb_ref, kb_ref, vb_ref, rwb_ref, rab_ref))

    def each(f, *lists):
        return [f(*xs) for xs in zip(*lists)]

    ds = [d for d, _ in chains]
    r = [refs[d][0][0, rows, :] for d, rows in chains]
    k = [refs[d][1][0, rows, :] for d, rows in chains]
    v = [refs[d][2][0, rows, :] for d, rows in chains]
    pw = [refs[d][3][0, rows, :][:, RWKV_W_RANK * d:RWKV_W_RANK * (d + 1)] for d, rows in chains]
    pa = [refs[d][4][0, rows, :][:, RWKV_A_RANK * d:RWKV_A_RANK * (d + 1)] for d, rows in chains]
    lw = each(lambda d, x: w0_ref[d] + dot(jnp.tanh(x), w2_ref[d]), ds, pw)
    logw = [-RWKV_W_SCALE * jax.nn.sigmoid(x) for x in lw]
    la = each(lambda d, x: a0_ref[d] + dot(x, a2_ref[d]), ds, pa)
    a = [jax.nn.sigmoid(x) for x in la]
    kkv = [x * kkp_ref[...] for x in k]
    ss = [dot((x * x).astype(BF16), ones_bd) for x in kkv]
    kk = each(lambda x, s: x / jnp.maximum(jnp.sqrt(s), 1e-12), kkv, ss)
    kd = each(lambda x, y: x * (1.0 + (y - 1.0) * kap_ref[...]), k, a)
    kka = each(lambda x, y: x * y, kk, a)

    ii = lax.broadcasted_iota(jnp.int32, (c, c), 0)
    jj = lax.broadcasted_iota(jnp.int32, (c, c), 1)
    ti = lax.broadcasted_iota(jnp.int32, (c, RWKV_LANES), 0)
    si = lax.broadcasted_iota(jnp.int32, (c, RWKV_LANES), 1) & (c - 1)
    keep_sq = [(ii >= jj) if d == 0 else (ii <= jj) for d in ds]
    incl = [(si <= ti) if d == 0 else (si >= ti) for d in ds]
    strict = [(si < ti) if d == 0 else (si > ti) for d in ds]
    eye = (si == ti).astype(F32)

    cb = each(_tri_matmul, keep_sq, logw)
    cblast = each(lambda d, x: x[c - 1:c, :] if d == 0 else x[0:1, :], ds, cb)
    ad = each(lambda x, b, w: -x * jnp.exp(b - w), kk, cb, logw)
    rd = each(lambda x, b: x * jnp.exp(b), r, cb)
    einv = [jnp.exp(-b) for b in cb]
    bi = each(lambda x, e: x * e, kka, einv)
    ki = each(lambda x, e: x * e, kd, einv)
    eend = each(lambda bl, b: jnp.exp(bl - b), cblast, cb)
    bend = each(lambda x, e: x * e, kka, eend)
    kend = each(lambda x, e: x * e, kd, eend)

    def bd(x):
        return _bd(x, mask_bd)

    def mm(x, y_bd):
        return dot(x.astype(BF16), y_bd)

    lhs = each(lambda x, y: jnp.concatenate([x, y], axis=0).astype(BF16), ad, rd)
    pb = each(lambda x, y: lax.dot_general(x, bd(y), nt, preferred_element_type=F32), lhs, bi)
    pk = each(lambda x, y: lax.dot_general(x, bd(y), nt, preferred_element_type=F32), lhs, ki)
    a_ab = each(lambda m, x: jnp.where(m, x[0:c], 0.0), strict, pb)
    r_ab = each(lambda m, x: jnp.where(m, x[c:2 * c], 0.0), incl, pb)
    a_ak = each(lambda m, x: jnp.where(m, x[0:c], 0.0), strict, pk)
    r_ak = each(lambda m, x: jnp.where(m, x[c:2 * c], 0.0), incl, pk)

    tinv = [eye + x for x in a_ab]
    pwr = each(lambda x: mm(x, bd(x)), a_ab)
    for lvl in range(5):
        if lvl < 4:
            both = each(lambda t_, p_: mm(jnp.concatenate([t_, p_], axis=0), bd(p_)), tinv, pwr)
            tinv = each(lambda t_, b: t_ + b[0:c], tinv, both)
            pwr = [b[c:2 * c] for b in both]
        else:
            tinv = each(lambda t_, p_: t_ + mm(t_, bd(p_)), tinv, pwr)

    ln = RWKV_LANES
    akv = each(lambda x, y, v_: mm(jnp.concatenate([x, y], axis=0), bd(v_)), a_ak, r_ak, v)
    mu = each(lambda t_, x, av: mm(t_, jnp.concatenate([bd(x), bd(av[0:c])], axis=1)), tinv, ad, akv)
    m1 = [x[:, 0:ln] for x in mu]
    u0 = [x[:, ln:2 * ln] for x in mu]
    ry = each(lambda rab, m, u_: mm(rab, jnp.concatenate([bd(m), bd(u_)], axis=1)), r_ab, m1, u0)
    r1 = each(lambda x, z: (x + z[:, 0:ln]).astype(BF16), rd, ry)
    y0 = each(lambda z, av: z[:, ln:2 * ln] + av[c:2 * c], ry, akv)
    g1 = each(lambda x, m: jnp.where(mask_bd, lax.dot_general(x.astype(BF16), m.astype(BF16), tn,
                                                              preferred_element_type=F32), 0.0).astype(BF16),
              bend, m1)
    h = each(lambda x, y, u_, v_: jnp.where(mask_bd, lax.dot_general(
        jnp.concatenate([x, y], axis=0).astype(BF16), jnp.concatenate([u_, v_], axis=0).astype(BF16), tn,
        preferred_element_type=F32), 0.0), bend, kend, u0, v)
    dec = [_row_to_col(jnp.exp(x)) for x in cblast]

    states = [stf_ref[...], stb_ref[...]]
    outs = (yf_ref, yb_ref)
    for n, (d, rows) in enumerate(chains):
        st = states[d]
        stb = st.astype(BF16)
        outs[d][0, rows, :] = dot(r1[n], stb) + y0[n]
        states[d] = dec[n] * st + dot(g1[n], stb) + h[n]
    stf_ref[...] = states[0]
    stb_ref[...] = states[1]


def _rwkv_post_kernel(r_ref, k_ref, v_ref, ra_ref, rg_ref, yf_ref, yb_ref, a0_ref, a2_ref, g2_ref,
                      kap_ref, rkp_ref, lng_ref, lnb_ref, o_ref):
    ones_bd = _bd_mask().astype(BF16)
    r = r_ref[0]
    k = k_ref[0]
    ra = ra_ref[0]
    y = yf_ref[0] + yb_ref[0]
    inv_n = 1.0 / RWKV_HEAD
    mu = _head_sum(y, ones_bd) * inv_n
    yc = y - mu
    var = _head_sum(yc * yc, ones_bd) * inv_n
    yn = yc * lax.rsqrt(var + RWKV_LN_EPS) * lng_ref[...] + lnb_ref[...]
    kds = jnp.zeros_like(k)
    for d in range(2):
        la = a0_ref[d] + jnp.dot(ra[:, RWKV_A_RANK * d:RWKV_A_RANK * (d + 1)], a2_ref[d],
                                 preferred_element_type=F32)
        kds = kds + k * (1.0 + (jax.nn.sigmoid(la) - 1.0) * kap_ref[...])
    bonus = _head_sum(r * kds * rkp_ref[...], ones_bd) * v_ref[0]
    g = jnp.dot(jax.nn.sigmoid(rg_ref[0]), g2_ref[...], preferred_element_type=F32)
    o_ref[0] = ((yn + bonus) * g).astype(BF16)


def _rwkv_mixer(pcv, p, prm, n_ctx):
    nb, t, _ = pcv.shape
    seg = RWKV_SEG
    nseg = t // seg
    nseg_ctx = n_ctx // seg
    ln = RWKV_LANES
    nhb = RWKV_DIM // ln
    cb0 = GLA_CONV // ln

    def fmap(s):
        return s

    def bmap(s):
        return jnp.where(s < nseg_ctx, nseg_ctx - 1 - s, nseg - 1 - (s - nseg_ctx))

    def tok(off_blocks, smap):
        return pl.BlockSpec((1, seg, ln), lambda b, h, s: (b, smap(s), off_blocks + h))

    def small(off, smap):
        return pl.BlockSpec((1, seg, 128), lambda b, h, s: (b, smap(s), off // 128))

    def dirspecs(smap):
        return [tok(cb0, smap), tok(cb0 + nhb, smap), tok(cb0 + 2 * nhb, smap),
                small(OFF_RW, smap), small(OFF_RA, smap)]

    def vec():
        return pl.BlockSpec((1, ln), lambda b, h, s: (0, h))

    w0 = prm["w0"].reshape(2, 1, RWKV_DIM)
    a0 = prm["a0"].reshape(2, 1, RWKV_DIM)
    kkp = prm["kk"].reshape(1, RWKV_DIM)
    kap = prm["ka"].reshape(1, RWKV_DIM)
    yf, yb = pl.pallas_call(
        _rwkv_group_kernel,
        out_shape=(jax.ShapeDtypeStruct((nb, t, RWKV_DIM), F32),) * 2,
        grid=(nb, nhb, nseg),
        in_specs=dirspecs(fmap) + dirspecs(bmap) + [
            pl.BlockSpec((2, 1, ln), lambda b, h, s: (0, 0, h)),
            pl.BlockSpec((2, RWKV_W_RANK, ln), lambda b, h, s: (0, 0, h)),
            pl.BlockSpec((2, 1, ln), lambda b, h, s: (0, 0, h)),
            pl.BlockSpec((2, RWKV_A_RANK, ln), lambda b, h, s: (0, 0, h)),
            vec(), vec()],
        out_specs=(pl.BlockSpec((1, seg, ln), lambda b, h, s: (b, fmap(s), h)),
                   pl.BlockSpec((1, seg, ln), lambda b, h, s: (b, bmap(s), h))),
        scratch_shapes=[pltpu.VMEM((ln, ln), F32)] * 2,
        compiler_params=_cparams(("arbitrary", "arbitrary", "arbitrary")),
        name="rwkv7_scan",
    )(pcv, pcv, pcv, p, p, pcv, pcv, pcv, p, p, w0, prm["w2"], a0, prm["a2"], kkp, kap)

    tm = _row_tile(t, 1088)

    def ptok(off_blocks):
        return pl.BlockSpec((1, tm, ln), lambda b, i, h: (b, i, off_blocks + h))

    def psmall(off):
        return pl.BlockSpec((1, tm, 128), lambda b, i, h: (b, i, off // 128))

    def pvec():
        return pl.BlockSpec((1, ln), lambda b, i, h: (0, h))

    return pl.pallas_call(
        _rwkv_post_kernel,
        out_shape=jax.ShapeDtypeStruct((nb, t, RWKV_DIM), BF16),
        grid=(nb, t // tm, nhb),
        in_specs=[ptok(cb0), ptok(cb0 + nhb), ptok(cb0 + 2 * nhb), psmall(OFF_RA), psmall(OFF_RG),
                  ptok(0), ptok(0),
                  pl.BlockSpec((2, 1, ln), lambda b, i, h: (0, 0, h)),
                  pl.BlockSpec((2, RWKV_A_RANK, ln), lambda b, i, h: (0, 0, h)),
                  pl.BlockSpec((RWKV_G_RANK, ln), lambda b, i, h: (0, h)),
                  pvec(), pvec(), pvec(), pvec()],
        out_specs=ptok(0),
        compiler_params=_cparams(("arbitrary", "arbitrary", "arbitrary")),
        name="rwkv7_post",
    )(pcv, pcv, pcv, p, p, yf, yb, a0, prm["a2"], prm["g2"], kap, prm["rk"].reshape(1, RWKV_DIM),
      prm["ln_g"].reshape(1, RWKV_DIM), prm["ln_b"].reshape(1, RWKV_DIM))


def _s5_operators(lam_re, lam_im, log_dt, b_re, b_im, c_re, c_im):
    g, p_, lc = S5_GROUPS, S5_STATE, S5_CHUNK
    lr, li = lam_re.astype(F32), lam_im.astype(F32)
    br, bi = b_re.astype(F32), b_im.astype(F32)
    den = lr * lr + li * li
    n = jnp.arange(lc + 1, dtype=F32)[:, None, None]
    out = []
    for d in range(2):
        dt = jnp.exp(log_dt[d].astype(F32))[:, None]
        mag = jnp.exp(n * (lr * dt)[None])
        ang = n * (li * dt)[None]
        pr, pi = mag * jnp.cos(ang), mag * jnp.sin(ang)
        m1 = jnp.exp(lr * dt)
        ar, ai = m1 * jnp.cos(li * dt), m1 * jnp.sin(li * dt)
        fr = ((ar - 1.0) * lr + ai * li) / den
        fi = (ai * lr - (ar - 1.0) * li) / den
        bb_r = fr[..., None] * br - fi[..., None] * bi
        bb_i = fr[..., None] * bi + fi[..., None] * br
        cr, ci = c_re[d].astype(F32), c_im[d].astype(F32)
        ca_r = cr[None] * pr[:, :, None, :] - ci[None] * pi[:, :, None, :]
        ca_i = cr[None] * pi[:, :, None, :] + ci[None] * pr[:, :, None, :]
        nlg, gl, ch = S5_LANE_GROUPS, S5_GROUPS // S5_LANE_GROUPS, S5_GROUP_CH

        def bbt(x):
            return x.reshape(nlg, gl, p_, ch).transpose(0, 3, 1, 2).reshape(nlg, ch, gl * p_)

        e_s = (lc - 1 - jnp.arange(lc)) if d == 0 else jnp.arange(lc)
        ps_r, ps_i = pr[e_s], pi[e_s]
        bst_r = ps_r[..., None] * bb_r[None] - ps_i[..., None] * bb_i[None]
        bst_i = ps_r[..., None] * bb_i[None] + ps_i[..., None] * bb_r[None]

        def rows_op(x):
            return x.reshape(lc, nlg, gl, p_, ch).transpose(1, 0, 4, 2, 3).reshape(nlg, lc, ch, gl * p_)

        f_t = jnp.concatenate([(jnp.arange(lc) + 1) if d == 0 else (lc - jnp.arange(lc)),
                               jnp.zeros((1,), jnp.int32)])

        def cols_op(x):
            nn = x.shape[0]
            return x.reshape(nn, nlg, gl, ch, p_).transpose(1, 0, 4, 2, 3).reshape(nlg, nn, p_, gl * ch)

        out.append(dict(
            bbt_r=bbt(bb_r), bbt_i=bbt(bb_i), bst_r=rows_op(bst_r), bst_i=rows_op(bst_i),
            co_r=cols_op(ca_r[f_t]), co_i=cols_op(-ca_i[f_t]),
            a_r=pr[lc].reshape(nlg, 1, gl * p_), a_i=pi[lc].reshape(nlg, 1, gl * p_)))
    return {k: jnp.stack([out[0][k], out[1][k]]) for k in out[0]}


def _s5_kernel(u_ref, bbr_ref, bbi_ref, br_ref, bi_ref, cr_ref, ci_ref, ar_ref, ai_ref, o_ref,
               ir_ref, ii_ref, xr_ref, xi_ref, op_ref, bsr_ref, bsi_ref, cor_ref, coi_ref, *, nchunk, nchunk_ctx):
    d = pl.program_id(1)
    lc = S5_CHUNK
    gl = S5_GROUPS // S5_LANE_GROUPS
    dot = functools.partial(jnp.dot, preferred_element_type=F32)

    def expand(x):
        w, n = x.shape
        m = n // gl
        rg = lax.broadcasted_iota(jnp.int32, (gl * w, n), 0) >> (w.bit_length() - 1)
        cg = lax.broadcasted_iota(jnp.int32, (gl * w, n), 1) >> (m.bit_length() - 1)
        return jnp.where(rg == cg, jnp.concatenate([x] * gl, axis=0), 0.0).astype(BF16)

    def build(direction):
        for t in range(lc + 1):
            cor_ref[:, t * 128:(t + 1) * 128] = expand(cr_ref[0, 0, t])
            coi_ref[:, t * 128:(t + 1) * 128] = expand(ci_ref[0, 0, t])
        for t in range(lc):
            bsr_ref[t * 128:(t + 1) * 128, :] = expand(br_ref[0, 0, t])
            bsi_ref[t * 128:(t + 1) * 128, :] = expand(bi_ref[0, 0, t])
        kall = dot(bbr_ref[0, 0].astype(BF16), cor_ref[...]) + dot(bbi_ref[0, 0].astype(BF16), coi_ref[...])

        def lag_block(tau):
            blk = lc if tau == 0 else (tau - 1 if direction == 0 else lc - tau)
            return expand(kall[:, blk * 128:(blk + 1) * 128])

        lags = [lag_block(tau) for tau in range(lc)]
        zero = jnp.zeros((128, 128), BF16)
        for s in range(lc):
            for t in range(lc):
                tau = (t - s) if direction == 0 else (s - t)
                op_ref[s * 128:(s + 1) * 128, t * 128:(t + 1) * 128] = lags[tau] if tau >= 0 else zero

    first = pl.program_id(2) == 0
    pl.when(first & (d == 0))(functools.partial(build, 0))
    pl.when(first & (d == 1))(functools.partial(build, 1))

    nbb = u_ref.shape[0]
    lhs = jnp.concatenate(
        [jnp.concatenate([u_ref[bb, pl.ds(s, nchunk, stride=lc), :] for s in range(lc)], axis=1)
         for bb in range(nbb)], axis=0).astype(BF16)
    ir_ref[...] = dot(lhs, bsr_ref[...])
    ii_ref[...] = dot(lhs, bsi_ref[...])
    ar = ar_ref[0, 0]
    ai = ai_ref[0, 0]

    def step(i, carry):
        rev = jnp.where(i < nchunk_ctx, nchunk_ctx - 1 - i, nchunk + nchunk_ctx - 1 - i)
        ci = jnp.where(d == 0, i, rev)
        out = []
        for bb in range(nbb):
            sr, si = carry[2 * bb], carry[2 * bb + 1]
            row = pl.ds(bb * nchunk + ci, 1)
            xr_ref[row, :] = sr
            xi_ref[row, :] = si
            out.append(ar * sr - ai * si + ir_ref[row, :])
            out.append(ar * si + ai * sr + ii_ref[row, :])
        return tuple(out)

    z = jnp.zeros((1, ir_ref.shape[1]), F32)
    lax.fori_loop(0, nchunk, step, (z,) * (2 * nbb), unroll=8)
    y = dot(lhs, op_ref[...]) + dot(xr_ref[...].astype(BF16), cor_ref[:, 0:lc * 128]) \
        + dot(xi_ref[...].astype(BF16), coi_ref[:, 0:lc * 128])
    for bb in range(nbb):
        for t in range(lc):
            o_ref[0, bb, pl.ds(t, nchunk, stride=lc), :] = y[bb * nchunk:(bb + 1) * nchunk, t * 128:(t + 1) * 128]


def _s5_mixer(p, ops, n_ctx):
    nb, t, _ = p.shape
    lc = S5_CHUNK
    nchunk = t // lc
    nlg = S5_LANE_GROUPS
    ns = ops["a_r"].shape[-1]

    def wspec(x):
        nd = x.ndim - 2
        return pl.BlockSpec((1, 1) + x.shape[2:], lambda g, d, b: (d, g) + (0,) * nd)

    keys = ("bbt_r", "bbt_i", "bst_r", "bst_i", "co_r", "co_i", "a_r", "a_i")
    nop = lc * 128
    kern = functools.partial(_s5_kernel, nchunk=nchunk, nchunk_ctx=n_ctx // lc)
    nbb = 2 if nb % 2 == 0 else 1
    return pl.pallas_call(
        kern,
        out_shape=jax.ShapeDtypeStruct((2, nb, t, D_MODEL), F32),
        grid=(nlg, 2, nb // nbb),
        in_specs=[pl.BlockSpec((nbb, t, 128), lambda g, d, b: (b, 0, OFF_U // 128 + g))]
        + [wspec(ops[k]) for k in keys],
        out_specs=pl.BlockSpec((1, nbb, t, 128), lambda g, d, b: (d, b, 0, g)),
        scratch_shapes=[pltpu.VMEM((nbb * nchunk, ns), F32)] * 4
        + [pltpu.VMEM((nop, nop), BF16), pltpu.VMEM((nop, ns), BF16), pltpu.VMEM((nop, ns), BF16),
           pltpu.VMEM((ns, nop + 128), BF16), pltpu.VMEM((ns, nop + 128), BF16)],
        compiler_params=_cparams(("arbitrary", "arbitrary", "arbitrary")),
        name="s5_mixer",
    )(p, *[ops[k] for k in keys])


def _merge_kernel(ygla_ref, yrw_ref, ys5f_ref, ys5b_ref, u_ref, ga_ref, gb_ref, gc_ref, x_ref, mod_ref, wb_ref, wo_ref,
                  d_ref, gw_ref, gbias_ref, o_ref, *, tm, n_ctx, nb):
    b = pl.program_id(0)
    i = pl.program_id(1)
    dot = functools.partial(jnp.dot, preferred_element_type=F32)
    z = jax.nn.gelu(ys5f_ref[0, 0] + ys5b_ref[0, 0] + d_ref[...] * u_ref[0])
    zs = z * jax.nn.sigmoid(dot(z.astype(BF16), gw_ref[...]) + gbias_ref[...])
    m = jax.nn.sigmoid(ga_ref[0]) * dot(ygla_ref[0].astype(BF16), wb_ref[0]) \
        + jax.nn.sigmoid(gb_ref[0]) * dot(yrw_ref[0].astype(BF16), wb_ref[1]) \
        + jax.nn.sigmoid(gc_ref[0]) * dot(zs.astype(BF16), wb_ref[2])
    mix = dot(m.astype(BF16), wo_ref[...])
    row = i * tm + lax.broadcasted_iota(jnp.int32, (tm, 1), 0)
    gate = _mod_rows(mod_ref, b, nb, 2 * D_MODEL, row < n_ctx)
    o_ref[0] = x_ref[0] + gate * mix


def _merge(ygla, yrw, ys5, p, resid, mod, w_branch, w_out, s5_d, glu_w, glu_b, n_ctx):
    nb, t, d = resid.shape
    tm = _row_tile(t, 272)
    kern = functools.partial(_merge_kernel, tm=tm, n_ctx=n_ctx, nb=nb)

    def tok():
        return pl.BlockSpec((1, tm, d), lambda b, i: (b, i, 0))

    return pl.pallas_call(
        kern,
        out_shape=jax.ShapeDtypeStruct((nb, t, d), F32),
        grid=(nb, t // tm),
        in_specs=[
            tok(), tok(),
            pl.BlockSpec((1, 1, tm, d), lambda b, i: (0, b, i, 0)),
            pl.BlockSpec((1, 1, tm, d), lambda b, i: (1, b, i, 0)),
            pl.BlockSpec((1, tm, d), lambda b, i: (b, i, OFF_U // d)),
            pl.BlockSpec((1, tm, d), lambda b, i: (b, i, OFF_GATES // d)),
            pl.BlockSpec((1, tm, d), lambda b, i: (b, i, OFF_GATES // d + 1)),
            pl.BlockSpec((1, tm, d), lambda b, i: (b, i, OFF_GATES // d + 2)),
            tok(),
            pl.BlockSpec(mod.shape, lambda b, i: (0, 0)),
            pl.BlockSpec((3, d, d), lambda b, i: (0, 0, 0)),
            pl.BlockSpec((d, d), lambda b, i: (0, 0)),
            pl.BlockSpec((1, d), lambda b, i: (0, 0)),
            pl.BlockSpec((d, d), lambda b, i: (0, 0)),
            pl.BlockSpec((1, d), lambda b, i: (0, 0)),
        ],
        out_specs=tok(),
        compiler_params=_cparams(("arbitrary", "arbitrary")),
        name="merge",
    )(ygla, yrw, ys5, ys5, p, p, p, p, resid, mod, w_branch.astype(BF16), w_out.astype(BF16), s5_d.reshape(1, d),
      glu_w.astype(BF16), glu_b.reshape(1, d))


def _router_kernel(x_ref, nw_ref, mod_ref, rw_ref, rb_ref, h_ref, comb_ref, combt_ref, rank_ref, rankt_ref,
                   offs_ref, *, tm, n_ctx, nb):
    b = pl.program_id(0)
    i = pl.program_id(1)
    x = x_ref[0]
    xn = x * lax.rsqrt(jnp.mean(x * x, axis=-1, keepdims=True) + NORM_EPS) * nw_ref[...]
    row = i * tm + lax.broadcasted_iota(jnp.int32, (tm, 1), 0)
    is_ctx = row < n_ctx
    sh = _mod_rows(mod_ref, b, nb, 3 * D_MODEL, is_ctx)
    sc = _mod_rows(mod_ref, b, nb, 4 * D_MODEL, is_ctx)
    h = xn * (1.0 + sc) + sh
    h_ref[0] = h.astype(BF16)
    logits = jnp.dot(h, rw_ref[...], precision=HI, preferred_element_type=F32) + rb_ref[...]
    lane = lax.broadcasted_iota(jnp.int32, logits.shape, 1).astype(F32)
    neg = jnp.float32(-jnp.inf)
    cur = logits
    comb = jnp.zeros_like(logits)
    denom = jnp.zeros((tm, 1), F32)
    top = None
    for kk in range(TOP_K):
        m = jnp.max(cur, axis=-1, keepdims=True)
        idx = jnp.min(jnp.where(cur == m, lane, 128.0), axis=-1, keepdims=True)
        sel = lane == idx
        if kk == 0:
            top = m
        e = jnp.exp(m - top)
        comb = comb + jnp.where(sel, e, 0.0)
        denom = denom + e
        cur = jnp.where(sel, neg, cur)
    comb = comb / denom
    comb_ref[0] = comb

    chosen = jnp.where(comb > 0.0, 1.0, 0.0).astype(BF16)
    ri = lax.broadcasted_iota(jnp.int32, (tm, tm), 0)
    ci = lax.broadcasted_iota(jnp.int32, (tm, tm), 1)
    rank = jnp.dot((ci < ri).astype(BF16), chosen, preferred_element_type=F32)
    rank_ref[0] = rank
    rankt_ref[0] = lax.dot_general(chosen, (ri < ci).astype(BF16), (((0,), (0,)), ((), ())),
                                   preferred_element_type=F32)
    eye = (lax.broadcasted_iota(jnp.int32, (128, 128), 0)
           == lax.broadcasted_iota(jnp.int32, (128, 128), 1)).astype(BF16)
    h1 = comb.astype(BF16)
    r1 = comb - h1.astype(F32)
    h2 = r1.astype(BF16)
    h3 = (r1 - h2.astype(F32)).astype(BF16)
    nt = (((1,), (1,)), ((), ()))
    combt_ref[0] = lax.dot_general(eye, h1, nt, preferred_element_type=F32) \
        + lax.dot_general(eye, h2, nt, preferred_element_type=F32) \
        + lax.dot_general(eye, h3, nt, preferred_element_type=F32)
    tile = tm // MOE_TILES
    rows = [rank[j * tile:j * tile + 1, :] for j in range(MOE_TILES)]
    rows.append(rank[tm - 1:tm, :] + chosen[tm - 1:tm, :].astype(F32))
    rows.append(jnp.zeros((16 - len(rows), 128), F32))
    offs_ref[0] = jnp.concatenate(rows, axis=0)


def _router(resid, norm_w, mod, router_w, router_b, n_ctx):
    nb, t, d = resid.shape
    tm = _row_tile(t, 1088)
    nper = t // tm
    nq = nb * nper
    rw = jnp.pad(router_w, ((0, 0), (0, 128 - N_EXPERTS)))
    rb = jnp.pad(router_b, (0, 128 - N_EXPERTS), constant_values=-1e30).reshape(1, 128)
    kern = functools.partial(_router_kernel, tm=tm, n_ctx=n_ctx, nb=nb)

    def qspec(r, c):
        return pl.BlockSpec((1, r, c), lambda b, i: (b * nper + i, 0, 0))

    return pl.pallas_call(
        kern,
        out_shape=(jax.ShapeDtypeStruct((nb, t, d), BF16),
                   jax.ShapeDtypeStruct((nq, tm, 128), F32), jax.ShapeDtypeStruct((nq, 128, tm), F32),
                   jax.ShapeDtypeStruct((nq, tm, 128), F32), jax.ShapeDtypeStruct((nq, 128, tm), F32),
                   jax.ShapeDtypeStruct((nq, 16, 128), F32)),
        grid=(nb, nper),
        in_specs=[
            pl.BlockSpec((1, tm, d), lambda b, i: (b, i, 0)),
            pl.BlockSpec((1, d), lambda b, i: (0, 0)),
            pl.BlockSpec(mod.shape, lambda b, i: (0, 0)),
            pl.BlockSpec((d, 128), lambda b, i: (0, 0)),
            pl.BlockSpec((1, 128), lambda b, i: (0, 0)),
        ],
        out_specs=(pl.BlockSpec((1, tm, d), lambda b, i: (b, i, 0)),
                   qspec(tm, 128), qspec(128, tm), qspec(tm, 128), qspec(128, tm), qspec(16, 128)),
        compiler_params=_cparams(("arbitrary", "arbitrary")),
        name="router",
    )(resid, norm_w.reshape(1, d), mod, rw, rb)


MOE_TILES = 8
MOE_FFN_ROWS = 160


def _moe_routed_kernel(off_ref, h_ref, comb_ref, combt_ref, rank_ref, rankt_ref, x_ref, mod_ref, wg_ref, bg_ref,
                       wu_ref, bu_ref, wd_ref, bd_ref, o_ref, g_ref, y_ref, *, qt, nper, nb, n_ctx):
    q = pl.program_id(0)
    e = pl.program_id(1)
    ne = pl.num_programs(1)
    tile = qt // MOE_TILES
    win = tile + 8
    fb = MOE_FFN_ROWS
    dot = functools.partial(jnp.dot, preferred_element_type=F32)

    @pl.when(e == 0)
    def _():
        o_ref[...] = jnp.zeros_like(o_ref)
        g_ref[...] = jnp.zeros_like(g_ref)
        y_ref[...] = jnp.zeros_like(y_ref)

    base = (q * ne + e) * (MOE_TILES + 1)
    wrow = combt_ref[0, pl.ds(e, 1), :]
    rrow = rankt_ref[0, pl.ds(e, 1), :]
    sub_i = lax.broadcasted_iota(jnp.int32, (win, 1), 0)
    sub_f = lax.broadcasted_iota(jnp.int32, (win, tile), 0).astype(F32)
    lane_f = lax.broadcasted_iota(jnp.int32, (tile, win), 1).astype(F32)
    lane_e = lax.broadcasted_iota(jnp.int32, (tile, 128), 1) == e

    for j in range(MOE_TILES):
        off = off_ref[base + j]
        nxt = off_ref[base + j + 1]
        ws = pl.multiple_of((off // 8) * 8, 8)
        rj = rrow[:, j * tile:(j + 1) * tile] - ws.astype(F32)
        wj = wrow[:, j * tile:(j + 1) * tile]
        onehot = jnp.where((sub_f == rj) & (wj > 0.0), 1.0, 0.0).astype(BF16)
        new = dot(onehot, h_ref[0, j * tile:(j + 1) * tile, :])
        slot = sub_i + ws
        hit = (slot >= off) & (slot < nxt)
        g_ref[pl.ds(ws, win), :] = jnp.where(hit, new, g_ref[pl.ds(ws, win), :])

    n = off_ref[base + MOE_TILES]

    def ffn(i, carry):
        rows = pl.ds(pl.multiple_of(i * fb, 8), fb)
        x = g_ref[rows, :].astype(BF16)
        gl = jnp.minimum(dot(x, wg_ref[0]) + bg_ref[0], SWIGLU_LIMIT)
        li = jnp.clip(dot(x, wu_ref[0]) + bu_ref[0], -SWIGLU_LIMIT, SWIGLU_LIMIT)
        act = gl * jax.nn.sigmoid(SWIGLU_ALPHA * gl) * (li + 1.0)
        y_ref[rows, :] = dot(act.astype(BF16), wd_ref[0]) + bd_ref[0]
        return carry

    lax.fori_loop(0, (n + fb - 1) // fb, ffn, 0)

    for j in range(MOE_TILES):
        off = off_ref[base + j]
        ws = pl.multiple_of((off // 8) * 8, 8)
        rows = slice(j * tile, (j + 1) * tile)
        wcol = jnp.sum(jnp.where(lane_e, comb_ref[0, rows, :], 0.0), axis=-1, keepdims=True)
        rcol = jnp.sum(jnp.where(lane_e, rank_ref[0, rows, :], 0.0), axis=-1, keepdims=True)
        onehot = jnp.where((lane_f == rcol - ws.astype(F32)) & (wcol > 0.0), 1.0, 0.0).astype(BF16)
        contrib = dot(onehot, y_ref[pl.ds(ws, win), :].astype(BF16))
        o_ref[0, rows, :] += wcol * contrib

    @pl.when(e == ne - 1)
    def _():
        b = q // nper
        row = (q - b * nper) * qt + lax.broadcasted_iota(jnp.int32, (qt, 1), 0)
        gate = _mod_rows(mod_ref, b, nb, 5 * D_MODEL, row < n_ctx)
        o_ref[0] = x_ref[0] + gate * o_ref[0]


def _moe_routed(h, combq, combt, rank, rankt, offs, resid, mod, wg, bg, wu, bu, wd, bd, n_ctx):
    nb, t, d = h.shape
    nq, qt, _ = combq.shape
    assert qt % (8 * MOE_TILES) == 0
    ne = N_EXPERTS
    off = offs[:, :MOE_TILES + 1, :ne].astype(jnp.int32).transpose(0, 2, 1).reshape(-1)
    gr = qt + MOE_FFN_ROWS + 16
    kern = functools.partial(_moe_routed_kernel, qt=qt, nper=t // qt, nb=nb, n_ctx=n_ctx)

    def wspec():
        return pl.BlockSpec((1, d, d), lambda q, e, off: (e, 0, 0))

    def bspec():
        return pl.BlockSpec((1, 1, d), lambda q, e, off: (e, 0, 0))

    def tokspec(w):
        return pl.BlockSpec((1, qt, w), lambda q, e, off: (q, 0, 0))

    def tspec():
        return pl.BlockSpec((1, 128, qt), lambda q, e, off: (q, 0, 0))

    out = pl.pallas_call(
        kern,
        out_shape=jax.ShapeDtypeStruct((nq, qt, d), F32),
        grid_spec=pltpu.PrefetchScalarGridSpec(
            num_scalar_prefetch=1,
            grid=(nq, ne),
            in_specs=[tokspec(d), tokspec(128), tspec(), tokspec(128), tspec(), tokspec(d),
                      pl.BlockSpec(mod.shape, lambda q, e, off: (0, 0)),
                      wspec(), bspec(), wspec(), bspec(), wspec(), bspec()],
            out_specs=tokspec(d),
            scratch_shapes=[pltpu.VMEM((gr, d), F32), pltpu.VMEM((gr, d), F32)]),
        compiler_params=_cparams(("arbitrary", "arbitrary")),
        name="moe_routed",
    )(off, h.reshape(nq, qt, d), combq, combt, rank, rankt, resid.reshape(nq, qt, d), mod,
      wg.astype(BF16), bg.reshape(ne, 1, d),
      wu.astype(BF16), bu.reshape(ne, 1, d), wd.astype(BF16), bd.reshape(ne, 1, d))
    return out.reshape(nb, t, d)


def _final_norm_kernel(x_ref, w_ref, o_ref):
    x = x_ref[0]
    o_ref[0] = x * lax.rsqrt(jnp.mean(x * x, axis=-1, keepdims=True) + NORM_EPS) * w_ref[...]


def _final_norm(resid, w, n_ctx):
    nb, t, d = resid.shape
    tm = n_ctx
    return pl.pallas_call(
        _final_norm_kernel,
        out_shape=jax.ShapeDtypeStruct((nb, t - n_ctx, d), F32),
        grid=(nb, (t - n_ctx) // tm),
        in_specs=[pl.BlockSpec((1, tm, d), lambda b, i: (b, i + 1, 0)),
                  pl.BlockSpec((1, d), lambda b, i: (0, 0))],
        out_specs=pl.BlockSpec((1, tm, d), lambda b, i: (b, i, 0)),
        compiler_params=_cparams(("arbitrary", "arbitrary")),
        name="final_norm",
    )(resid, w.reshape(1, d))


def _pack_w_in(w):
    o_gg = CONV_CH + GLA_V
    o_rg = o_gg + 2 * GLA_GATE_RANK
    o_rw = o_rg + RWKV_G_RANK
    o_ra = o_rw + 2 * RWKV_W_RANK
    o_u = o_ra + 2 * RWKV_A_RANK
    o_gt = o_u + D_MODEL
    pad = jnp.zeros((w.shape[0], 128 - 2 * GLA_GATE_RANK), w.dtype)
    return jnp.concatenate([w[:, :o_gg], w[:, o_u:o_gt], w[:, o_gt:], w[:, o_gg:o_rg], pad,
                            w[:, o_rg:o_rw], w[:, o_rw:o_ra], w[:, o_ra:o_u]], axis=1).astype(BF16)


def kernel(x, c, ctx, c_ctx, w_ada, b_ada, norm_mix, w_in, conv_w, gla_wg2, gla_bg, gla_norm, rw_w0, rw_w2, rw_a0, rw_a2, rw_g2, rw_kk, rw_ka, rw_rk, rw_ln_g, rw_ln_b, s5_lam_re, s5_lam_im, s5_log_dt, s5_b_re, s5_b_im, s5_c_re, s5_c_im, s5_d, s5_glu_w, s5_glu_b, w_branch, w_out, norm_ffn, router_w, router_b, exp_w_gate, exp_b_gate, exp_w_up, exp_b_up, exp_w_down, exp_b_down, final_norm):
    depth = w_ada.shape[0]
    nb, _, d = x.shape
    n_ctx = ctx.shape[1]
    assert d == D_MODEL and n_ctx % RWKV_SEG == 0 and x.shape[1] % RWKV_SEG == 0 and nb < MOD_ROWS

    resid = jnp.concatenate([ctx, x], axis=1).astype(F32)
    cc = jnp.concatenate([c, c_ctx[None], jnp.zeros((MOD_ROWS - nb - 1, d), F32)], axis=0)
    mods = _ada_table(cc, w_ada, b_ada)

    for l in range(depth):
        mod = mods[l]
        p = _norm_in_proj(resid, norm_mix[l], mod, _pack_w_in(w_in[l]), n_ctx)
        pcv = _short_conv(p, conv_w[l], n_ctx)
        y_gla = _gla_mixer(pcv, p, gla_wg2[l], gla_bg[l], gla_norm[l], n_ctx)
        prm = dict(w0=rw_w0[l], w2=rw_w2[l], a0=rw_a0[l], a2=rw_a2[l], g2=rw_g2[l], kk=rw_kk[l], ka=rw_ka[l],
                   rk=rw_rk[l].reshape(-1), ln_g=rw_ln_g[l], ln_b=rw_ln_b[l])
        y_rw = _rwkv_mixer(pcv, p, prm, n_ctx)
        ops = _s5_operators(s5_lam_re[l], s5_lam_im[l], s5_log_dt[l], s5_b_re[l], s5_b_im[l],
                            s5_c_re[l], s5_c_im[l])
        y_s5 = _s5_mixer(p, ops, n_ctx)
        resid = _merge(y_gla, y_rw, y_s5, p, resid, mod, w_branch[l], w_out[l], s5_d[l], s5_glu_w[l],
                       s5_glu_b[l], n_ctx)
        h, *routing = _router(resid, norm_ffn[l], mod, router_w[l], router_b[l], n_ctx)
        resid = _moe_routed(h, *routing, resid, mod, exp_w_gate[l], exp_b_gate[l], exp_w_up[l], exp_b_up[l],
                            exp_w_down[l], exp_b_down[l], n_ctx)
    return _final_norm(resid, final_norm, n_ctx)
```
